```python
import math
import jax
import jax.numpy as jnp
from jax import lax
import numpy as np

D_MODEL = 1024
BATCH = 4
SEQ = 8192
DEPTH = 4
DEC_BATCH = 2
DEC_SEQ = 16384
PAST_LEN = 128

MIX_WIDTH = D_MODEL // 2
N_BRANCH = 3
MLA_HEADS = 8
MLA_NOPE_DIM = MIX_WIDTH // MLA_HEADS
MLA_ROPE_DIM = MLA_NOPE_DIM // 2
MLA_V_DIM = MIX_WIDTH // MLA_HEADS
MLA_Q_LORA = 3 * D_MODEL // 8
MLA_KV_LORA = D_MODEL // 4
ROPE_BASE = 10000.0
Q_BLOCK = 128
ATTN_SCALE = 1.0 / math.sqrt(MLA_NOPE_DIM + MLA_ROPE_DIM)
S5_WIDTH = MIX_WIDTH
S5_GROUP = 16
S5_GROUPS = S5_WIDTH // S5_GROUP
S5_STATE = 64
S5_DT_MIN = 0.001
S5_DT_MAX = 0.1
HG_WIDTH = MIX_WIDTH
HG_HEADS = 4
HG_DK = HG_WIDTH // HG_HEADS
HG_DV = HG_WIDTH // HG_HEADS
HG_CHUNK = 64
D_FF = ((8 * D_MODEL // 3 + 127) // 128) * 128
CONV_WIDTH = 3
NORM_EPS = 1e-6
IN_SIZES = (MLA_Q_LORA, MLA_KV_LORA, MLA_ROPE_DIM, S5_WIDTH, HG_WIDTH, HG_WIDTH, HG_WIDTH, HG_WIDTH, HG_WIDTH, N_BRANCH * D_MODEL)
N_IN = MLA_Q_LORA + MLA_KV_LORA + MLA_ROPE_DIM + S5_WIDTH + 5 * HG_WIDTH + N_BRANCH * D_MODEL

kernel_name = "hybrid_mla_s5_hgrn2_encoder"


def _rmsnorm(x, g):
    xf = x.astype(jnp.float32)
    y = xf * lax.rsqrt(jnp.mean(xf * xf, axis=-1, keepdims=True) + NORM_EPS)
    return (y * g.astype(jnp.float32)).astype(x.dtype)


def _split_cols(t, sizes):
    idx, acc = [], 0
    for s in sizes[:-1]:
        acc += s
        idx.append(acc)
    return jnp.split(t, idx, axis=-1)


def _rotate(x, cos, sin):
    x1, x2 = jnp.split(x, 2, axis=-1)
    return jnp.concatenate([x1 * cos - x2 * sin, x1 * sin + x2 * cos], axis=-1)


def _mla_attention(q_nope, q_rope, k_nope, k_rope, v):
    b, l, h, _ = q_nope.shape
    nb = l // Q_BLOCK

    def blocks(t):
        return t.reshape((b, nb, Q_BLOCK) + t.shape[2:]).swapaxes(0, 1)

    def attend(qb):
        qn, qr = qb
        s = jnp.einsum('bqhd,bkhd->bhqk', qn, k_nope, preferred_element_type=jnp.float32)
        s = s + jnp.einsum('bqhd,bkd->bhqk', qr, k_rope, preferred_element_type=jnp.float32)
        p = jax.nn.softmax(s * ATTN_SCALE, axis=-1).astype(v.dtype)
        return jnp.einsum('bhqk,bkhd->bqhd', p, v)

    o = lax.map(attend, (blocks(q_nope), blocks(q_rope)))
    return o.swapaxes(0, 1).reshape(b, l, h * v.shape[-1])


def _ssm_combine(e_i, e_j):
    a_i, b_i = e_i
    a_j, b_j = e_j
    return a_j * a_i, a_j * b_i + b_j


def _s5_direction(u_c, lam_re, lam_im, log_dt, b_re, b_im, c_re, c_im, reverse):
    f32 = jnp.float32
    lam = lax.complex(lam_re.astype(f32), lam_im.astype(f32))
    dt = jnp.exp(log_dt.astype(f32))[:, None]
    lam_bar = jnp.exp(lam * dt)
    b_bar = ((lam_bar - 1.0) / lam)[..., None] * lax.complex(b_re.astype(f32), b_im.astype(f32))
    bu = jnp.einsum('blgh,gph->blgp', u_c, b_bar)
    a = jnp.broadcast_to(lam_bar, bu.shape)
    _, xs = lax.associative_scan(_ssm_combine, (a, bu), axis=1, reverse=reverse)
    c = lax.complex(c_re.astype(f32), c_im.astype(f32))
    return jnp.real(jnp.einsum('blgp,ghp->blgh', xs, c))


def _hgrn2_chunked(q, k, v, logf):
    b, l, h, dk = q.shape
    dv = v.shape[-1]
    nc = l // HG_CHUNK

    def to_chunks(t):
        return t.reshape(b, nc, HG_CHUNK, h, t.shape[-1]).transpose(1, 0, 3, 2, 4)

    mask = jnp.tril(jnp.ones((HG_CHUNK, HG_CHUNK), dtype=bool))[:, :, None]

    def step(state, inp):
        qc, kc, vc, gc = inp
        cb = jnp.cumsum(gc, axis=2)
        diff = cb[:, :, :, None, :] - cb[:, :, None, :, :]
        decay = jnp.exp(jnp.where(mask, diff, -jnp.inf))
        scores = jnp.einsum('bhtd,bhtsd,bhsd->bhts', qc, decay, kc)
        o = jnp.einsum('bhts,bhse->bhte', scores, vc)
        o = o + jnp.einsum('bhtd,bhde->bhte', qc * jnp.exp(cb), state)
        last = cb[:, :, -1:, :]
        state = jnp.exp(last[:, :, 0, :])[..., None] * state + jnp.einsum('bhsd,bhse->bhde', kc * jnp.exp(last - cb), vc)
        return state, o

    s0 = jnp.zeros((b, h, dk, dv), jnp.float32)
    _, o = lax.scan(step, s0, (to_chunks(q), to_chunks(k), to_chunks(v), to_chunks(logf)))
    return o.transpose(1, 0, 3, 2, 4).reshape(b, l, h, dv)


def _dwconv3(x, w, bias):
    y = lax.conv_general_dilated(x, w[:, None, :], window_strides=(1,), padding=((1, 1),),
                                 dimension_numbers=('NWC', 'WIO', 'NWC'), feature_group_count=x.shape[-1])
    return y + bias


def _mixer(h, l, p, cos, sin, lb_f, lb_b):
    b, L, _ = h.shape
    f32 = jnp.float32
    cq, ckv, kr, u, hq, hf_f, hf_b, hi, hog, gate_pre = _split_cols(h @ p['w_in'][l], IN_SIZES)

    q = (_rmsnorm(cq, p['g_q_lat'][l]) @ p['w_q_up'][l]).reshape(b, L, MLA_HEADS, MLA_NOPE_DIM + MLA_ROPE_DIM)
    q_nope = q[..., :MLA_NOPE_DIM]
    q_rope = _rotate(q[..., MLA_NOPE_DIM:], cos[None, :, None], sin[None, :, None])
    kv = (_rmsnorm(ckv, p['g_kv_lat'][l]) @ p['w_kv_up'][l]).reshape(b, L, MLA_HEADS, MLA_NOPE_DIM + MLA_V_DIM)
    k_nope, v = kv[..., :MLA_NOPE_DIM], kv[..., MLA_NOPE_DIM:]
    k_rope = _rotate(kr, cos[None], sin[None])
    z_a = _mla_attention(q_nope, q_rope, k_nope, k_rope, v)

    uf = u.astype(f32)
    u_c = uf.reshape(b, L, S5_GROUPS, S5_GROUP).astype(jnp.complex64)
    y = uf * p['s5_d'][l].astype(f32)
    for d in range(2):
        y = y + _s5_direction(u_c, p['s5_lam_re'][l, d], p['s5_lam_im'][l, d], p['s5_log_dt'][l, d],
                              p['s5_b_re'][l, d], p['s5_b_im'][l, d], p['s5_c_re'][l, d], p['s5_c_im'][l, d],
                              d == 1).reshape(b, L, S5_WIDTH)
    g = jax.nn.gelu(y)
    z_b = (g * jax.nn.sigmoid(g @ p['w_glu'][l].astype(f32) + p['b_glu'][l].astype(f32))).astype(h.dtype)

    def heads(t, dh):
        return t.astype(f32).reshape(b, L, HG_HEADS, dh)

    def log_forget(pre, lb):
        lb = lb.reshape(HG_HEADS, HG_DK)
        return jnp.log(lb + (1.0 - lb) * jax.nn.sigmoid(heads(pre, HG_DK)))

    def rev(t):
        return jnp.flip(t, axis=1)

    qh = jax.nn.silu(heads(hq, HG_DK))
    vh = heads(hi, HG_DV)
    lf_f = log_forget(hf_f, lb_f)
    lf_b = log_forget(hf_b, lb_b)
    o_f = _hgrn2_chunked(qh, -jnp.expm1(lf_f), vh, lf_f)
    o_b = rev(_hgrn2_chunked(rev(qh), rev(-jnp.expm1(lf_b)), rev(vh), rev(lf_b)))
    o = _rmsnorm(o_f + o_b, p['g_hg_out'][l].reshape(HG_HEADS, HG_DV)).reshape(b, L, HG_WIDTH)
    z_c = (o * jax.nn.silu(hog.astype(f32))).astype(h.dtype)

    z = jnp.stack([z_a, z_b, z_c], axis=2)
    branch = jnp.einsum('blkc,kcd->blkd', z, p['w_branch'][l])
    gates = jax.nn.sigmoid(gate_pre.reshape(b, L, N_BRANCH, D_MODEL))
    return jnp.sum(gates * branch, axis=2) @ p['w_out'][l]


def _ffn(h, l, p):
    up = _dwconv3(h @ p['w_ffn_up'][l], p['w_ffn_conv'][l], p['b_ffn_conv'][l])
    a, val = jnp.split(up, 2, axis=-1)
    return (jax.nn.silu(a) * val) @ p['w_ffn_down'][l]


def _trunk(x, c, p):
    L = x.shape[1]
    inv_freq = 1.0 / (ROPE_BASE ** (jnp.arange(0, MLA_ROPE_DIM, 2, dtype=jnp.float32) / MLA_ROPE_DIM))
    ang = jnp.arange(L, dtype=jnp.float32)[:, None] * inv_freq[None, :]
    cos = jnp.cos(ang).astype(x.dtype)
    sin = jnp.sin(ang).astype(x.dtype)
    gam = jax.nn.softmax(p['hg_lb_logits'].astype(jnp.float32), axis=1)
    lb = jnp.cumsum(gam, axis=1) - gam[:, :1]
    c_act = jax.nn.silu(c)
    for l in range(DEPTH):
        mod = c_act @ p['w_ada'][l] + p['b_ada'][l]
        sh1, sc1, g1, sh2, sc2, g2 = jnp.split(mod[:, None, :], 6, axis=-1)
        h = _rmsnorm(x, p['g_mix'][l]) * (1.0 + sc1) + sh1
        x = x + g1 * _mixer(h, l, p, cos, sin, lb[0, l], lb[1, l])
        h = _rmsnorm(x, p['g_ffn'][l]) * (1.0 + sc2) + sh2
        x = x + g2 * _ffn(h, l, p)
    return _rmsnorm(x, p['g_final'])


def setup_inputs(seed: int = 0) -> dict:
    key = jax.random.key(seed)
    keys = list(jax.random.split(key, 32))

    def nrm(shape, scale):
        return scale * jax.random.normal(keys.pop(), shape, jnp.float32)

    def gain(shape):
        return 1.0 + nrm(shape, 0.01)

    G, P, GS = S5_GROUPS, S5_STATE, S5_GROUP
    lam_im = jnp.broadcast_to(math.pi * jnp.arange(P, dtype=jnp.float32), (DEPTH, 2, G, P))
    log_dt = jax.random.uniform(keys.pop(), (DEPTH, 2, G), jnp.float32, math.log(S5_DT_MIN), math.log(S5_DT_MAX))
    return {
        'x_prompt': nrm((BATCH, SEQ, D_MODEL), 1.0),
        'x_sample': nrm((DEC_BATCH, DEC_SEQ, D_MODEL), 1.0),
        'c_prompt': nrm((BATCH, D_MODEL), 1.0),
        'c_sample': nrm((DEC_BATCH, D_MODEL), 1.0),
        'w_ada': nrm((DEPTH, D_MODEL, 6 * D_MODEL), 0.5 * D_MODEL ** -0.5),
        'b_ada': nrm((DEPTH, 6 * D_MODEL), 0.01),
        'g_mix': gain((DEPTH, D_MODEL)),
        'w_in': nrm((DEPTH, D_MODEL, N_IN), D_MODEL ** -0.5),
        'g_q_lat': gain((DEPTH, MLA_Q_LORA)),
        'w_q_up': nrm((DEPTH, MLA_Q_LORA, MLA_HEADS * (MLA_NOPE_DIM + MLA_ROPE_DIM)), MLA_Q_LORA ** -0.5),
        'g_kv_lat': gain((DEPTH, MLA_KV_LORA)),
        'w_kv_up': nrm((DEPTH, MLA_KV_LORA, MLA_HEADS * (MLA_NOPE_DIM + MLA_V_DIM)), MLA_KV_LORA ** -0.5),
        's5_lam_re': -0.5 + nrm((DEPTH, 2, G, P), 0.01),
        's5_lam_im': lam_im,
        's5_log_dt': log_dt,
        's5_b_re': nrm((DEPTH, 2, G, P, GS), (0.5 / GS) ** 0.5),
        's5_b_im': nrm((DEPTH, 2, G, P, GS), (0.5 / GS) ** 0.5),
        's5_c_re': nrm((DEPTH, 2, G, GS, P), (0.5 / P) ** 0.5),
        's5_c_im': nrm((DEPTH, 2, G, GS, P), (0.5 / P) ** 0.5),
        's5_d': nrm((DEPTH, S5_WIDTH), 1.0),
        'w_glu': nrm((DEPTH, S5_WIDTH, S5_WIDTH), S5_WIDTH ** -0.5),
        'b_glu': nrm((DEPTH, S5_WIDTH), 0.01),
        'hg_lb_logits': nrm((2, DEPTH, HG_WIDTH), 0.1),
        'g_hg_out': gain((DEPTH, HG_WIDTH)),
        'w_branch': nrm((DEPTH, N_BRANCH, MIX_WIDTH, D_MODEL), MIX_WIDTH ** -0.5),
        'w_out': nrm((DEPTH, D_MODEL, D_MODEL), D_MODEL ** -0.5),
        'g_ffn': gain((DEPTH, D_MODEL)),
        'w_ffn_up': nrm((DEPTH, D_MODEL, 2 * D_FF), D_MODEL ** -0.5),
        'w_ffn_conv': nrm((DEPTH, CONV_WIDTH, 2 * D_FF), CONV_WIDTH ** -0.5),
        'b_ffn_conv': nrm((DEPTH, 2 * D_FF), 0.01),
        'w_ffn_down': nrm((DEPTH, D_FF, D_MODEL), D_FF ** -0.5),
        'g_final': gain((D_MODEL,)),
    }


def reference(x_prompt, x_sample, c_prompt, c_sample, w_ada, b_ada, g_mix, w_in, g_q_lat, w_q_up, g_kv_lat,
              w_kv_up, s5_lam_re, s5_lam_im, s5_log_dt, s5_b_re, s5_b_im, s5_c_re, s5_c_im, s5_d, w_glu, b_glu,
              hg_lb_logits, g_hg_out, w_branch, w_out, g_ffn, w_ffn_up, w_ffn_conv, b_ffn_conv, w_ffn_down, g_final):
    p = dict(w_ada=w_ada, b_ada=b_ada, g_mix=g_mix, w_in=w_in, g_q_lat=g_q_lat, w_q_up=w_q_up,
             g_kv_lat=g_kv_lat, w_kv_up=w_kv_up, s5_lam_re=s5_lam_re, s5_lam_im=s5_lam_im,
             s5_log_dt=s5_log_dt, s5_b_re=s5_b_re, s5_b_im=s5_b_im, s5_c_re=s5_c_re, s5_c_im=s5_c_im,
             s5_d=s5_d, w_glu=w_glu, b_glu=b_glu, hg_lb_logits=hg_lb_logits, g_hg_out=g_hg_out,
             w_branch=w_branch, w_out=w_out, g_ffn=g_ffn, w_ffn_up=w_ffn_up, w_ffn_conv=w_ffn_conv,
             b_ffn_conv=b_ffn_conv, w_ffn_down=w_ffn_down, g_final=g_final)
    y_prompt = _trunk(x_prompt, c_prompt, p)
    y_sample = _trunk(x_sample, c_sample, p)
    return (y_prompt, y_sample)
```

```python
import functools
import math

import jax
import jax.numpy as jnp
from jax import lax
from jax.experimental import pallas as pl
from jax.experimental.pallas import tpu as pltpu

F32 = jnp.float32
BF16 = jnp.bfloat16

D_MODEL = 1024
DEPTH = 4
MIX = 512
N_HEADS = 8
NOPE = 64
ROPE = 32
Q_LORA = 384
KV_LORA = 256
ROPE_BASE = 10000.0
S5_GROUPS = 32
S5_GROUP = 16
S5_STATE = 64
S5_CHUNK = 128
HG_HEADS = 4
HG_D = 128
HG_CHUNK = 64
HG_BLOCK = 16
D_FF = 2816
EPS = 1e-6
ATTN_SCALE = 1.0 / math.sqrt(NOPE + ROPE)
LOG2E = 1.4426950408889634
IN_OFFS = (0, 384, 640, 672, 1184, 1696, 2208, 2720, 3232, 3744, 6816)

LANE = 128
VMEM_LIMIT = 56 * 1024 * 1024

TL_IN = 256
TQ = 256
TK = 512
TL_HG = 256
TL_MG = 256
TL_FF = 256
FF_CW = 256
HALO = 8


def _dot(a, b):
    return jnp.dot(a, b, preferred_element_type=F32)


def _dot_nt(a, b):
    return lax.dot_general(a, b, (((1,), (1,)), ((), ())), preferred_element_type=F32)


def _dot_tn(a, b):
    return lax.dot_general(a, b, (((0,), (0,)), ((), ())), preferred_element_type=F32)


def _sigmoid(x):
    return 1.0 / (1.0 + jnp.exp(-x))


def _rms(x):
    return x * lax.rsqrt(jnp.mean(x * x, axis=-1, keepdims=True) + EPS)


def _gelu_tanh(x):
    return 0.5 * x * (1.0 + jnp.tanh(math.sqrt(2.0 / math.pi) * (x + 0.044715 * (x * x * x))))


def _params(*sem):
    return pltpu.CompilerParams(dimension_semantics=sem, vmem_limit_bytes=VMEM_LIMIT)


def _const_spec(shape):
    nd = len(shape)
    return pl.BlockSpec(shape, lambda *_: (0,) * nd, pipeline_mode=pl.Buffered(1))


def _ada_kernel(c_ref, w_ref, b_ref, o_ref):
    c = c_ref[...]
    a = (c * _sigmoid(c)).astype(BF16)
    o_ref[0] = _dot(a, w_ref[0].astype(BF16)) + b_ref[0]


def _ada_call(c_all, w_ada, b_ada):
    nb = 1536
    rows = c_all.shape[0]
    return pl.pallas_call(
        _ada_kernel,
        grid=(DEPTH, 6 * D_MODEL // nb),
        in_specs=[pl.BlockSpec((rows, D_MODEL), lambda l, j: (0, 0)),
                  pl.BlockSpec((1, D_MODEL, nb), lambda l, j: (l, 0, j)),
                  pl.BlockSpec((1, 1, nb), lambda l, j: (l, 0, j))],
        out_specs=pl.BlockSpec((1, rows, nb), lambda l, j: (l, 0, j)),
        out_shape=jax.ShapeDtypeStruct((DEPTH, rows, 6 * D_MODEL), F32),
        compiler_params=_params("arbitrary", "arbitrary"),
        name="ada_mod",
    )(c_all, w_ada, b_ada.reshape(DEPTH, 1, 6 * D_MODEL))


def _inproj_kernel(x_ref, mod_ref, gmix_ref, cos_ref, sin_ref,
                   wlat_ref, wkr_ref, wut_ref, whqv_ref, whog_ref, whff_ref, wgate_ref,
                   gq_ref, gkv_ref, wqa_ref, wqb_ref, wk_ref, wvt_ref,
                   q_ref, k_ref, vt_ref, ut_ref, hqv_ref, hog_ref, hff_ref, gp_ref):
    mod = mod_ref[0]
    h = _rms(x_ref[0]) * gmix_ref[...] * (1.0 + mod[1:2]) + mod[0:1]
    hb = h.astype(BF16)
    lat = _dot(hb, wlat_ref[...])
    qn = (_rms(lat[:, :Q_LORA]) * gq_ref[...]).astype(BF16)
    kvn = (_rms(lat[:, Q_LORA:]) * gkv_ref[...]).astype(BF16)
    cos = cos_ref[...]
    sin = sin_ref[...]
    qa = _dot(qn, wqa_ref[...])
    qb = _dot(qn, wqb_ref[...])
    krp = _dot(hb, wkr_ref[...])
    kr = krp[:, :LANE] * cos + krp[:, LANE:] * sin
    kn = _dot(kvn, wk_ref[...])
    for hd in range(N_HEADS):
        sl = slice(hd * LANE, (hd + 1) * LANE)
        q_ref[0, :, sl] = ((qa[:, sl] * cos + qb[:, sl] * sin) * (ATTN_SCALE * LOG2E)).astype(BF16)
        k_ref[0, :, sl] = (kn[:, sl] + kr).astype(BF16)
    vt_ref[0] = _dot_nt(wvt_ref[...], kvn).astype(BF16)
    ut_ref[0] = _dot_nt(wut_ref[...], hb).astype(BF16)
    hqv_ref[0] = _dot(hb, whqv_ref[...]).astype(BF16)
    hog_ref[0] = _dot(hb, whog_ref[...]).astype(BF16)
    hff_ref[0] = _dot(hb, whff_ref[...])
    gp_ref[0] = _dot(hb, wgate_ref[...]).astype(BF16)


def _inproj_call(x, mod, w, cos_t, sin_t):
    b, L, _ = x.shape
    tl = TL_IN
    tok = lambda n: pl.BlockSpec((1, tl, n), lambda bi, i: (bi, i, 0))
    tr = lambda n: pl.BlockSpec((1, n, tl), lambda bi, i: (bi, 0, i))
    weights = [w["gmix"], None, None, w["wlat"], w["wkr"], w["wut"], w["whqv"], w["whog"], w["whff"], w["wgate"],
               w["gq"], w["gkv"], w["wqa"], w["wqb"], w["wk"], w["wvt"]]
    in_specs = [tok(D_MODEL), pl.BlockSpec((1, 8, D_MODEL), lambda bi, i: (bi, 0, 0))]
    args = [x, mod]
    for a in weights:
        if a is None:
            continue
        in_specs.append(_const_spec(a.shape))
        args.append(a)
    in_specs[3:3] = [pl.BlockSpec((tl, LANE), lambda bi, i: (i, 0))] * 2
    args[3:3] = [cos_t, sin_t]
    out_shape = [jax.ShapeDtypeStruct((b, L, 1024), BF16), jax.ShapeDtypeStruct((b, L, 1024), BF16),
                 jax.ShapeDtypeStruct((b, MIX, L), BF16), jax.ShapeDtypeStruct((b, MIX, L), BF16),
                 jax.ShapeDtypeStruct((b, L, 1024), BF16), jax.ShapeDtypeStruct((b, L, MIX), BF16),
                 jax.ShapeDtypeStruct((b, L, 1024), F32), jax.ShapeDtypeStruct((b, L, 3 * D_MODEL), BF16)]
    out_specs = [tok(1024), tok(1024), tr(MIX), tr(MIX), tok(1024), tok(MIX), tok(1024), tok(3 * D_MODEL)]
    return pl.pallas_call(
        _inproj_kernel, grid=(b, L // tl), in_specs=in_specs, out_specs=out_specs, out_shape=out_shape,
        compiler_params=_params("parallel", "parallel"), name="inproj",
    )(*args)


def _flash_kernel(q_ref, k_ref, vt_ref, o_ref, *, tk, nk):
    tq = q_ref.shape[1]
    half = NOPE

    def body(j, carry):
        off = pl.multiple_of(j * tk, tk)
        new = []
        for hh in range(2):
            m, l, acc = carry[hh]
            q = q_ref[0, :, hh * LANE:(hh + 1) * LANE]
            k = k_ref[0, pl.ds(off, tk), hh * LANE:(hh + 1) * LANE]
            s = _dot_nt(k, q)
            mn = jnp.maximum(m, jnp.max(s, axis=0, keepdims=True))
            p = jnp.exp2(s - mn)
            a = jnp.exp2(m - mn)
            l2 = a * l + jnp.sum(p, axis=0, keepdims=True)
            v = vt_ref[0, hh * half:(hh + 1) * half, pl.ds(off, tk)]
            acc2 = a * acc + _dot(v, p.astype(BF16))
            new.append((mn, l2, acc2))
        return tuple(new)

    init = tuple((jnp.full((1, tq), -1e30, F32), jnp.zeros((1, tq), F32), jnp.zeros((half, tq), F32))
                 for _ in range(2))
    res = lax.fori_loop(0, nk, body, init)
    for hh in range(2):
        o_ref[0, hh * half:(hh + 1) * half, :] = (res[hh][2] / res[hh][1]).astype(BF16)


def _flash_call(q, k, vt):
    b, L, _ = q.shape
    tq, tk = min(TQ, L), min(TK, L)
    return pl.pallas_call(
        functools.partial(_flash_kernel, tk=tk, nk=L // tk),
        grid=(b, N_HEADS // 2, L // tq),
        in_specs=[pl.BlockSpec((1, tq, 2 * LANE), lambda bi, p, i: (bi, i, p)),
                  pl.BlockSpec((1, L, 2 * LANE), lambda bi, p, i: (bi, 0, p)),
                  pl.BlockSpec((1, LANE, L), lambda bi, p, i: (bi, p, 0))],
        out_specs=pl.BlockSpec((1, LANE, tq), lambda bi, p, i: (bi, p, i)),
        out_shape=jax.ShapeDtypeStruct((b, MIX, L), BF16),
        compiler_params=_params("parallel", "parallel", "arbitrary"), name="flash",
    )(q, k, vt)


def _s5_kernel(u_ref, t_ref, f_ref, e_ref, a_ref, y_ref, sloc_ref, st_ref, *, nb, nct):
    m = nb * nct
    ucat = jnp.concatenate([u_ref[:, 0, hi].reshape(m, S5_CHUNK) for hi in range(S5_GROUP)], axis=1)
    y = _dot(ucat, t_ref[0])
    sloc_ref[...] = _dot(ucat, f_ref[0])
    a = a_ref[0]
    arf, aif, arb, aib = a[0:1], a[1:2], a[2:3], a[3:4]

    sb = 8 if nct % 8 == 0 else nct
    nblk = nct // sb

    def body(blk, carry):
        new = []
        for bi in range(nb):
            xrf, xif, xrb, xib = carry[bi]
            rf = pl.multiple_of(bi * nct + blk * sb, sb)
            rb = pl.multiple_of(bi * nct + (nblk - 1 - blk) * sb, sb)
            lf = sloc_ref[pl.ds(rf, sb), 0:2 * LANE]
            lb = sloc_ref[pl.ds(rb, sb), 2 * LANE:4 * LANE]
            frows, brows = [], []
            for r in range(sb):
                frows.append((xrf, xif))
                xrf, xif = (arf * xrf - aif * xif + lf[r:r + 1, 0:LANE],
                            arf * xif + aif * xrf + lf[r:r + 1, LANE:2 * LANE])
            for r in range(sb - 1, -1, -1):
                brows.append((xrb, xib))
                xrb, xib = (arb * xrb - aib * xib + lb[r:r + 1, 0:LANE],
                            arb * xib + aib * xrb + lb[r:r + 1, LANE:2 * LANE])
            brows = brows[::-1]
            st_ref[pl.ds(rf, sb), 0:LANE] = jnp.concatenate([t[0] for t in frows], axis=0)
            st_ref[pl.ds(rf, sb), LANE:2 * LANE] = jnp.concatenate([t[1] for t in frows], axis=0)
            st_ref[pl.ds(rb, sb), 2 * LANE:3 * LANE] = jnp.concatenate([t[0] for t in brows], axis=0)
            st_ref[pl.ds(rb, sb), 3 * LANE:4 * LANE] = jnp.concatenate([t[1] for t in brows], axis=0)
            new.append((xrf, xif, xrb, xib))
        return tuple(new)

    z = jnp.zeros((1, LANE), F32)
    lax.fori_loop(0, nblk, body, tuple((z, z, z, z) for _ in range(nb)))
    y = y + _dot(st_ref[...].astype(BF16), e_ref[0])
    for ho in range(S5_GROUP):
        y_ref[:, 0, ho] = y[:, ho * S5_CHUNK:(ho + 1) * S5_CHUNK].reshape(nb, nct, S5_CHUNK)


def _s5_call(ut, ops):
    b, _, L = ut.shape
    nct = L // S5_CHUNK
    u5 = ut.reshape(b, S5_GROUPS, S5_GROUP, nct, S5_CHUNK)
    gw = S5_GROUP * S5_CHUNK
    blk = pl.BlockSpec((b, 1, S5_GROUP, nct, S5_CHUNK), lambda g: (0, g, 0, 0, 0))
    y = pl.pallas_call(
        functools.partial(_s5_kernel, nb=b, nct=nct),
        grid=(S5_GROUPS,),
        in_specs=[blk,
                  pl.BlockSpec((1, gw, gw), lambda g: (g, 0, 0)),
                  pl.BlockSpec((1, gw, 4 * LANE), lambda g: (g, 0, 0)),
                  pl.BlockSpec((1, 4 * LANE, gw), lambda g: (g, 0, 0)),
                  pl.BlockSpec((1, 8, LANE), lambda g: (g, 0, 0))],
        out_specs=blk,
        out_shape=jax.ShapeDtypeStruct(u5.shape, F32),
        scratch_shapes=[pltpu.VMEM((b * nct, 4 * LANE), F32), pltpu.VMEM((b * nct, 4 * LANE), F32)],
        compiler_params=_params("parallel"), name="s5",
    )(u5, ops["t"], ops["f"], ops["e"], ops["a"])
    return y.reshape(b, MIX, L)


def _s5_operators(lam_re, lam_im, log_dt, b_re, b_im, c_re, c_im, d):
    hp = lax.Precision.HIGHEST
    G, P, GS, C = S5_GROUPS, S5_STATE, S5_GROUP, S5_CHUNK
    lam_re, lam_im = lam_re.astype(F32), lam_im.astype(F32)
    dt = jnp.exp(log_dt.astype(F32))[..., None]
    zr, zi = lam_re * dt, lam_im * dt
    kk = jnp.arange(C + 1, dtype=F32)[None, None, :, None]
    mag = jnp.exp(zr[:, :, None, :] * kk)
    ang = zi[:, :, None, :] * kk
    pw_re, pw_im = mag * jnp.cos(ang), mag * jnp.sin(ang)
    lb_re, lb_im = pw_re[:, :, 1], pw_im[:, :, 1]
    den = lam_re * lam_re + lam_im * lam_im
    nr, ni = lb_re - 1.0, lb_im
    cr = (nr * lam_re + ni * lam_im) / den
    ci = (ni * lam_re - nr * lam_im) / den
    bb_re = cr[..., None] * b_re - ci[..., None] * b_im
    bb_im = cr[..., None] * b_im + ci[..., None] * b_re
    c_re, c_im = c_re.astype(F32), c_im.astype(F32)

    def kern(dr):
        pr, pi_ = pw_re[dr, :, :C], pw_im[dr, :, :C]
        cp_re = c_re[dr][:, None] * pr[:, :, None, :] - c_im[dr][:, None] * pi_[:, :, None, :]
        cp_im = c_re[dr][:, None] * pi_[:, :, None, :] + c_im[dr][:, None] * pr[:, :, None, :]
        return (jnp.einsum('gkhp,gpi->gkhi', cp_re, bb_re[dr], precision=hp)
                - jnp.einsum('gkhp,gpi->gkhi', cp_im, bb_im[dr], precision=hp))

    kf, kb = kern(0), kern(1)
    k0 = kf[:, 0] + kb[:, 0] + jnp.eye(GS, dtype=F32)[None] * d.astype(F32).reshape(G, GS)[:, :, None]
    kfull = jnp.concatenate([kb[:, 1:][:, ::-1], k0[:, None], kf[:, 1:], jnp.zeros((G, 1, GS, GS), F32)], axis=1)
    w = kfull.transpose(0, 3, 2, 1).astype(BF16)
    tiled = jnp.broadcast_to(w[:, :, :, None, :], (G, GS, GS, C, 2 * C)).reshape(G, GS, GS, C * 2 * C)
    mt = tiled[..., :C * (2 * C - 1)].reshape(G, GS, GS, C, 2 * C - 1)[..., C - 1:]
    t_op = mt.transpose(0, 1, 3, 2, 4).reshape(G, GS * C, GS * C)

    def f_part(dr, idx):
        pr, pi_ = pw_re[dr][:, idx], pw_im[dr][:, idx]
        br, bi = bb_re[dr].transpose(0, 2, 1), bb_im[dr].transpose(0, 2, 1)
        re = pr[:, None] * br[:, :, None] - pi_[:, None] * bi[:, :, None]
        im = pr[:, None] * bi[:, :, None] + pi_[:, None] * br[:, :, None]
        return re.reshape(G, GS * C, P), im.reshape(G, GS * C, P)

    tau = jnp.arange(C)
    ffr, ffi = f_part(0, C - 1 - tau)
    fbr, fbi = f_part(1, tau)
    padl = lambda a: jnp.pad(a, ((0, 0), (0, 0), (0, LANE - P)))
    f_op = jnp.concatenate([padl(ffr), padl(ffi), padl(fbr), padl(fbi)], axis=-1).astype(BF16)

    def e_part(dr, idx):
        pr, pi_ = pw_re[dr][:, idx], pw_im[dr][:, idx]
        cr_, ci_ = c_re[dr].transpose(0, 2, 1), c_im[dr].transpose(0, 2, 1)
        prt, pit = pr.transpose(0, 2, 1), pi_.transpose(0, 2, 1)
        re = cr_[:, :, :, None] * prt[:, :, None, :] - ci_[:, :, :, None] * pit[:, :, None, :]
        im = cr_[:, :, :, None] * pit[:, :, None, :] + ci_[:, :, :, None] * prt[:, :, None, :]
        return re.reshape(G, P, GS * C), -im.reshape(G, P, GS * C)

    efr, efi = e_part(0, tau + 1)
    ebr, ebi = e_part(1, C - tau)
    padr = lambda a: jnp.pad(a, ((0, 0), (0, LANE - P), (0, 0)))
    e_op = jnp.concatenate([padr(efr), padr(efi), padr(ebr), padr(ebi)], axis=1).astype(BF16)

    arows = [pw_re[0, :, C], pw_im[0, :, C], pw_re[1, :, C], pw_im[1, :, C]]
    a_op = jnp.stack([jnp.pad(r, ((0, 0), (0, LANE - P))) for r in arows]
                     + [jnp.zeros((G, LANE), F32)] * 4, axis=1)
    return dict(t=t_op, f=f_op, e=e_op, a=a_op)


def _split3(x):
    hi = x.astype(BF16)
    r1 = x - hi.astype(F32)
    mid = r1.astype(BF16)
    lo = (r1 - mid.astype(F32)).astype(BF16)
    return hi, mid, lo


def _hg_consts(fwd):
    c, bk = HG_CHUNK, HG_BLOCK
    t = lax.broadcasted_iota(jnp.int32, (c, c), 0)
    s = lax.broadcasted_iota(jnp.int32, (c, c), 1)
    bt, bs = t // bk, s // bk
    if fwd:
        tri = (s <= t)
        m0 = (bt == bs) & (s <= t)
        m1 = ((bt == 1) & (bs == 0)) | ((bt == 3) & (bs == 2))
        m2 = (bt >= 2) & (bs < 2)
    else:
        tri = (s >= t)
        m0 = (bt == bs) & (s >= t)
        m1 = ((bt == 0) & (bs == 1)) | ((bt == 2) & (bs == 3))
        m2 = (bt < 2) & (bs >= 2)
    return jnp.where(tri, 1.0, 0.0).astype(BF16), m0, m1, m2


def _hg_chunk(hq, hv, hf, lb, st, fwd, consts):
    tri, m0, m1, m2 = consts
    bk, nblk = HG_BLOCK, HG_CHUNK // HG_BLOCK
    q = hq * _sigmoid(hq)
    f = lb + (1.0 - lb) * _sigmoid(hf)
    logf = jnp.log(f)
    k = 1.0 - f
    p1, p2, p3 = _split3(logf)
    cum = _dot(tri, p1) + _dot(tri, p2) + _dot(tri, p3)
    cexc = cum - logf
    zero = jnp.zeros((bk, HG_D), F32)
    cumb = [cum[bk * i:bk * (i + 1)] for i in range(nblk)]
    edge = 0 if fwd else bk - 1

    def ref(row):
        return jnp.broadcast_to(cexc[row:row + 1, :], (bk, HG_D))

    def levels():
        r0 = [ref(bk * i + edge) for i in range(nblk)]
        yield ([cumb[i] - r0[i] for i in range(nblk)], [r0[i] - cumb[i] for i in range(nblk)], m0)
        late1, early1 = ((1, 3), (0, 2)) if fwd else ((0, 2), (1, 3))
        eq, ek = [zero] * nblk, [zero] * nblk
        for lt, er in zip(late1, early1):
            r = ref(bk * lt + edge)
            eq[lt] = cumb[lt] - r
            ek[er] = r - cumb[er]
        yield (eq, ek, m1)
        late2, early2 = ((2, 3), (0, 1)) if fwd else ((1, 0), (2, 3))
        r = ref(bk * late2[0] + edge)
        eq, ek = [zero] * nblk, [zero] * nblk
        for lt in late2:
            eq[lt] = cumb[lt] - r
        for er in early2:
            ek[er] = r - cumb[er]
        yield (eq, ek, m2)

    a = jnp.zeros((HG_CHUNK, HG_CHUNK), F32)
    for eq, ek, msk in levels():
        qs = (q * jnp.exp(jnp.concatenate(eq, axis=0))).astype(BF16)
        ks = (k * jnp.exp(jnp.concatenate(ek, axis=0))).astype(BF16)
        a = jnp.where(msk, _dot_nt(qs, ks), a)
    vb = hv.astype(BF16)
    o = _dot(a.astype(BF16), vb) + _dot_nt((q * jnp.exp(cum)).astype(BF16), st.astype(BF16))
    last = cum[HG_CHUNK - 1:HG_CHUNK] if fwd else cum[0:1]
    kst = (k * jnp.exp(last - cum)).astype(BF16)
    st_new = st * jnp.exp(last) + _dot_tn(vb, kst)
    return o, st_new


def _hg_kernel(qvf_ref, ff_ref, qvb_ref, fb_ref, lbf_ref, lbb_ref, of_ref, ob_ref, sf_ref, sb_ref, *, ncc):
    @pl.when(pl.program_id(1) == 0)
    def _():
        sf_ref[...] = jnp.zeros_like(sf_ref)
        sb_ref[...] = jnp.zeros_like(sb_ref)

    cf = _hg_consts(True)
    cb = _hg_consts(False)

    def body(cc, _):
        rf = pl.multiple_of(cc * HG_CHUNK, HG_CHUNK)
        rb = pl.multiple_of((ncc - 1 - cc) * HG_CHUNK, HG_CHUNK)
        for hd in range(HG_HEADS):
            sl = slice(hd * HG_D, (hd + 1) * HG_D)
            sv = slice(MIX + hd * HG_D, MIX + (hd + 1) * HG_D)
            o, s_new = _hg_chunk(qvf_ref[0, pl.ds(rf, HG_CHUNK), sl].astype(F32),
                                 qvf_ref[0, pl.ds(rf, HG_CHUNK), sv].astype(F32),
                                 ff_ref[0, pl.ds(rf, HG_CHUNK), sl], lbf_ref[:, sl], sf_ref[hd], True, cf)
            of_ref[0, pl.ds(rf, HG_CHUNK), sl] = o
            sf_ref[hd] = s_new
            o, s_new = _hg_chunk(qvb_ref[0, pl.ds(rb, HG_CHUNK), sl].astype(F32),
                                 qvb_ref[0, pl.ds(rb, HG_CHUNK), sv].astype(F32),
                                 fb_ref[0, pl.ds(rb, HG_CHUNK), sl], lbb_ref[:, sl], sb_ref[hd], False, cb)
            ob_ref[0, pl.ds(rb, HG_CHUNK), sl] = o
            sb_ref[hd] = s_new
        return 0

    lax.fori_loop(0, ncc, body, 0)


def _hg_call(hqv, hff, lbf, lbb):
    b, L, _ = hqv.shape
    tl = min(TL_HG, L)
    n = L // tl
    fwd = lambda w: pl.BlockSpec((1, tl, w), lambda bi, i: (bi, i, 0))
    bwd = lambda w, j: pl.BlockSpec((1, tl, w), lambda bi, i: (bi, n - 1 - i, j))
    return pl.pallas_call(
        functools.partial(_hg_kernel, ncc=tl // HG_CHUNK),
        grid=(b, n),
        in_specs=[fwd(2 * MIX), fwd(MIX), bwd(2 * MIX, 0), bwd(MIX, 1),
                  pl.BlockSpec((1, MIX), lambda bi, i: (0, 0)), pl.BlockSpec((1, MIX), lambda bi, i: (0, 0))],
        out_specs=[fwd(MIX), bwd(MIX, 0)],
        out_shape=[jax.ShapeDtypeStruct((b, L, MIX), F32)] * 2,
        scratch_shapes=[pltpu.VMEM((HG_HEADS, HG_D, HG_D), F32), pltpu.VMEM((HG_HEADS, HG_D, HG_D), F32)],
        compiler_params=_params("parallel", "arbitrary"), name="hgrn2",
    )(hqv, hff, hqv, hff, lbf, lbb)


def _merge_kernel(x_ref, mod_ref, zat_ref, yt_ref, of_ref, ob_ref, og_ref, gp_ref,
                  wglut_ref, bglu_ref, ghg_ref, wbr_ref, wout_ref, o_ref):
    g = _gelu_tanh(yt_ref[0])
    glu = _dot(wglut_ref[...], g.astype(BF16)) + bglu_ref[...]
    zbt = (g * _sigmoid(glu)).astype(BF16)
    o = of_ref[0] + ob_ref[0]
    on = jnp.concatenate([_rms(o[:, hd * HG_D:(hd + 1) * HG_D]) for hd in range(HG_HEADS)], axis=1)
    og = og_ref[0].astype(F32)
    zc = (on * ghg_ref[...] * (og * _sigmoid(og))).astype(BF16)
    bra = _dot_tn(zat_ref[0], wbr_ref[0])
    brb = _dot_tn(zbt, wbr_ref[1])
    brc = _dot(zc, wbr_ref[2])
    gp = gp_ref[0].astype(F32)
    mix = (_sigmoid(gp[:, :D_MODEL]) * bra + _sigmoid(gp[:, D_MODEL:2 * D_MODEL]) * brb
           + _sigmoid(gp[:, 2 * D_MODEL:]) * brc)
    out = _dot(mix.astype(BF16), wout_ref[...])
    o_ref[0] = x_ref[0] + mod_ref[0][2:3] * out


def _merge_call(x, mod, zat, yt, of, ob, hog, gp, w):
    b, L, _ = x.shape
    tl = TL_MG
    tok = lambda n: pl.BlockSpec((1, tl, n), lambda bi, i: (bi, i, 0))
    tr = pl.BlockSpec((1, MIX, tl), lambda bi, i: (bi, 0, i))
    consts = [w["wglut"], w["bglu"], w["ghg"], w["wbr"], w["wout"]]
    return pl.pallas_call(
        _merge_kernel, grid=(b, L // tl),
        in_specs=[tok(D_MODEL), pl.BlockSpec((1, 8, D_MODEL), lambda bi, i: (bi, 0, 0)), tr, tr,
                  tok(MIX), tok(MIX), tok(MIX), tok(3 * D_MODEL)] + [_const_spec(a.shape) for a in consts],
        out_specs=tok(D_MODEL),
        out_shape=jax.ShapeDtypeStruct(x.shape, F32),
        compiler_params=_params("parallel", "parallel"), name="merge",
    )(x, mod, zat, yt, of, ob, hog, gp, *consts)


def _ffn_kernel(x_ref, xp_ref, xn_ref, mod_ref, gffn_ref, wup_ref, wconv_ref, bconv_ref, wdn_ref, o_ref,
                up_ref, acc_ref, *, tl):
    i = pl.program_id(1)
    mod = mod_ref[0]

    def modulate(xv):
        return _rms(xv) * gffn_ref[...] * (1.0 + mod[4:5]) + mod[3:4]

    keep_prev = jnp.where(i > 0, 1.0, 0.0)
    keep_next = jnp.where(i < pl.num_programs(1) - 1, 1.0, 0.0)
    hcat = jnp.concatenate([modulate(xp_ref[0]) * keep_prev, modulate(x_ref[0]), modulate(xn_ref[0]) * keep_next],
                           axis=0).astype(BF16)
    acc_ref[...] = jnp.zeros_like(acc_ref)

    def body(j, _):
        off = pl.multiple_of(j * FF_CW, FF_CW)
        act = None
        for part in range(2):
            col = off + part * D_FF
            up_ref[...] = _dot(hcat, wup_ref[:, pl.ds(col, FF_CW)])
            wc = wconv_ref[:, pl.ds(col, FF_CW)]
            y = (up_ref[pl.ds(HALO - 1, tl), :] * wc[0:1] + up_ref[pl.ds(HALO, tl), :] * wc[1:2]
                 + up_ref[pl.ds(HALO + 1, tl), :] * wc[2:3] + bconv_ref[:, pl.ds(col, FF_CW)])
            act = y * _sigmoid(y) if part == 0 else act * y
        acc_ref[...] += _dot(act.astype(BF16), wdn_ref[pl.ds(off, FF_CW), :])
        return 0

    lax.fori_loop(0, D_FF // FF_CW, body, 0)
    o_ref[0] = x_ref[0] + mod[5:6] * acc_ref[...]


def _ffn_call(x, mod, w):
    b, L, _ = x.shape
    tl = TL_FF
    hb = tl // HALO
    nh = L // HALO
    consts = [w["gffn"], w["wup"], w["wconv"], w["bconv"], w["wdn"]]
    return pl.pallas_call(
        functools.partial(_ffn_kernel, tl=tl), grid=(b, L // tl),
        in_specs=[pl.BlockSpec((1, tl, D_MODEL), lambda bi, i: (bi, i, 0)),
                  pl.BlockSpec((1, HALO, D_MODEL), lambda bi, i: (bi, jnp.maximum(i * hb - 1, 0), 0)),
                  pl.BlockSpec((1, HALO, D_MODEL), lambda bi, i: (bi, jnp.minimum((i + 1) * hb, nh - 1), 0)),
                  pl.BlockSpec((1, 8, D_MODEL), lambda bi, i: (bi, 0, 0))] + [_const_spec(a.shape) for a in consts],
        out_specs=pl.BlockSpec((1, tl, D_MODEL), lambda bi, i: (bi, i, 0)),
        out_shape=jax.ShapeDtypeStruct(x.shape, F32),
        scratch_shapes=[pltpu.VMEM((tl + 2 * HALO, FF_CW), F32), pltpu.VMEM((tl, D_MODEL), F32)],
        compiler_params=_params("parallel", "parallel"), name="ffn",
    )(x, x, x, mod, *consts)


def _final_kernel(x_ref, g_ref, o_ref):
    o_ref[0] = _rms(x_ref[0]) * g_ref[...]


def _final_call(x, g):
    b, L, _ = x.shape
    tl = 512
    spec = pl.BlockSpec((1, tl, D_MODEL), lambda bi, i: (bi, i, 0))
    return pl.pallas_call(
        _final_kernel, grid=(b, L // tl), in_specs=[spec, pl.BlockSpec((1, D_MODEL), lambda bi, i: (0, 0))],
        out_specs=spec, out_shape=jax.ShapeDtypeStruct(x.shape, F32),
        compiler_params=_params("parallel", "parallel"), name="final_norm",
    )(x, g)


def _layer_weights(p, l):
    w_in = p["w_in"][l]
    col = lambda i: w_in[:, IN_OFFS[i]:IN_OFFS[i + 1]]
    half = ROPE // 2
    kr = col(2)
    z = lambda n: jnp.zeros((D_MODEL, n), F32)
    wkr = jnp.concatenate([z(NOPE), kr[:, :half], kr[:, half:], z(LANE - NOPE - ROPE),
                           z(NOPE), kr[:, half:], kr[:, :half], z(LANE - NOPE - ROPE)], axis=1)
    wq = p["w_q_up"][l].reshape(Q_LORA, N_HEADS, NOPE + ROPE)
    zq = lambda n: jnp.zeros((Q_LORA, N_HEADS, n), F32)
    x1, x2 = wq[:, :, NOPE:NOPE + half], wq[:, :, NOPE + half:]
    wqa = jnp.concatenate([wq[:, :, :NOPE], x1, x2, zq(LANE - NOPE - ROPE)], axis=2).reshape(Q_LORA, N_HEADS * LANE)
    wqb = jnp.concatenate([zq(NOPE), x2, x1, zq(LANE - NOPE - ROPE)], axis=2).reshape(Q_LORA, N_HEADS * LANE)
    wkv = p["w_kv_up"][l].reshape(KV_LORA, N_HEADS, 2 * NOPE)
    wk = jnp.concatenate([wkv[:, :, :NOPE], jnp.zeros((KV_LORA, N_HEADS, LANE - NOPE), F32)],
                         axis=2).reshape(KV_LORA, N_HEADS * LANE)
    wvt = wkv[:, :, NOPE:].reshape(KV_LORA, MIX).T
    bf = lambda a: a.astype(BF16)
    return dict(
        gmix=p["g_mix"][l][None], gq=p["g_q_lat"][l][None], gkv=p["g_kv_lat"][l][None],
        wlat=bf(jnp.concatenate([col(0), col(1)], axis=1)), wkr=bf(wkr), wut=bf(col(3).T),
        whqv=bf(jnp.concatenate([col(4), col(7)], axis=1)), whog=bf(col(8)),
        whff=bf(jnp.concatenate([col(5), col(6)], axis=1)), wgate=bf(col(9)),
        wqa=bf(wqa), wqb=bf(wqb), wk=bf(wk), wvt=bf(wvt),
        wglut=bf(p["w_glu"][l].T), bglu=p["b_glu"][l][:, None], ghg=p["g_hg_out"][l][None],
        wbr=bf(p["w_branch"][l]), wout=bf(p["w_out"][l]),
        gffn=p["g_ffn"][l][None], wup=bf(p["w_ffn_up"][l]), wconv=p["w_ffn_conv"][l],
        bconv=p["b_ffn_conv"][l][None], wdn=bf(p["w_ffn_down"][l]),
    )


def _rope_tables(L):
    half = ROPE // 2
    inv_freq = 1.0 / (ROPE_BASE ** (jnp.arange(0, ROPE, 2, dtype=F32) / ROPE))
    ang = jnp.arange(L, dtype=F32)[:, None] * inv_freq[None, :]
    cos, sin = jnp.cos(ang), jnp.sin(ang)
    one, zero = jnp.ones((L, NOPE), F32), jnp.zeros((L, NOPE), F32)
    pad = jnp.zeros((L, LANE - NOPE - ROPE), F32)
    return (jnp.concatenate([one, cos, cos, pad], axis=1), jnp.concatenate([zero, -sin, sin, pad], axis=1))


def _trunk(x, mods, weights, s5ops, lb, g_final):
    b, L, _ = x.shape
    cos_t, sin_t = _rope_tables(L)
    for l in range(DEPTH):
        w = weights[l]
        q, k, vt, ut, hqv, hog, hff, gp = _inproj_call(x, mods[l], w, cos_t, sin_t)
        zat = _flash_call(q, k, vt)
        yt = _s5_call(ut, s5ops[l])
        of, ob = _hg_call(hqv, hff, lb[0, l][None], lb[1, l][None])
        x = _merge_call(x, mods[l], zat, yt, of, ob, hog, gp, w)
        x = _ffn_call(x, mods[l], w)
    return _final_call(x, g_final[None])


def kernel(x_prompt, x_sample, c_prompt, c_sample, w_ada, b_ada, g_mix, w_in, g_q_lat, w_q_up, g_kv_lat, w_kv_up,
           s5_lam_re, s5_lam_im, s5_log_dt, s5_b_re, s5_b_im, s5_c_re, s5_c_im, s5_d, w_glu, b_glu, hg_lb_logits,
           g_hg_out, w_branch, w_out, g_ffn, w_ffn_up, w_ffn_conv, b_ffn_conv, w_ffn_down, g_final):
    p = dict(g_mix=g_mix, w_in=w_in, g_q_lat=g_q_lat, w_q_up=w_q_up, g_kv_lat=g_kv_lat, w_kv_up=w_kv_up,
             w_glu=w_glu, b_glu=b_glu, g_hg_out=g_hg_out, w_branch=w_branch, w_out=w_out, g_ffn=g_ffn,
             w_ffn_up=w_ffn_up, w_ffn_conv=w_ffn_conv, b_ffn_conv=b_ffn_conv, w_ffn_down=w_ffn_down)
    depth = w_in.shape[0]
    assert depth == DEPTH
    bp, bs = c_prompt.shape[0], c_sample.shape[0]
    rows = -(-(bp + bs) // 8) * 8
    c_all = jnp.concatenate([c_prompt, c_sample, jnp.zeros((rows - bp - bs, D_MODEL), F32)], axis=0)
    mod_all = _ada_call(c_all, w_ada, b_ada)

    def mods_for(lo, n):
        m = mod_all[:, lo:lo + n].reshape(DEPTH, n, 6, D_MODEL)
        return jnp.pad(m, ((0, 0), (0, 0), (0, 2), (0, 0)))

    weights = [_layer_weights(p, l) for l in range(DEPTH)]
    s5ops = [_s5_operators(s5_lam_re[l], s5_lam_im[l], s5_log_dt[l], s5_b_re[l], s5_b_im[l],
                           s5_c_re[l], s5_c_im[l], s5_d[l]) for l in range(DEPTH)]
    gam = jax.nn.softmax(hg_lb_logits.astype(F32), axis=1)
    lb = jnp.cumsum(gam, axis=1) - gam[:, :1]
    y_prompt = _trunk(x_prompt, mods_for(0, bp), weights, s5ops, lb, g_final)
    y_sample = _trunk(x_sample, mods_for(bp, bs), weights, s5ops, lb, g_final)
    return (y_prompt, y_sample)
```

```python
import functools
import math

import jax
import jax.numpy as jnp
from jax import lax
from jax.experimental import pallas as pl
from jax.experimental.pallas import tpu as pltpu

F32 = jnp.float32
BF16 = jnp.bfloat16

D_MODEL = 1024
DEPTH = 4
MIX = 512
N_HEADS = 8
NOPE = 64
ROPE = 32
Q_LORA = 384
KV_LORA = 256
ROPE_BASE = 10000.0
S5_GROUPS = 32
S5_GROUP = 16
S5_STATE = 64
S5_CHUNK = 128
HG_HEADS = 4
HG_D = 128
HG_CHUNK = 64
HG_BLOCK = 16
D_FF = 2816
EPS = 1e-6
ATTN_SCALE = 1.0 / math.sqrt(NOPE + ROPE)
LOG2E = 1.4426950408889634
IN_OFFS = (0, 384, 640, 672, 1184, 1696, 2208, 2720, 3232, 3744, 6816)

LANE = 128
VMEM_LIMIT = 56 * 1024 * 1024

TL_IN = 256
TQ = 256
TK = 512
TL_HG = 256
TL_MG = 256
TL_FF = 256
FF_CW = 768
HALO = 8


def _dot(a, b):
    return jnp.dot(a, b, preferred_element_type=F32)


def _dot_nt(a, b):
    return lax.dot_general(a, b, (((1,), (1,)), ((), ())), preferred_element_type=F32)


def _dot_tn(a, b):
    return lax.dot_general(a, b, (((0,), (0,)), ((), ())), preferred_element_type=F32)


def _sigmoid(x):
    return 1.0 / (1.0 + jnp.exp(-x))


def _rms(x):
    return x * lax.rsqrt(jnp.mean(x * x, axis=-1, keepdims=True) + EPS)


def _gelu_tanh(x):
    return 0.5 * x * (1.0 + jnp.tanh(math.sqrt(2.0 / math.pi) * (x + 0.044715 * (x * x * x))))


def _params(*sem):
    return pltpu.CompilerParams(dimension_semantics=sem, vmem_limit_bytes=VMEM_LIMIT)


def _const_spec(shape):
    nd = len(shape)
    return pl.BlockSpec(shape, lambda *_: (0,) * nd, pipeline_mode=pl.Buffered(1))


def _ada_kernel(c_ref, w_ref, b_ref, o_ref):
    c = c_ref[...]
    a = (c * _sigmoid(c)).astype(BF16)
    o_ref[0] = _dot(a, w_ref[0].astype(BF16)) + b_ref[0]


def _ada_call(c_all, w_ada, b_ada):
    nb = 1536
    rows = c_all.shape[0]
    return pl.pallas_call(
        _ada_kernel,
        grid=(DEPTH, 6 * D_MODEL // nb),
        in_specs=[pl.BlockSpec((rows, D_MODEL), lambda l, j: (0, 0)),
                  pl.BlockSpec((1, D_MODEL, nb), lambda l, j: (l, 0, j)),
                  pl.BlockSpec((1, 1, nb), lambda l, j: (l, 0, j))],
        out_specs=pl.BlockSpec((1, rows, nb), lambda l, j: (l, 0, j)),
        out_shape=jax.ShapeDtypeStruct((DEPTH, rows, 6 * D_MODEL), F32),
        compiler_params=_params("arbitrary", "arbitrary"),
        name="ada_mod",
    )(c_all, w_ada, b_ada.reshape(DEPTH, 1, 6 * D_MODEL))


def _inproj_kernel(x_ref, mod_ref, gmix_ref, cos_ref, sin_ref,
                   wlat_ref, wkr_ref, wut_ref, whqv_ref, whog_ref, whff_ref, wgate_ref,
                   gq_ref, gkv_ref, wqa_ref, wqb_ref, wk_ref, wvt_ref,
                   q_ref, k_ref, vt_ref, ut_ref, hqv_ref, hog_ref, hff_ref, gp_ref):
    mod = mod_ref[0]
    h = _rms(x_ref[0]) * gmix_ref[...] * (1.0 + mod[1:2]) + mod[0:1]
    hb = h.astype(BF16)
    lat = _dot(hb, wlat_ref[...])
    qn = (_rms(lat[:, :Q_LORA]) * gq_ref[...]).astype(BF16)
    kvn = (_rms(lat[:, Q_LORA:]) * gkv_ref[...]).astype(BF16)
    cos = cos_ref[...]
    sin = sin_ref[...]
    qa = _dot(qn, wqa_ref[...])
    qb = _dot(qn, wqb_ref[...])
    krp = _dot(hb, wkr_ref[...])
    kr = krp[:, :LANE] * cos + krp[:, LANE:] * sin
    kn = _dot(kvn, wk_ref[...])
    for hd in range(N_HEADS):
        sl = slice(hd * LANE, (hd + 1) * LANE)
        q_ref[0, :, sl] = ((qa[:, sl] * cos + qb[:, sl] * sin) * (ATTN_SCALE * LOG2E)).astype(BF16)
        k_ref[0, :, sl] = (kn[:, sl] + kr).astype(BF16)
    vt_ref[0] = _dot_nt(wvt_ref[...], kvn).astype(BF16)
    ut_ref[0] = _dot_nt(wut_ref[...], hb).astype(BF16)
    hqv_ref[0] = _dot(hb, whqv_ref[...]).astype(BF16)
    hog_ref[0] = _dot(hb, whog_ref[...]).astype(BF16)
    hff_ref[0] = _dot(hb, whff_ref[...])
    gp_ref[0] = _dot(hb, wgate_ref[...]).astype(BF16)


def _inproj_call(x, mod, w, cos_t, sin_t):
    b, L, _ = x.shape
    tl = TL_IN
    tok = lambda n: pl.BlockSpec((1, tl, n), lambda bi, i: (bi, i, 0))
    tr = lambda n: pl.BlockSpec((1, n, tl), lambda bi, i: (bi, 0, i))
    weights = [w["gmix"], None, None, w["wlat"], w["wkr"], w["wut"], w["whqv"], w["whog"], w["whff"], w["wgate"],
               w["gq"], w["gkv"], w["wqa"], w["wqb"], w["wk"], w["wvt"]]
    in_specs = [tok(D_MODEL), pl.BlockSpec((1, 8, D_MODEL), lambda bi, i: (bi, 0, 0))]
    args = [x, mod]
    for a in weights:
        if a is None:
            continue
        in_specs.append(_const_spec(a.shape))
        args.append(a)
    in_specs[3:3] = [pl.BlockSpec((tl, LANE), lambda bi, i: (i, 0))] * 2
    args[3:3] = [cos_t, sin_t]
    out_shape = [jax.ShapeDtypeStruct((b, L, 1024), BF16), jax.ShapeDtypeStruct((b, L, 1024), BF16),
                 jax.ShapeDtypeStruct((b, MIX, L), BF16), jax.ShapeDtypeStruct((b, MIX, L), BF16),
                 jax.ShapeDtypeStruct((b, L, 1024), BF16), jax.ShapeDtypeStruct((b, L, MIX), BF16),
                 jax.ShapeDtypeStruct((b, L, 1024), F32), jax.ShapeDtypeStruct((b, L, 3 * D_MODEL), BF16)]
    out_specs = [tok(1024), tok(1024), tr(MIX), tr(MIX), tok(1024), tok(MIX), tok(1024), tok(3 * D_MODEL)]
    return pl.pallas_call(
        _inproj_kernel, grid=(b, L // tl), in_specs=in_specs, out_specs=out_specs, out_shape=out_shape,
        compiler_params=_params("parallel", "parallel"), name="inproj",
    )(*args)


def _flash_kernel(q_ref, k_ref, vt_ref, o_ref, s_ref, *, tk, nk):
    tq = q_ref.shape[1]
    half = NOPE
    ones = jnp.ones((16, tk), BF16)

    def scores(buf, j):
        off = pl.multiple_of(j * tk, tk)
        for hh in range(2):
            s_ref[buf, hh] = _dot_nt(k_ref[0, pl.ds(off, tk), hh * LANE:(hh + 1) * LANE],
                                     q_ref[0, :, hh * LANE:(hh + 1) * LANE])

    def process(buf, j, carry):
        off = pl.multiple_of(j * tk, tk)
        new = []
        for hh in range(2):
            m, acc = carry[hh]
            s = s_ref[buf, hh]
            mn = jnp.maximum(m, jnp.max(s, axis=0, keepdims=True))
            p = jnp.exp2(s - mn).astype(BF16)
            a = jnp.exp2(m - mn)
            v = jnp.concatenate([vt_ref[0, hh * half:(hh + 1) * half, pl.ds(off, tk)], ones], axis=0)
            new.append((mn, a * acc + _dot(v, p)))
        return tuple(new)

    def body(i, carry):
        j0 = 2 * i
        scores(1, j0 + 1)
        carry = process(0, j0, carry)
        scores(0, jnp.minimum(j0 + 2, nk - 1))
        return process(1, j0 + 1, carry)

    scores(0, 0)
    init = tuple((jnp.full((1, tq), -1e30, F32), jnp.zeros((half + 16, tq), F32)) for _ in range(2))
    res = lax.fori_loop(0, nk // 2, body, init)
    if nk % 2:
        res = process(0, nk - 1, res)
    for hh in range(2):
        acc = res[hh][1]
        o_ref[0, hh * half:(hh + 1) * half, :] = (acc[:half] / acc[half:half + 1]).astype(BF16)


def _flash_call(q, k, vt):
    b, L, _ = q.shape
    tq, tk = min(TQ, L), min(TK, L)
    return pl.pallas_call(
        functools.partial(_flash_kernel, tk=tk, nk=L // tk),
        scratch_shapes=[pltpu.VMEM((2, 2, tk, tq), F32)],
        grid=(b, N_HEADS // 2, L // tq),
        in_specs=[pl.BlockSpec((1, tq, 2 * LANE), lambda bi, p, i: (bi, i, p)),
                  pl.BlockSpec((1, L, 2 * LANE), lambda bi, p, i: (bi, 0, p)),
                  pl.BlockSpec((1, LANE, L), lambda bi, p, i: (bi, p, 0))],
        out_specs=pl.BlockSpec((1, LANE, tq), lambda bi, p, i: (bi, p, i)),
        out_shape=jax.ShapeDtypeStruct((b, MIX, L), BF16),
        compiler_params=_params("parallel", "parallel", "arbitrary"), name="flash",
    )(q, k, vt)


def _s5_kernel(u_ref, t_ref, f_ref, e_ref, a_ref, y_ref, sloc_ref, st_ref, *, nb, nct):
    m = nb * nct
    ucat = jnp.concatenate([u_ref[:, 0, hi].reshape(m, S5_CHUNK) for hi in range(S5_GROUP)], axis=1)
    y = _dot(ucat, t_ref[0])
    sloc_ref[...] = _dot(ucat, f_ref[0])
    a = a_ref[0]
    arf, aif, arb, aib = a[0:1], a[1:2], a[2:3], a[3:4]

    sb = 8 if nct % 8 == 0 else nct
    nblk = nct // sb

    def body(blk, carry):
        new = []
        for bi in range(nb):
            xrf, xif, xrb, xib = carry[bi]
            rf = pl.multiple_of(bi * nct + blk * sb, sb)
            rb = pl.multiple_of(bi * nct + (nblk - 1 - blk) * sb, sb)
            lf = sloc_ref[pl.ds(rf, sb), 0:2 * LANE]
            lb = sloc_ref[pl.ds(rb, sb), 2 * LANE:4 * LANE]
            frows, brows = [], []
            for r in range(sb):
                frows.append((xrf, xif))
                xrf, xif = (arf * xrf - aif * xif + lf[r:r + 1, 0:LANE],
                            arf * xif + aif * xrf + lf[r:r + 1, LANE:2 * LANE])
            for r in range(sb - 1, -1, -1):
                brows.append((xrb, xib))
                xrb, xib = (arb * xrb - aib * xib + lb[r:r + 1, 0:LANE],
                            arb * xib + aib * xrb + lb[r:r + 1, LANE:2 * LANE])
            brows = brows[::-1]
            st_ref[pl.ds(rf, sb), 0:LANE] = jnp.concatenate([t[0] for t in frows], axis=0)
            st_ref[pl.ds(rf, sb), LANE:2 * LANE] = jnp.concatenate([t[1] for t in frows], axis=0)
            st_ref[pl.ds(rb, sb), 2 * LANE:3 * LANE] = jnp.concatenate([t[0] for t in brows], axis=0)
            st_ref[pl.ds(rb, sb), 3 * LANE:4 * LANE] = jnp.concatenate([t[1] for t in brows], axis=0)
            new.append((xrf, xif, xrb, xib))
        return tuple(new)

    z = jnp.zeros((1, LANE), F32)
    lax.fori_loop(0, nblk, body, tuple((z, z, z, z) for _ in range(nb)))
    y = y + _dot(st_ref[...].astype(BF16), e_ref[0])
    for ho in range(S5_GROUP):
        y_ref[:, 0, ho] = y[:, ho * S5_CHUNK:(ho + 1) * S5_CHUNK].reshape(nb, nct, S5_CHUNK)


def _s5_call(ut, ops):
    b, _, L = ut.shape
    nct = L // S5_CHUNK
    u5 = ut.reshape(b, S5_GROUPS, S5_GROUP, nct, S5_CHUNK)
    gw = S5_GROUP * S5_CHUNK
    blk = pl.BlockSpec((b, 1, S5_GROUP, nct, S5_CHUNK), lambda g: (0, g, 0, 0, 0))
    y = pl.pallas_call(
        functools.partial(_s5_kernel, nb=b, nct=nct),
        grid=(S5_GROUPS,),
        in_specs=[blk,
                  pl.BlockSpec((1, gw, gw), lambda g: (g, 0, 0)),
                  pl.BlockSpec((1, gw, 4 * LANE), lambda g: (g, 0, 0)),
                  pl.BlockSpec((1, 4 * LANE, gw), lambda g: (g, 0, 0)),
                  pl.BlockSpec((1, 8, LANE), lambda g: (g, 0, 0))],
        out_specs=blk,
        out_shape=jax.ShapeDtypeStruct(u5.shape, F32),
        scratch_shapes=[pltpu.VMEM((b * nct, 4 * LANE), F32), pltpu.VMEM((b * nct, 4 * LANE), F32)],
        compiler_params=_params("parallel"), name="s5",
    )(u5, ops["t"], ops["f"], ops["e"], ops["a"])
    return y.reshape(b, MIX, L)


def _toeplitz_kernel(w_ref, t_ref):
    c = S5_CHUNK

    def body(hi, _):
        r0 = pl.multiple_of(hi * c, c)
        for ho in range(S5_GROUP):
            row = w_ref[0, hi, ho:ho + 1, :]
            skew = pltpu.roll(jnp.broadcast_to(row, (c, 2 * c)), 0, 1, stride=1, stride_axis=0)
            t_ref[0, pl.ds(r0, c), ho * c:(ho + 1) * c] = skew[:, :c].astype(BF16)
        return 0

    lax.fori_loop(0, S5_GROUP, body, 0)


def _toeplitz_call(w):
    g = w.shape[0]
    gw = S5_GROUP * S5_CHUNK
    return pl.pallas_call(
        _toeplitz_kernel, grid=(g,),
        in_specs=[pl.BlockSpec((1, S5_GROUP, S5_GROUP, 2 * S5_CHUNK), lambda i: (i, 0, 0, 0))],
        out_specs=pl.BlockSpec((1, gw, gw), lambda i: (i, 0, 0)),
        out_shape=jax.ShapeDtypeStruct((g, gw, gw), BF16),
        compiler_params=_params("parallel"), name="s5_toeplitz",
    )(w)


def _s5_operators(lam_re, lam_im, log_dt, b_re, b_im, c_re, c_im, d):
    hp = lax.Precision.HIGHEST
    G, P, GS, C = S5_GROUPS, S5_STATE, S5_GROUP, S5_CHUNK
    lam_re, lam_im = lam_re.astype(F32), lam_im.astype(F32)
    dt = jnp.exp(log_dt.astype(F32))[..., None]
    zr, zi = lam_re * dt, lam_im * dt
    kk = jnp.arange(C + 1, dtype=F32)[None, None, :, None]
    mag = jnp.exp(zr[:, :, None, :] * kk)
    ang = zi[:, :, None, :] * kk
    pw_re, pw_im = mag * jnp.cos(ang), mag * jnp.sin(ang)
    lb_re, lb_im = pw_re[:, :, 1], pw_im[:, :, 1]
    den = lam_re * lam_re + lam_im * lam_im
    nr, ni = lb_re - 1.0, lb_im
    cr = (nr * lam_re + ni * lam_im) / den
    ci = (ni * lam_re - nr * lam_im) / den
    bb_re = cr[..., None] * b_re - ci[..., None] * b_im
    bb_im = cr[..., None] * b_im + ci[..., None] * b_re
    c_re, c_im = c_re.astype(F32), c_im.astype(F32)

    def kern(dr):
        pr, pi_ = pw_re[dr, :, :C], pw_im[dr, :, :C]
        cp_re = c_re[dr][:, None] * pr[:, :, None, :] - c_im[dr][:, None] * pi_[:, :, None, :]
        cp_im = c_re[dr][:, None] * pi_[:, :, None, :] + c_im[dr][:, None] * pr[:, :, None, :]
        return (jnp.einsum('gkhp,gpi->gkhi', cp_re, bb_re[dr], precision=hp)
                - jnp.einsum('gkhp,gpi->gkhi', cp_im, bb_im[dr], precision=hp))

    kf, kb = kern(0), kern(1)
    k0 = kf[:, 0] + kb[:, 0] + jnp.eye(GS, dtype=F32)[None] * d.astype(F32).reshape(G, GS)[:, :, None]
    kfull = jnp.concatenate([kb[:, 1:][:, ::-1], k0[:, None], kf[:, 1:], jnp.zeros((G, 1, GS, GS), F32)], axis=1)
    t_op = _toeplitz_call(jnp.roll(kfull, -(C - 1), axis=1).transpose(0, 3, 2, 1))

    def f_part(dr, idx):
        pr, pi_ = pw_re[dr][:, idx], pw_im[dr][:, idx]
        br, bi = bb_re[dr].transpose(0, 2, 1), bb_im[dr].transpose(0, 2, 1)
        re = pr[:, None] * br[:, :, None] - pi_[:, None] * bi[:, :, None]
        im = pr[:, None] * bi[:, :, None] + pi_[:, None] * br[:, :, None]
        return re.reshape(G, GS * C, P), im.reshape(G, GS * C, P)

    tau = jnp.arange(C)
    ffr, ffi = f_part(0, C - 1 - tau)
    fbr, fbi = f_part(1, tau)
    padl = lambda a: jnp.pad(a, ((0, 0), (0, 0), (0, LANE - P)))
    f_op = jnp.concatenate([padl(ffr), padl(ffi), padl(fbr), padl(fbi)], axis=-1).astype(BF16)

    def e_part(dr, idx):
        pr, pi_ = pw_re[dr][:, idx], pw_im[dr][:, idx]
        cr_, ci_ = c_re[dr].transpose(0, 2, 1), c_im[dr].transpose(0, 2, 1)
        prt, pit = pr.transpose(0, 2, 1), pi_.transpose(0, 2, 1)
        re = cr_[:, :, :, None] * prt[:, :, None, :] - ci_[:, :, :, None] * pit[:, :, None, :]
        im = cr_[:, :, :, None] * pit[:, :, None, :] + ci_[:, :, :, None] * prt[:, :, None, :]
        return re.reshape(G, P, GS * C), -im.reshape(G, P, GS * C)

    efr, efi = e_part(0, tau + 1)
    ebr, ebi = e_part(1, C - tau)
    padr = lambda a: jnp.pad(a, ((0, 0), (0, LANE - P), (0, 0)))
    e_op = jnp.concatenate([padr(efr), padr(efi), padr(ebr), padr(ebi)], axis=1).astype(BF16)

    arows = [pw_re[0, :, C], pw_im[0, :, C], pw_re[1, :, C], pw_im[1, :, C]]
    a_op = jnp.stack([jnp.pad(r, ((0, 0), (0, LANE - P))) for r in arows]
                     + [jnp.zeros((G, LANE), F32)] * 4, axis=1)
    return dict(t=t_op, f=f_op, e=e_op, a=a_op)


def _split3(x):
    hi = x.astype(BF16)
    r1 = x - hi.astype(F32)
    mid = r1.astype(BF16)
    lo = (r1 - mid.astype(F32)).astype(BF16)
    return hi, mid, lo


def _hg_consts(fwd):
    c, bk = HG_CHUNK, HG_BLOCK
    t = lax.broadcasted_iota(jnp.int32, (c, c), 0)
    s = lax.broadcasted_iota(jnp.int32, (c, c), 1)
    bt, bs = t // bk, s // bk
    if fwd:
        tri = (s <= t)
        m0 = (bt == bs) & (s <= t)
        m1 = ((bt == 1) & (bs == 0)) | ((bt == 3) & (bs == 2))
        m2 = (bt >= 2) & (bs < 2)
    else:
        tri = (s >= t)
        m0 = (bt == bs) & (s >= t)
        m1 = ((bt == 0) & (bs == 1)) | ((bt == 2) & (bs == 3))
        m2 = (bt < 2) & (bs >= 2)
    return jnp.where(tri, 1.0, 0.0).astype(BF16), m0, m1, m2


def _hg_chunks(streams):
    bk, nblk = HG_BLOCK, HG_CHUNK // HG_BLOCK
    zero = jnp.zeros((bk, HG_D), F32)

    gates = []
    for hq, hv, hf, lb, st, fwd, consts in streams:
        q = hq * _sigmoid(hq)
        f = lb + (1.0 - lb) * _sigmoid(hf)
        logf = jnp.log(f)
        p1, p2, p3 = _split3(logf)
        tri = consts[0]
        cum = _dot(tri, p1) + _dot(tri, p2) + _dot(tri, p3)
        gates.append((q, 1.0 - f, logf, cum))

    scores = []
    for (hq, hv, hf, lb, st, fwd, consts), (q, k, logf, cum) in zip(streams, gates):
        _, m0, m1, m2 = consts
        cexc = cum - logf
        cumb = [cum[bk * i:bk * (i + 1)] for i in range(nblk)]
        edge = 0 if fwd else bk - 1

        def ref(row, cexc=cexc):
            return jnp.broadcast_to(cexc[row:row + 1, :], (bk, HG_D))

        levels = []
        r0 = [ref(bk * i + edge) for i in range(nblk)]
        levels.append(([cumb[i] - r0[i] for i in range(nblk)], [r0[i] - cumb[i] for i in range(nblk)], m0))
        late1, early1 = ((1, 3), (0, 2)) if fwd else ((0, 2), (1, 3))
        eq, ek = [zero] * nblk, [zero] * nblk
        for lt, er in zip(late1, early1):
            r = ref(bk * lt + edge)
            eq[lt] = cumb[lt] - r
            ek[er] = r - cumb[er]
        levels.append((eq, ek, m1))
        late2, early2 = ((2, 3), (0, 1)) if fwd else ((1, 0), (2, 3))
        r = ref(bk * late2[0] + edge)
        eq, ek = [zero] * nblk, [zero] * nblk
        for lt in late2:
            eq[lt] = cumb[lt] - r
        for er in early2:
            ek[er] = r - cumb[er]
        levels.append((eq, ek, m2))

        a = jnp.zeros((HG_CHUNK, HG_CHUNK), F32)
        for eq, ek, msk in levels:
            qs = (q * jnp.exp(jnp.concatenate(eq, axis=0))).astype(BF16)
            ks = (k * jnp.exp(jnp.concatenate(ek, axis=0))).astype(BF16)
            a = jnp.where(msk, _dot_nt(qs, ks), a)
        scores.append(a)

    outs = []
    for (hq, hv, hf, lb, st, fwd, consts), (q, k, logf, cum), a in zip(streams, gates, scores):
        vb = hv.astype(BF16)
        o = _dot(a.astype(BF16), vb) + _dot_nt((q * jnp.exp(cum)).astype(BF16), st.astype(BF16))
        last = cum[HG_CHUNK - 1:HG_CHUNK] if fwd else cum[0:1]
        kst = (k * jnp.exp(last - cum)).astype(BF16)
        outs.append((o, st * jnp.exp(last) + _dot_tn(vb, kst)))
    return outs


def _hg_kernel(qvf_ref, ff_ref, qvb_ref, fb_ref, lbf_ref, lbb_ref, of_ref, ob_ref, sf_ref, sb_ref, *, ncc):
    @pl.when(pl.program_id(1) == 0)
    def _():
        sf_ref[...] = jnp.zeros_like(sf_ref)
        sb_ref[...] = jnp.zeros_like(sb_ref)

    cf = _hg_consts(True)
    cb = _hg_consts(False)

    def body(cc, _):
        rf = pl.multiple_of(cc * HG_CHUNK, HG_CHUNK)
        rb = pl.multiple_of((ncc - 1 - cc) * HG_CHUNK, HG_CHUNK)
        streams = []
        for hd in range(HG_HEADS):
            sl = slice(hd * HG_D, (hd + 1) * HG_D)
            sv = slice(MIX + hd * HG_D, MIX + (hd + 1) * HG_D)
            streams.append((qvf_ref[0, pl.ds(rf, HG_CHUNK), sl].astype(F32),
                            qvf_ref[0, pl.ds(rf, HG_CHUNK), sv].astype(F32),
                            ff_ref[0, pl.ds(rf, HG_CHUNK), sl], lbf_ref[:, sl], sf_ref[hd], True, cf))
            streams.append((qvb_ref[0, pl.ds(rb, HG_CHUNK), sl].astype(F32),
                            qvb_ref[0, pl.ds(rb, HG_CHUNK), sv].astype(F32),
                            fb_ref[0, pl.ds(rb, HG_CHUNK), sl], lbb_ref[:, sl], sb_ref[hd], False, cb))
        outs = _hg_chunks(streams)
        for hd in range(HG_HEADS):
            sl = slice(hd * HG_D, (hd + 1) * HG_D)
            of_ref[0, pl.ds(rf, HG_CHUNK), sl], sf_ref[hd] = outs[2 * hd]
            ob_ref[0, pl.ds(rb, HG_CHUNK), sl], sb_ref[hd] = outs[2 * hd + 1]
        return 0

    lax.fori_loop(0, ncc, body, 0)


def _hg_call(hqv, hff, lbf, lbb):
    b, L, _ = hqv.shape
    tl = min(TL_HG, L)
    n = L // tl
    fwd = lambda w: pl.BlockSpec((1, tl, w), lambda bi, i: (bi, i, 0))
    bwd = lambda w, j: pl.BlockSpec((1, tl, w), lambda bi, i: (bi, n - 1 - i, j))
    return pl.pallas_call(
        functools.partial(_hg_kernel, ncc=tl // HG_CHUNK),
        grid=(b, n),
        in_specs=[fwd(2 * MIX), fwd(MIX), bwd(2 * MIX, 0), bwd(MIX, 1),
                  pl.BlockSpec((1, MIX), lambda bi, i: (0, 0)), pl.BlockSpec((1, MIX), lambda bi, i: (0, 0))],
        out_specs=[fwd(MIX), bwd(MIX, 0)],
        out_shape=[jax.ShapeDtypeStruct((b, L, MIX), F32)] * 2,
        scratch_shapes=[pltpu.VMEM((HG_HEADS, HG_D, HG_D), F32), pltpu.VMEM((HG_HEADS, HG_D, HG_D), F32)],
        compiler_params=_params("parallel", "arbitrary"), name="hgrn2",
    )(hqv, hff, hqv, hff, lbf, lbb)


def _merge_kernel(x_ref, mod_ref, zat_ref, yt_ref, of_ref, ob_ref, og_ref, gp_ref,
                  wglut_ref, bglu_ref, ghg_ref, wbr_ref, wout_ref, o_ref):
    g = _gelu_tanh(yt_ref[0])
    glu = _dot(wglut_ref[...], g.astype(BF16)) + bglu_ref[...]
    zbt = (g * _sigmoid(glu)).astype(BF16)
    o = of_ref[0] + ob_ref[0]
    on = jnp.concatenate([_rms(o[:, hd * HG_D:(hd + 1) * HG_D]) for hd in range(HG_HEADS)], axis=1)
    og = og_ref[0].astype(F32)
    zc = (on * ghg_ref[...] * (og * _sigmoid(og))).astype(BF16)
    bra = _dot_tn(zat_ref[0], wbr_ref[0])
    brb = _dot_tn(zbt, wbr_ref[1])
    brc = _dot(zc, wbr_ref[2])
    gp = gp_ref[0].astype(F32)
    mix = (_sigmoid(gp[:, :D_MODEL]) * bra + _sigmoid(gp[:, D_MODEL:2 * D_MODEL]) * brb
           + _sigmoid(gp[:, 2 * D_MODEL:]) * brc)
    out = _dot(mix.astype(BF16), wout_ref[...])
    o_ref[0] = x_ref[0] + mod_ref[0][2:3] * out


def _merge_call(x, mod, zat, yt, of, ob, hog, gp, w):
    b, L, _ = x.shape
    tl = TL_MG
    tok = lambda n: pl.BlockSpec((1, tl, n), lambda bi, i: (bi, i, 0))
    tr = pl.BlockSpec((1, MIX, tl), lambda bi, i: (bi, 0, i))
    consts = [w["wglut"], w["bglu"], w["ghg"], w["wbr"], w["wout"]]
    return pl.pallas_call(
        _merge_kernel, grid=(b, L // tl),
        in_specs=[tok(D_MODEL), pl.BlockSpec((1, 8, D_MODEL), lambda bi, i: (bi, 0, 0)), tr, tr,
                  tok(MIX), tok(MIX), tok(MIX), tok(3 * D_MODEL)] + [_const_spec(a.shape) for a in consts],
        out_specs=tok(D_MODEL),
        out_shape=jax.ShapeDtypeStruct(x.shape, F32),
        compiler_params=_params("parallel", "parallel"), name="merge",
    )(x, mod, zat, yt, of, ob, hog, gp, *consts)


def _ff_chunks():
    out, off = [], 0
    while off < D_FF:
        w = min(FF_CW, D_FF - off)
        out.append((off, w))
        off += w
    return out


def _ffn_kernel(x_ref, xp_ref, xn_ref, mod_ref, gffn_ref, wup_ref, wconv_ref, bconv_ref, wdn_ref, o_ref,
                up_ref, *, tl):
    i = pl.program_id(1)
    mod = mod_ref[0]

    def modulate(xv):
        return _rms(xv) * gffn_ref[...] * (1.0 + mod[4:5]) + mod[3:4]

    keep_prev = jnp.where(i > 0, 1.0, 0.0)
    keep_next = jnp.where(i < pl.num_programs(1) - 1, 1.0, 0.0)
    hcat = jnp.concatenate([modulate(xp_ref[0]) * keep_prev, modulate(x_ref[0]), modulate(xn_ref[0]) * keep_next],
                           axis=0).astype(BF16)
    acc = None
    for off, cw in _ff_chunks():
        act = None
        for part in range(2):
            col = off + part * D_FF
            up_ref[part, :, 0:cw] = _dot(hcat, wup_ref[:, col:col + cw])
            wc = wconv_ref[:, col:col + cw]
            y = (up_ref[part, pl.ds(HALO - 1, tl), 0:cw] * wc[0:1] + up_ref[part, pl.ds(HALO, tl), 0:cw] * wc[1:2]
                 + up_ref[part, pl.ds(HALO + 1, tl), 0:cw] * wc[2:3] + bconv_ref[:, col:col + cw])
            act = y * _sigmoid(y) if part == 0 else act * y
        dn = _dot(act.astype(BF16), wdn_ref[off:off + cw, :])
        acc = dn if acc is None else acc + dn
    o_ref[0] = x_ref[0] + mod[5:6] * acc


def _ffn_call(x, mod, w):
    b, L, _ = x.shape
    tl = TL_FF
    hb = tl // HALO
    nh = L // HALO
    consts = [w["gffn"], w["wup"], w["wconv"], w["bconv"], w["wdn"]]
    return pl.pallas_call(
        functools.partial(_ffn_kernel, tl=tl), grid=(b, L // tl),
        in_specs=[pl.BlockSpec((1, tl, D_MODEL), lambda bi, i: (bi, i, 0)),
                  pl.BlockSpec((1, HALO, D_MODEL), lambda bi, i: (bi, jnp.maximum(i * hb - 1, 0), 0)),
                  pl.BlockSpec((1, HALO, D_MODEL), lambda bi, i: (bi, jnp.minimum((i + 1) * hb, nh - 1), 0)),
                  pl.BlockSpec((1, 8, D_MODEL), lambda bi, i: (bi, 0, 0))] + [_const_spec(a.shape) for a in consts],
        out_specs=pl.BlockSpec((1, tl, D_MODEL), lambda bi, i: (bi, i, 0)),
        out_shape=jax.ShapeDtypeStruct(x.shape, F32),
        scratch_shapes=[pltpu.VMEM((2, tl + 2 * HALO, FF_CW), F32)],
        compiler_params=_params("parallel", "parallel"), name="ffn",
    )(x, x, x, mod, *consts)


def _final_kernel(x_ref, g_ref, o_ref):
    o_ref[0] = _rms(x_ref[0]) * g_ref[...]


def _final_call(x, g):
    b, L, _ = x.shape
    tl = 512
    spec = pl.BlockSpec((1, tl, D_MODEL), lambda bi, i: (bi, i, 0))
    return pl.pallas_call(
        _final_kernel, grid=(b, L // tl), in_specs=[spec, pl.BlockSpec((1, D_MODEL), lambda bi, i: (0, 0))],
        out_specs=spec, out_shape=jax.ShapeDtypeStruct(x.shape, F32),
        compiler_params=_params("parallel", "parallel"), name="final_norm",
    )(x, g)


def _layer_weights(p, l):
    w_in = p["w_in"][l]
    col = lambda i: w_in[:, IN_OFFS[i]:IN_OFFS[i + 1]]
    half = ROPE // 2
    kr = col(2)
    z = lambda n: jnp.zeros((D_MODEL, n), F32)
    wkr = jnp.concatenate([z(NOPE), kr[:, :half], kr[:, half:], z(LANE - NOPE - ROPE),
                           z(NOPE), kr[:, half:], kr[:, :half], z(LANE - NOPE - ROPE)], axis=1)
    wq = p["w_q_up"][l].reshape(Q_LORA, N_HEADS, NOPE + ROPE)
    zq = lambda n: jnp.zeros((Q_LORA, N_HEADS, n), F32)
    x1, x2 = wq[:, :, NOPE:NOPE + half], wq[:, :, NOPE + half:]
    wqa = jnp.concatenate([wq[:, :, :NOPE], x1, x2, zq(LANE - NOPE - ROPE)], axis=2).reshape(Q_LORA, N_HEADS * LANE)
    wqb = jnp.concatenate([zq(NOPE), x2, x1, zq(LANE - NOPE - ROPE)], axis=2).reshape(Q_LORA, N_HEADS * LANE)
    wkv = p["w_kv_up"][l].reshape(KV_LORA, N_HEADS, 2 * NOPE)
    wk = jnp.concatenate([wkv[:, :, :NOPE], jnp.zeros((KV_LORA, N_HEADS, LANE - NOPE), F32)],
                         axis=2).reshape(KV_LORA, N_HEADS * LANE)
    wvt = wkv[:, :, NOPE:].reshape(KV_LORA, MIX).T
    bf = lambda a: a.astype(BF16)
    return dict(
        gmix=p["g_mix"][l][None], gq=p["g_q_lat"][l][None], gkv=p["g_kv_lat"][l][None],
        wlat=bf(jnp.concatenate([col(0), col(1)], axis=1)), wkr=bf(wkr), wut=bf(col(3).T),
        whqv=bf(jnp.concatenate([col(4), col(7)], axis=1)), whog=bf(col(8)),
        whff=bf(jnp.concatenate([col(5), col(6)], axis=1)), wgate=bf(col(9)),
        wqa=bf(wqa), wqb=bf(wqb), wk=bf(wk), wvt=bf(wvt),
        wglut=bf(p["w_glu"][l].T), bglu=p["b_glu"][l][:, None], ghg=p["g_hg_out"][l][None],
        wbr=bf(p["w_branch"][l]), wout=bf(p["w_out"][l]),
        gffn=p["g_ffn"][l][None], wup=bf(p["w_ffn_up"][l]), wconv=p["w_ffn_conv"][l],
        bconv=p["b_ffn_conv"][l][None], wdn=bf(p["w_ffn_down"][l]),
    )


def _rope_tables(L):
    half = ROPE // 2
    inv_freq = 1.0 / (ROPE_BASE ** (jnp.arange(0, ROPE, 2, dtype=F32) / ROPE))
    ang = jnp.arange(L, dtype=F32)[:, None] * inv_freq[None, :]
    cos, sin = jnp.cos(ang), jnp.sin(ang)
    one, zero = jnp.ones((L, NOPE), F32), jnp.zeros((L, NOPE), F32)
    pad = jnp.zeros((L, LANE - NOPE - ROPE), F32)
    return (jnp.concatenate([one, cos, cos, pad], axis=1), jnp.concatenate([zero, -sin, sin, pad], axis=1))


def _trunk(x, mods, weights, s5ops, lb, g_final):
    b, L, _ = x.shape
    cos_t, sin_t = _rope_tables(L)
    for l in range(DEPTH):
        w = weights[l]
        q, k, vt, ut, hqv, hog, hff, gp = _inproj_call(x, mods[l], w, cos_t, sin_t)
        zat = _flash_call(q, k, vt)
        yt = _s5_call(ut, s5ops[l])
        of, ob = _hg_call(hqv, hff, lb[0, l][None], lb[1, l][None])
        x = _merge_call(x, mods[l], zat, yt, of, ob, hog, gp, w)
        x = _ffn_call(x, mods[l], w)
    return _final_call(x, g_final[None])


def kernel(x_prompt, x_sample, c_prompt, c_sample, w_ada, b_ada, g_mix, w_in, g_q_lat, w_q_up, g_kv_lat, w_kv_up,
           s5_lam_re, s5_lam_im, s5_log_dt, s5_b_re, s5_b_im, s5_c_re, s5_c_im, s5_d, w_glu, b_glu, hg_lb_logits,
           g_hg_out, w_branch, w_out, g_ffn, w_ffn_up, w_ffn_conv, b_ffn_conv, w_ffn_down, g_final):
    p = dict(g_mix=g_mix, w_in=w_in, g_q_lat=g_q_lat, w_q_up=w_q_up, g_kv_lat=g_kv_lat, w_kv_up=w_kv_up,
             w_glu=w_glu, b_glu=b_glu, g_hg_out=g_hg_out, w_branch=w_branch, w_out=w_out, g_ffn=g_ffn,
             w_ffn_up=w_ffn_up, w_ffn_conv=w_ffn_conv, b_ffn_conv=b_ffn_conv, w_ffn_down=w_ffn_down)
    depth = w_in.shape[0]
    assert depth == DEPTH
    bp, bs = c_prompt.shape[0], c_sample.shape[0]
    rows = -(-(bp + bs) // 8) * 8
    c_all = jnp.concatenate([c_prompt, c_sample, jnp.zeros((rows - bp - bs, D_MODEL), F32)], axis=0)
    mod_all = _ada_call(c_all, w_ada, b_ada)

    def mods_for(lo, n):
        m = mod_all[:, lo:lo + n].reshape(DEPTH, n, 6, D_MODEL)
        return jnp.pad(m, ((0, 0), (0, 0), (0, 2), (0, 0)))

    weights = [_layer_weights(p, l) for l in range(DEPTH)]
    s5ops = [_s5_operators(s5_lam_re[l], s5_lam_im[l], s5_log_dt[l], s5_b_re[l], s5_b_im[l],
                           s5_c_re[l], s5_c_im[l], s5_d[l]) for l in range(DEPTH)]
    gam = jax.nn.softmax(hg_lb_logits.astype(F32), axis=1)
    lb = jnp.cumsum(gam, axis=1) - gam[:, :1]
    y_prompt = _trunk(x_prompt, mods_for(0, bp), weights, s5ops, lb, g_final)
    y_sample = _trunk(x_sample, mods_for(bp, bs), weights, s5ops, lb, g_final)
    return (y_prompt, y_sample)
```

```python
import functools
import math

import jax
import jax.numpy as jnp
from jax import lax
from jax.experimental import pallas as pl
from jax.experimental.pallas import tpu as pltpu

F32 = jnp.float32
BF16 = jnp.bfloat16

D_MODEL = 1024
DEPTH = 4
MIX = 512
N_HEADS = 8
NOPE = 64
ROPE = 32
Q_LORA = 384
KV_LORA = 256
ROPE_BASE = 10000.0
S5_GROUPS = 32
S5_GROUP = 16
S5_STATE = 64
S5_CHUNK = 128
HG_HEADS = 4
HG_D = 128
HG_CHUNK = 64
HG_BLOCK = 16
D_FF = 2816
EPS = 1e-6
ATTN_SCALE = 1.0 / math.sqrt(NOPE + ROPE)
LOG2E = 1.4426950408889634
IN_OFFS = (0, 384, 640, 672, 1184, 1696, 2208, 2720, 3232, 3744, 6816)

LANE = 128
VMEM_LIMIT = 56 * 1024 * 1024

TL_IN = 256
TQ = 256
TK = 512
FLASH_UNROLL = 8
TL_HG = 256
TL_MG = 256
TL_FF = 256
FF_CW = 768
HALO = 8


def _dot(a, b):
    return jnp.dot(a, b, preferred_element_type=F32)


def _dot_nt(a, b):
    return lax.dot_general(a, b, (((1,), (1,)), ((), ())), preferred_element_type=F32)


def _dot_tn(a, b):
    return lax.dot_general(a, b, (((0,), (0,)), ((), ())), preferred_element_type=F32)


def _sigmoid(x):
    return 1.0 / (1.0 + jnp.exp(-x))


def _rms(x):
    return x * lax.rsqrt(jnp.mean(x * x, axis=-1, keepdims=True) + EPS)


def _gelu_tanh(x):
    return 0.5 * x * (1.0 + jnp.tanh(math.sqrt(2.0 / math.pi) * (x + 0.044715 * (x * x * x))))


def _params(*sem):
    return pltpu.CompilerParams(dimension_semantics=sem, vmem_limit_bytes=VMEM_LIMIT)


def _const_spec(shape):
    nd = len(shape)
    return pl.BlockSpec(shape, lambda *_: (0,) * nd, pipeline_mode=pl.Buffered(1))


def _ada_kernel(c_ref, w_ref, b_ref, o_ref):
    c = c_ref[...]
    a = (c * _sigmoid(c)).astype(BF16)
    o_ref[0] = _dot(a, w_ref[0].astype(BF16)) + b_ref[0]


def _ada_call(c_all, w_ada, b_ada):
    nb = 1536
    rows = c_all.shape[0]
    return pl.pallas_call(
        _ada_kernel,
        grid=(DEPTH, 6 * D_MODEL // nb),
        in_specs=[pl.BlockSpec((rows, D_MODEL), lambda l, j: (0, 0)),
                  pl.BlockSpec((1, D_MODEL, nb), lambda l, j: (l, 0, j)),
                  pl.BlockSpec((1, 1, nb), lambda l, j: (l, 0, j))],
        out_specs=pl.BlockSpec((1, rows, nb), lambda l, j: (l, 0, j)),
        out_shape=jax.ShapeDtypeStruct((DEPTH, rows, 6 * D_MODEL), F32),
        compiler_params=_params("arbitrary", "arbitrary"),
        name="ada_mod",
    )(c_all, w_ada, b_ada.reshape(DEPTH, 1, 6 * D_MODEL))


def _inproj_kernel(x_ref, mod_ref, gmix_ref, cos_ref, sin_ref,
                   wlat_ref, wkr_ref, wut_ref, whqv_ref, whog_ref, whff_ref, wgate_ref,
                   gq_ref, gkv_ref, wqa_ref, wqb_ref, wk_ref, wvt_ref,
                   q_ref, k_ref, vt_ref, ut_ref, hqv_ref, hog_ref, hff_ref, gp_ref):
    mod = mod_ref[0]
    h = _rms(x_ref[0]) * gmix_ref[...] * (1.0 + mod[1:2]) + mod[0:1]
    hb = h.astype(BF16)
    lat = _dot(hb, wlat_ref[...])
    qn = (_rms(lat[:, :Q_LORA]) * gq_ref[...]).astype(BF16)
    kvn = (_rms(lat[:, Q_LORA:]) * gkv_ref[...]).astype(BF16)
    cos = cos_ref[...]
    sin = sin_ref[...]
    qa = _dot(qn, wqa_ref[...])
    qb = _dot(qn, wqb_ref[...])
    krp = _dot(hb, wkr_ref[...])
    kr = krp[:, :LANE] * cos + krp[:, LANE:] * sin
    kn = _dot(kvn, wk_ref[...])
    for hd in range(N_HEADS):
        sl = slice(hd * LANE, (hd + 1) * LANE)
        q_ref[0, :, sl] = ((qa[:, sl] * cos + qb[:, sl] * sin) * (ATTN_SCALE * LOG2E)).astype(BF16)
        k_ref[0, :, sl] = (kn[:, sl] + kr).astype(BF16)
    vt_ref[0] = _dot_nt(wvt_ref[...], kvn).astype(BF16)
    ut_ref[0] = _dot_nt(wut_ref[...], hb).astype(BF16)
    hqv_ref[0] = _dot(hb, whqv_ref[...]).astype(BF16)
    hog_ref[0] = _dot(hb, whog_ref[...]).astype(BF16)
    hff_ref[0] = _dot(hb, whff_ref[...])
    gp_ref[0] = _dot(hb, wgate_ref[...]).astype(BF16)


def _inproj_call(x, mod, w, cos_t, sin_t):
    b, L, _ = x.shape
    tl = TL_IN
    tok = lambda n: pl.BlockSpec((1, tl, n), lambda bi, i: (bi, i, 0))
    tr = lambda n: pl.BlockSpec((1, n, tl), lambda bi, i: (bi, 0, i))
    weights = [w["gmix"], None, None, w["wlat"], w["wkr"], w["wut"], w["whqv"], w["whog"], w["whff"], w["wgate"],
               w["gq"], w["gkv"], w["wqa"], w["wqb"], w["wk"], w["wvt"]]
    in_specs = [tok(D_MODEL), pl.BlockSpec((1, 8, D_MODEL), lambda bi, i: (bi, 0, 0))]
    args = [x, mod]
    for a in weights:
        if a is None:
            continue
        in_specs.append(_const_spec(a.shape))
        args.append(a)
    in_specs[3:3] = [pl.BlockSpec((tl, LANE), lambda bi, i: (i, 0))] * 2
    args[3:3] = [cos_t, sin_t]
    out_shape = [jax.ShapeDtypeStruct((b, L, 1024), BF16), jax.ShapeDtypeStruct((b, L, 1024), BF16),
                 jax.ShapeDtypeStruct((b, MIX, L), BF16), jax.ShapeDtypeStruct((b, MIX, L), BF16),
                 jax.ShapeDtypeStruct((b, L, 1024), BF16), jax.ShapeDtypeStruct((b, L, MIX), BF16),
                 jax.ShapeDtypeStruct((b, L, 1024), F32), jax.ShapeDtypeStruct((b, L, 3 * D_MODEL), BF16)]
    out_specs = [tok(1024), tok(1024), tr(MIX), tr(MIX), tok(1024), tok(MIX), tok(1024), tok(3 * D_MODEL)]
    return pl.pallas_call(
        _inproj_kernel, grid=(b, L // tl), in_specs=in_specs, out_specs=out_specs, out_shape=out_shape,
        compiler_params=_params("parallel", "parallel"), name="inproj",
    )(*args)


def _flash_kernel(q_ref, k_ref, vt_ref, o_ref, s_ref, *, tk, nk):
    tq = q_ref.shape[1]
    half = NOPE
    ones = jnp.ones((16, tk), BF16)

    def scores(buf, j):
        off = pl.multiple_of(j * tk, tk)
        for hh in range(2):
            s_ref[buf, hh] = _dot_nt(k_ref[0, pl.ds(off, tk), hh * LANE:(hh + 1) * LANE],
                                     q_ref[0, :, hh * LANE:(hh + 1) * LANE])

    def process(buf, j, carry):
        off = pl.multiple_of(j * tk, tk)
        new = []
        for hh in range(2):
            m, acc = carry[hh]
            s = s_ref[buf, hh]
            mn = jnp.maximum(m, jnp.max(s, axis=0, keepdims=True))
            p = jnp.exp2(s - mn).astype(BF16)
            a = jnp.exp2(m - mn)
            v = jnp.concatenate([vt_ref[0, hh * half:(hh + 1) * half, pl.ds(off, tk)], ones], axis=0)
            new.append((mn, a * acc + _dot(v, p)))
        return tuple(new)

    def body(i, carry):
        for u in range(FLASH_UNROLL):
            j = i * FLASH_UNROLL + u
            scores(1 - u % 2, j + 1)
            carry = process(u % 2, j, carry)
        return carry

    scores(0, 0)
    init = tuple((jnp.full((1, tq), -1e30, F32), jnp.zeros((half + 16, tq), F32)) for _ in range(2))
    n_loop = (nk - 1) // FLASH_UNROLL
    res = lax.fori_loop(0, n_loop, body, init)
    for j in range(n_loop * FLASH_UNROLL, nk):
        if j + 1 < nk:
            scores(1 - j % 2, j + 1)
        res = process(j % 2, j, res)
    for hh in range(2):
        acc = res[hh][1]
        o_ref[0, hh * half:(hh + 1) * half, :] = (acc[:half] / acc[half:half + 1]).astype(BF16)


def _flash_call(q, k, vt):
    b, L, _ = q.shape
    tq, tk = min(TQ, L), min(TK, L)
    return pl.pallas_call(
        functools.partial(_flash_kernel, tk=tk, nk=L // tk),
        scratch_shapes=[pltpu.VMEM((2, 2, tk, tq), F32)],
        grid=(b, N_HEADS // 2, L // tq),
        in_specs=[pl.BlockSpec((1, tq, 2 * LANE), lambda bi, p, i: (bi, i, p)),
                  pl.BlockSpec((1, L, 2 * LANE), lambda bi, p, i: (bi, 0, p)),
                  pl.BlockSpec((1, LANE, L), lambda bi, p, i: (bi, p, 0))],
        out_specs=pl.BlockSpec((1, LANE, tq), lambda bi, p, i: (bi, p, i)),
        out_shape=jax.ShapeDtypeStruct((b, MIX, L), BF16),
        compiler_params=_params("parallel", "parallel", "arbitrary"), name="flash",
    )(q, k, vt)


def _s5_kernel(u_ref, t_ref, f_ref, e_ref, a_ref, y_ref, sloc_ref, st_ref, *, nb, nct):
    m = nb * nct
    ucat = jnp.concatenate([u_ref[:, 0, hi].reshape(m, S5_CHUNK) for hi in range(S5_GROUP)], axis=1)
    y = _dot(ucat, t_ref[0])
    sloc_ref[...] = _dot(ucat, f_ref[0])
    a = a_ref[0]
    arf, aif, arb, aib = a[0:1], a[1:2], a[2:3], a[3:4]

    sb = 8 if nct % 8 == 0 else nct
    nblk = nct // sb

    def body(blk, carry):
        new = []
        for bi in range(nb):
            xrf, xif, xrb, xib = carry[bi]
            rf = pl.multiple_of(bi * nct + blk * sb, sb)
            rb = pl.multiple_of(bi * nct + (nblk - 1 - blk) * sb, sb)
            lf = sloc_ref[pl.ds(rf, sb), 0:2 * LANE]
            lb = sloc_ref[pl.ds(rb, sb), 2 * LANE:4 * LANE]
            frows, brows = [], []
            for r in range(sb):
                frows.append((xrf, xif))
                xrf, xif = (arf * xrf - aif * xif + lf[r:r + 1, 0:LANE],
                            arf * xif + aif * xrf + lf[r:r + 1, LANE:2 * LANE])
            for r in range(sb - 1, -1, -1):
                brows.append((xrb, xib))
                xrb, xib = (arb * xrb - aib * xib + lb[r:r + 1, 0:LANE],
                            arb * xib + aib * xrb + lb[r:r + 1, LANE:2 * LANE])
            brows = brows[::-1]
            st_ref[pl.ds(rf, sb), 0:LANE] = jnp.concatenate([t[0] for t in frows], axis=0)
            st_ref[pl.ds(rf, sb), LANE:2 * LANE] = jnp.concatenate([t[1] for t in frows], axis=0)
            st_ref[pl.ds(rb, sb), 2 * LANE:3 * LANE] = jnp.concatenate([t[0] for t in brows], axis=0)
            st_ref[pl.ds(rb, sb), 3 * LANE:4 * LANE] = jnp.concatenate([t[1] for t in brows], axis=0)
            new.append((xrf, xif, xrb, xib))
        return tuple(new)

    z = jnp.zeros((1, LANE), F32)
    lax.fori_loop(0, nblk, body, tuple((z, z, z, z) for _ in range(nb)))
    y = y + _dot(st_ref[...].astype(BF16), e_ref[0])
    for ho in range(S5_GROUP):
        y_ref[:, 0, ho] = y[:, ho * S5_CHUNK:(ho + 1) * S5_CHUNK].reshape(nb, nct, S5_CHUNK)


def _s5_call(ut, ops):
    b, _, L = ut.shape
    nct = L // S5_CHUNK
    u5 = ut.reshape(b, S5_GROUPS, S5_GROUP, nct, S5_CHUNK)
    gw = S5_GROUP * S5_CHUNK
    blk = pl.BlockSpec((b, 1, S5_GROUP, nct, S5_CHUNK), lambda g: (0, g, 0, 0, 0))
    y = pl.pallas_call(
        functools.partial(_s5_kernel, nb=b, nct=nct),
        grid=(S5_GROUPS,),
        in_specs=[blk,
                  pl.BlockSpec((1, gw, gw), lambda g: (g, 0, 0)),
                  pl.BlockSpec((1, gw, 4 * LANE), lambda g: (g, 0, 0)),
                  pl.BlockSpec((1, 4 * LANE, gw), lambda g: (g, 0, 0)),
                  pl.BlockSpec((1, 8, LANE), lambda g: (g, 0, 0))],
        out_specs=blk,
        out_shape=jax.ShapeDtypeStruct(u5.shape, F32),
        scratch_shapes=[pltpu.VMEM((b * nct, 4 * LANE), F32), pltpu.VMEM((b * nct, 4 * LANE), F32)],
        compiler_params=_params("parallel"), name="s5",
    )(u5, ops["t"], ops["f"], ops["e"], ops["a"])
    return y.reshape(b, MIX, L)


def _toeplitz_kernel(w_ref, t_ref):
    c = S5_CHUNK

    def body(hi, _):
        r0 = pl.multiple_of(hi * c, c)
        for ho in range(S5_GROUP):
            row = w_ref[0, hi, ho:ho + 1, :]
            skew = pltpu.roll(jnp.broadcast_to(row, (c, 2 * c)), 0, 1, stride=1, stride_axis=0)
            t_ref[0, pl.ds(r0, c), ho * c:(ho + 1) * c] = skew[:, :c].astype(BF16)
        return 0

    lax.fori_loop(0, S5_GROUP, body, 0)


def _toeplitz_call(w):
    g = w.shape[0]
    gw = S5_GROUP * S5_CHUNK
    return pl.pallas_call(
        _toeplitz_kernel, grid=(g,),
        in_specs=[pl.BlockSpec((1, S5_GROUP, S5_GROUP, 2 * S5_CHUNK), lambda i: (i, 0, 0, 0))],
        out_specs=pl.BlockSpec((1, gw, gw), lambda i: (i, 0, 0)),
        out_shape=jax.ShapeDtypeStruct((g, gw, gw), BF16),
        compiler_params=_params("parallel"), name="s5_toeplitz",
    )(w)


def _s5_operators(lam_re, lam_im, log_dt, b_re, b_im, c_re, c_im, d):
    hp = lax.Precision.HIGHEST
    G, P, GS, C = S5_GROUPS, S5_STATE, S5_GROUP, S5_CHUNK
    lam_re, lam_im = lam_re.astype(F32), lam_im.astype(F32)
    dt = jnp.exp(log_dt.astype(F32))[..., None]
    zr, zi = lam_re * dt, lam_im * dt
    kk = jnp.arange(C + 1, dtype=F32)[None, None, :, None]
    mag = jnp.exp(zr[:, :, None, :] * kk)
    ang = zi[:, :, None, :] * kk
    pw_re, pw_im = mag * jnp.cos(ang), mag * jnp.sin(ang)
    lb_re, lb_im = pw_re[:, :, 1], pw_im[:, :, 1]
    den = lam_re * lam_re + lam_im * lam_im
    nr, ni = lb_re - 1.0, lb_im
    cr = (nr * lam_re + ni * lam_im) / den
    ci = (ni * lam_re - nr * lam_im) / den
    bb_re = cr[..., None] * b_re - ci[..., None] * b_im
    bb_im = cr[..., None] * b_im + ci[..., None] * b_re
    c_re, c_im = c_re.astype(F32), c_im.astype(F32)

    def kern(dr):
        pr, pi_ = pw_re[dr, :, :C], pw_im[dr, :, :C]
        cp_re = c_re[dr][:, None] * pr[:, :, None, :] - c_im[dr][:, None] * pi_[:, :, None, :]
        cp_im = c_re[dr][:, None] * pi_[:, :, None, :] + c_im[dr][:, None] * pr[:, :, None, :]
        return (jnp.einsum('gkhp,gpi->gkhi', cp_re, bb_re[dr], precision=hp)
                - jnp.einsum('gkhp,gpi->gkhi', cp_im, bb_im[dr], precision=hp))

    kf, kb = kern(0), kern(1)
    k0 = kf[:, 0] + kb[:, 0] + jnp.eye(GS, dtype=F32)[None] * d.astype(F32).reshape(G, GS)[:, :, None]
    kfull = jnp.concatenate([kb[:, 1:][:, ::-1], k0[:, None], kf[:, 1:], jnp.zeros((G, 1, GS, GS), F32)], axis=1)
    t_op = _toeplitz_call(jnp.roll(kfull, -(C - 1), axis=1).transpose(0, 3, 2, 1))

    def f_part(dr, idx):
        pr, pi_ = pw_re[dr][:, idx], pw_im[dr][:, idx]
        br, bi = bb_re[dr].transpose(0, 2, 1), bb_im[dr].transpose(0, 2, 1)
        re = pr[:, None] * br[:, :, None] - pi_[:, None] * bi[:, :, None]
        im = pr[:, None] * bi[:, :, None] + pi_[:, None] * br[:, :, None]
        return re.reshape(G, GS * C, P), im.reshape(G, GS * C, P)

    tau = jnp.arange(C)
    ffr, ffi = f_part(0, C - 1 - tau)
    fbr, fbi = f_part(1, tau)
    padl = lambda a: jnp.pad(a, ((0, 0), (0, 0), (0, LANE - P)))
    f_op = jnp.concatenate([padl(ffr), padl(ffi), padl(fbr), padl(fbi)], axis=-1).astype(BF16)

    def e_part(dr, idx):
        pr, pi_ = pw_re[dr][:, idx], pw_im[dr][:, idx]
        cr_, ci_ = c_re[dr].transpose(0, 2, 1), c_im[dr].transpose(0, 2, 1)
        prt, pit = pr.transpose(0, 2, 1), pi_.transpose(0, 2, 1)
        re = cr_[:, :, :, None] * prt[:, :, None, :] - ci_[:, :, :, None] * pit[:, :, None, :]
        im = cr_[:, :, :, None] * pit[:, :, None, :] + ci_[:, :, :, None] * prt[:, :, None, :]
        return re.reshape(G, P, GS * C), -im.reshape(G, P, GS * C)

    efr, efi = e_part(0, tau + 1)
    ebr, ebi = e_part(1, C - tau)
    padr = lambda a: jnp.pad(a, ((0, 0), (0, LANE - P), (0, 0)))
    e_op = jnp.concatenate([padr(efr), padr(efi), padr(ebr), padr(ebi)], axis=1).astype(BF16)

    arows = [pw_re[0, :, C], pw_im[0, :, C], pw_re[1, :, C], pw_im[1, :, C]]
    a_op = jnp.stack([jnp.pad(r, ((0, 0), (0, LANE - P))) for r in arows]
                     + [jnp.zeros((G, LANE), F32)] * 4, axis=1)
    return dict(t=t_op, f=f_op, e=e_op, a=a_op)


def _split3(x):
    hi = x.astype(BF16)
    r1 = x - hi.astype(F32)
    mid = r1.astype(BF16)
    lo = (r1 - mid.astype(F32)).astype(BF16)
    return hi, mid, lo


def _hg_consts(fwd):
    c, bk = HG_CHUNK, HG_BLOCK
    t = lax.broadcasted_iota(jnp.int32, (c, c), 0)
    s = lax.broadcasted_iota(jnp.int32, (c, c), 1)
    bt, bs = t // bk, s // bk
    if fwd:
        tri = (s <= t)
        m0 = (bt == bs) & (s <= t)
        m1 = ((bt == 1) & (bs == 0)) | ((bt == 3) & (bs == 2))
        m2 = (bt >= 2) & (bs < 2)
    else:
        tri = (s >= t)
        m0 = (bt == bs) & (s >= t)
        m1 = ((bt == 0) & (bs == 1)) | ((bt == 2) & (bs == 3))
        m2 = (bt < 2) & (bs >= 2)
    return jnp.where(tri, 1.0, 0.0).astype(BF16), m0, m1, m2


def _hg_chunks(streams):
    bk, nblk = HG_BLOCK, HG_CHUNK // HG_BLOCK
    zero = jnp.zeros((bk, HG_D), F32)

    gates = []
    for hq, hv, hf, lb, st, fwd, consts in streams:
        q = hq * _sigmoid(hq)
        f = lb + (1.0 - lb) * _sigmoid(hf)
        logf = jnp.log(f)
        p1, p2, p3 = _split3(logf)
        tri = consts[0]
        cum = _dot(tri, p1) + _dot(tri, p2) + _dot(tri, p3)
        gates.append((q, 1.0 - f, logf, cum))

    scores = []
    for (hq, hv, hf, lb, st, fwd, consts), (q, k, logf, cum) in zip(streams, gates):
        _, m0, m1, m2 = consts
        cexc = cum - logf
        cumb = [cum[bk * i:bk * (i + 1)] for i in range(nblk)]
        edge = 0 if fwd else bk - 1

        def ref(row, cexc=cexc):
            return jnp.broadcast_to(cexc[row:row + 1, :], (bk, HG_D))

        levels = []
        r0 = [ref(bk * i + edge) for i in range(nblk)]
        levels.append(([cumb[i] - r0[i] for i in range(nblk)], [r0[i] - cumb[i] for i in range(nblk)], m0))
        late1, early1 = ((1, 3), (0, 2)) if fwd else ((0, 2), (1, 3))
        eq, ek = [zero] * nblk, [zero] * nblk
        for lt, er in zip(late1, early1):
            r = ref(bk * lt + edge)
            eq[lt] = cumb[lt] - r
            ek[er] = r - cumb[er]
        levels.append((eq, ek, m1))
        late2, early2 = ((2, 3), (0, 1)) if fwd else ((1, 0), (2, 3))
        r = ref(bk * late2[0] + edge)
        eq, ek = [zero] * nblk, [zero] * nblk
        for lt in late2:
            eq[lt] = cumb[lt] - r
        for er in early2:
            ek[er] = r - cumb[er]
        levels.append((eq, ek, m2))

        a = jnp.zeros((HG_CHUNK, HG_CHUNK), F32)
        for eq, ek, msk in levels:
            qs = (q * jnp.exp(jnp.concatenate(eq, axis=0))).astype(BF16)
            ks = (k * jnp.exp(jnp.concatenate(ek, axis=0))).astype(BF16)
            a = jnp.where(msk, _dot_nt(qs, ks), a)
        scores.append(a)

    outs = []
    for (hq, hv, hf, lb, st, fwd, consts), (q, k, logf, cum), a in zip(streams, gates, scores):
        vb = hv.astype(BF16)
        o = _dot(a.astype(BF16), vb) + _dot_nt((q * jnp.exp(cum)).astype(BF16), st.astype(BF16))
        last = cum[HG_CHUNK - 1:HG_CHUNK] if fwd else cum[0:1]
        kst = (k * jnp.exp(last - cum)).astype(BF16)
        outs.append((o, st * jnp.exp(last) + _dot_tn(vb, kst)))
    return outs


def _hg_kernel(qvf_ref, ff_ref, qvb_ref, fb_ref, lbf_ref, lbb_ref, of_ref, ob_ref, sf_ref, sb_ref, *, ncc):
    @pl.when(pl.program_id(1) == 0)
    def _():
        sf_ref[...] = jnp.zeros_like(sf_ref)
        sb_ref[...] = jnp.zeros_like(sb_ref)

    cf = _hg_consts(True)
    cb = _hg_consts(False)

    def body(cc, _):
        rf = pl.multiple_of(cc * HG_CHUNK, HG_CHUNK)
        rb = pl.multiple_of((ncc - 1 - cc) * HG_CHUNK, HG_CHUNK)
        streams = []
        for hd in range(HG_HEADS):
            sl = slice(hd * HG_D, (hd + 1) * HG_D)
            sv = slice(MIX + hd * HG_D, MIX + (hd + 1) * HG_D)
            streams.append((qvf_ref[0, pl.ds(rf, HG_CHUNK), sl].astype(F32),
                            qvf_ref[0, pl.ds(rf, HG_CHUNK), sv].astype(F32),
                            ff_ref[0, pl.ds(rf, HG_CHUNK), sl], lbf_ref[:, sl], sf_ref[hd], True, cf))
            streams.append((qvb_ref[0, pl.ds(rb, HG_CHUNK), sl].astype(F32),
                            qvb_ref[0, pl.ds(rb, HG_CHUNK), sv].astype(F32),
                            fb_ref[0, pl.ds(rb, HG_CHUNK), sl], lbb_ref[:, sl], sb_ref[hd], False, cb))
        outs = _hg_chunks(streams)
        for hd in range(HG_HEADS):
            sl = slice(hd * HG_D, (hd + 1) * HG_D)
            of_ref[0, pl.ds(rf, HG_CHUNK), sl], sf_ref[hd] = outs[2 * hd]
            ob_ref[0, pl.ds(rb, HG_CHUNK), sl], sb_ref[hd] = outs[2 * hd + 1]
        return 0

    lax.fori_loop(0, ncc, body, 0)


def _hg_call(hqv, hff, lbf, lbb):
    b, L, _ = hqv.shape
    tl = min(TL_HG, L)
    n = L // tl
    fwd = lambda w: pl.BlockSpec((1, tl, w), lambda bi, i: (bi, i, 0))
    bwd = lambda w, j: pl.BlockSpec((1, tl, w), lambda bi, i: (bi, n - 1 - i, j))
    return pl.pallas_call(
        functools.partial(_hg_kernel, ncc=tl // HG_CHUNK),
        grid=(b, n),
        in_specs=[fwd(2 * MIX), fwd(MIX), bwd(2 * MIX, 0), bwd(MIX, 1),
                  pl.BlockSpec((1, MIX), lambda bi, i: (0, 0)), pl.BlockSpec((1, MIX), lambda bi, i: (0, 0))],
        out_specs=[fwd(MIX), bwd(MIX, 0)],
        out_shape=[jax.ShapeDtypeStruct((b, L, MIX), F32)] * 2,
        scratch_shapes=[pltpu.VMEM((HG_HEADS, HG_D, HG_D), F32), pltpu.VMEM((HG_HEADS, HG_D, HG_D), F32)],
        compiler_params=_params("parallel", "arbitrary"), name="hgrn2",
    )(hqv, hff, hqv, hff, lbf, lbb)


def _merge_kernel(x_ref, mod_ref, zat_ref, yt_ref, of_ref, ob_ref, og_ref, gp_ref,
                  wglut_ref, bglu_ref, ghg_ref, wbr_ref, wout_ref, o_ref):
    g = _gelu_tanh(yt_ref[0])
    glu = _dot(wglut_ref[...], g.astype(BF16)) + bglu_ref[...]
    zbt = (g * _sigmoid(glu)).astype(BF16)
    o = of_ref[0] + ob_ref[0]
    on = jnp.concatenate([_rms(o[:, hd * HG_D:(hd + 1) * HG_D]) for hd in range(HG_HEADS)], axis=1)
    og = og_ref[0].astype(F32)
    zc = (on * ghg_ref[...] * (og * _sigmoid(og))).astype(BF16)
    bra = _dot_tn(zat_ref[0], wbr_ref[0])
    brb = _dot_tn(zbt, wbr_ref[1])
    brc = _dot(zc, wbr_ref[2])
    gp = gp_ref[0].astype(F32)
    mix = (_sigmoid(gp[:, :D_MODEL]) * bra + _sigmoid(gp[:, D_MODEL:2 * D_MODEL]) * brb
           + _sigmoid(gp[:, 2 * D_MODEL:]) * brc)
    out = _dot(mix.astype(BF16), wout_ref[...])
    o_ref[0] = x_ref[0] + mod_ref[0][2:3] * out


def _merge_call(x, mod, zat, yt, of, ob, hog, gp, w):
    b, L, _ = x.shape
    tl = TL_MG
    tok = lambda n: pl.BlockSpec((1, tl, n), lambda bi, i: (bi, i, 0))
    tr = pl.BlockSpec((1, MIX, tl), lambda bi, i: (bi, 0, i))
    consts = [w["wglut"], w["bglu"], w["ghg"], w["wbr"], w["wout"]]
    return pl.pallas_call(
        _merge_kernel, grid=(b, L // tl),
        in_specs=[tok(D_MODEL), pl.BlockSpec((1, 8, D_MODEL), lambda bi, i: (bi, 0, 0)), tr, tr,
                  tok(MIX), tok(MIX), tok(MIX), tok(3 * D_MODEL)] + [_const_spec(a.shape) for a in consts],
        out_specs=tok(D_MODEL),
        out_shape=jax.ShapeDtypeStruct(x.shape, F32),
        compiler_params=_params("parallel", "parallel"), name="merge",
    )(x, mod, zat, yt, of, ob, hog, gp, *consts)


def _ff_chunks():
    out, off = [], 0
    while off < D_FF:
        w = min(FF_CW, D_FF - off)
        out.append((off, w))
        off += w
    return out


def _ffn_kernel(x_ref, xp_ref, xn_ref, mod_ref, gffn_ref, wup_ref, wconv_ref, bconv_ref, wdn_ref, o_ref,
                up_ref, h_ref, acc_ref, *, tl):
    i = pl.program_id(1)
    mod = mod_ref[0]

    def modulate(xv):
        return _rms(xv) * gffn_ref[...] * (1.0 + mod[4:5]) + mod[3:4]

    keep_prev = jnp.where(i > 0, 1.0, 0.0)
    keep_next = jnp.where(i < pl.num_programs(1) - 1, 1.0, 0.0)
    h_ref[...] = jnp.concatenate([modulate(xp_ref[0]) * keep_prev, modulate(x_ref[0]),
                                  modulate(xn_ref[0]) * keep_next], axis=0).astype(BF16)
    chunks = _ff_chunks()

    def up_proj(ci):
        off, cw = chunks[ci]
        for part in range(2):
            col = off + part * D_FF
            up_ref[ci % 2, part, :, 0:cw] = _dot(h_ref[...], wup_ref[:, col:col + cw])

    up_proj(0)
    for ci, (off, cw) in enumerate(chunks):
        if ci + 1 < len(chunks):
            up_proj(ci + 1)
        act = None
        for part in range(2):
            col = off + part * D_FF
            wc = wconv_ref[:, col:col + cw]
            buf = up_ref.at[ci % 2, part]
            y = (buf[pl.ds(HALO - 1, tl), 0:cw] * wc[0:1] + buf[pl.ds(HALO, tl), 0:cw] * wc[1:2]
                 + buf[pl.ds(HALO + 1, tl), 0:cw] * wc[2:3] + bconv_ref[:, col:col + cw])
            act = y * _sigmoid(y) if part == 0 else act * y
        dn = _dot(act.astype(BF16), wdn_ref[off:off + cw, :])
        if ci == 0:
            acc_ref[...] = dn
        else:
            acc_ref[...] += dn
    o_ref[0] = x_ref[0] + mod[5:6] * acc_ref[...]


def _ffn_call(x, mod, w):
    b, L, _ = x.shape
    tl = TL_FF
    hb = tl // HALO
    nh = L // HALO
    consts = [w["gffn"], w["wup"], w["wconv"], w["bconv"], w["wdn"]]
    return pl.pallas_call(
        functools.partial(_ffn_kernel, tl=tl), grid=(b, L // tl),
        in_specs=[pl.BlockSpec((1, tl, D_MODEL), lambda bi, i: (bi, i, 0)),
                  pl.BlockSpec((1, HALO, D_MODEL), lambda bi, i: (bi, jnp.maximum(i * hb - 1, 0), 0)),
                  pl.BlockSpec((1, HALO, D_MODEL), lambda bi, i: (bi, jnp.minimum((i + 1) * hb, nh - 1), 0)),
                  pl.BlockSpec((1, 8, D_MODEL), lambda bi, i: (bi, 0, 0))] + [_const_spec(a.shape) for a in consts],
        out_specs=pl.BlockSpec((1, tl, D_MODEL), lambda bi, i: (bi, i, 0)),
        out_shape=jax.ShapeDtypeStruct(x.shape, F32),
        scratch_shapes=[pltpu.VMEM((2, 2, tl + 2 * HALO, FF_CW), F32), pltpu.VMEM((tl + 2 * HALO, D_MODEL), BF16),
                        pltpu.VMEM((tl, D_MODEL), F32)],
        compiler_params=_params("parallel", "parallel"), name="ffn",
    )(x, x, x, mod, *consts)


def _final_kernel(x_ref, g_ref, o_ref):
    o_ref[0] = _rms(x_ref[0]) * g_ref[...]


def _final_call(x, g):
    b, L, _ = x.shape
    tl = 512
    spec = pl.BlockSpec((1, tl, D_MODEL), lambda bi, i: (bi, i, 0))
    return pl.pallas_call(
        _final_kernel, grid=(b, L // tl), in_specs=[spec, pl.BlockSpec((1, D_MODEL), lambda bi, i: (0, 0))],
        out_specs=spec, out_shape=jax.ShapeDtypeStruct(x.shape, F32),
        compiler_params=_params("parallel", "parallel"), name="final_norm",
    )(x, g)


def _layer_weights(p, l):
    w_in = p["w_in"][l]
    col = lambda i: w_in[:, IN_OFFS[i]:IN_OFFS[i + 1]]
    half = ROPE // 2
    kr = col(2)
    z = lambda n: jnp.zeros((D_MODEL, n), F32)
    wkr = jnp.concatenate([z(NOPE), kr[:, :half], kr[:, half:], z(LANE - NOPE - ROPE),
                           z(NOPE), kr[:, half:], kr[:, :half], z(LANE - NOPE - ROPE)], axis=1)
    wq = p["w_q_up"][l].reshape(Q_LORA, N_HEADS, NOPE + ROPE)
    zq = lambda n: jnp.zeros((Q_LORA, N_HEADS, n), F32)
    x1, x2 = wq[:, :, NOPE:NOPE + half], wq[:, :, NOPE + half:]
    wqa = jnp.concatenate([wq[:, :, :NOPE], x1, x2, zq(LANE - NOPE - ROPE)], axis=2).reshape(Q_LORA, N_HEADS * LANE)
    wqb = jnp.concatenate([zq(NOPE), x2, x1, zq(LANE - NOPE - ROPE)], axis=2).reshape(Q_LORA, N_HEADS * LANE)
    wkv = p["w_kv_up"][l].reshape(KV_LORA, N_HEADS, 2 * NOPE)
    wk = jnp.concatenate([wkv[:, :, :NOPE], jnp.zeros((KV_LORA, N_HEADS, LANE - NOPE), F32)],
                         axis=2).reshape(KV_LORA, N_HEADS * LANE)
    wvt = wkv[:, :, NOPE:].reshape(KV_LORA, MIX).T
    bf = lambda a: a.astype(BF16)
    return dict(
        gmix=p["g_mix"][l][None], gq=p["g_q_lat"][l][None], gkv=p["g_kv_lat"][l][None],
        wlat=bf(jnp.concatenate([col(0), col(1)], axis=1)), wkr=bf(wkr), wut=bf(col(3).T),
        whqv=bf(jnp.concatenate([col(4), col(7)], axis=1)), whog=bf(col(8)),
        whff=bf(jnp.concatenate([col(5), col(6)], axis=1)), wgate=bf(col(9)),
        wqa=bf(wqa), wqb=bf(wqb), wk=bf(wk), wvt=bf(wvt),
        wglut=bf(p["w_glu"][l].T), bglu=p["b_glu"][l][:, None], ghg=p["g_hg_out"][l][None],
        wbr=bf(p["w_branch"][l]), wout=bf(p["w_out"][l]),
        gffn=p["g_ffn"][l][None], wup=bf(p["w_ffn_up"][l]), wconv=p["w_ffn_conv"][l],
        bconv=p["b_ffn_conv"][l][None], wdn=bf(p["w_ffn_down"][l]),
    )


def _rope_tables(L):
    half = ROPE // 2
    inv_freq = 1.0 / (ROPE_BASE ** (jnp.arange(0, ROPE, 2, dtype=F32) / ROPE))
    ang = jnp.arange(L, dtype=F32)[:, None] * inv_freq[None, :]
    cos, sin = jnp.cos(ang), jnp.sin(ang)
    one, zero = jnp.ones((L, NOPE), F32), jnp.zeros((L, NOPE), F32)
    pad = jnp.zeros((L, LANE - NOPE - ROPE), F32)
    return (jnp.concatenate([one, cos, cos, pad], axis=1), jnp.concatenate([zero, -sin, sin, pad], axis=1))


def _trunk(x, mods, weights, s5ops, lb, g_final):
    b, L, _ = x.shape
    cos_t, sin_t = _rope_tables(L)
    for l in range(DEPTH):
        w = weights[l]
        q, k, vt, ut, hqv, hog, hff, gp = _inproj_call(x, mods[l], w, cos_t, sin_t)
        zat = _flash_call(q, k, vt)
        yt = _s5_call(ut, s5ops[l])
        of, ob = _hg_call(hqv, hff, lb[0, l][None], lb[1, l][None])
        x = _merge_call(x, mods[l], zat, yt, of, ob, hog, gp, w)
        x = _ffn_call(x, mods[l], w)
    return _final_call(x, g_final[None])


def kernel(x_prompt, x_sample, c_prompt, c_sample, w_ada, b_ada, g_mix, w_in, g_q_lat, w_q_up, g_kv_lat, w_kv_up,
           s5_lam_re, s5_lam_im, s5_log_dt, s5_b_re, s5_b_im, s5_c_re, s5_c_im, s5_d, w_glu, b_glu, hg_lb_logits,
           g_hg_out, w_branch, w_out, g_ffn, w_ffn_up, w_ffn_conv, b_ffn_conv, w_ffn_down, g_final):
    p = dict(g_mix=g_mix, w_in=w_in, g_q_lat=g_q_lat, w_q_up=w_q_up, g_kv_lat=g_kv_lat, w_kv_up=w_kv_up,
             w_glu=w_glu, b_glu=b_glu, g_hg_out=g_hg_out, w_branch=w_branch, w_out=w_out, g_ffn=g_ffn,
             w_ffn_up=w_ffn_up, w_ffn_conv=w_ffn_conv, b_ffn_conv=b_ffn_conv, w_ffn_down=w_ffn_down)
    depth = w_in.shape[0]
    assert depth == DEPTH
    bp, bs = c_prompt.shape[0], c_sample.shape[0]
    rows = -(-(bp + bs) // 8) * 8
    c_all = jnp.concatenate([c_prompt, c_sample, jnp.zeros((rows - bp - bs, D_MODEL), F32)], axis=0)
    mod_all = _ada_call(c_all, w_ada, b_ada)

    def mods_for(lo, n):
        m = mod_all[:, lo:lo + n].reshape(DEPTH, n, 6, D_MODEL)
        return jnp.pad(m, ((0, 0), (0, 0), (0, 2), (0, 0)))

    weights = [_layer_weights(p, l) for l in range(DEPTH)]
    s5ops = [_s5_operators(s5_lam_re[l], s5_lam_im[l], s5_log_dt[l], s5_b_re[l], s5_b_im[l],
                           s5_c_re[l], s5_c_im[l], s5_d[l]) for l in range(DEPTH)]
    gam = jax.nn.softmax(hg_lb_logits.astype(F32), axis=1)
    lb = jnp.cumsum(gam, axis=1) - gam[:, :1]
    y_prompt = _trunk(x_prompt, mods_for(0, bp), weights, s5ops, lb, g_final)
    y_sample = _trunk(x_sample, mods_for(bp, bs), weights, s5ops, lb, g_final)
    return (y_prompt, y_sample)
```

```python
import functools
import math

import jax
import jax.numpy as jnp
from jax import lax
from jax.experimental import pallas as pl
from jax.experimental.pallas import tpu as pltpu

F32 = jnp.float32
BF16 = jnp.bfloat16

D_MODEL = 1024
DEPTH = 4
MIX = 512
N_HEADS = 8
NOPE = 64
ROPE = 32
Q_LORA = 384
KV_LORA = 256
ROPE_BASE = 10000.0
S5_GROUPS = 32
S5_GROUP = 16
S5_STATE = 64
S5_CHUNK = 128
HG_HEADS = 4
HG_D = 128
HG_CHUNK = 64
D_FF = 2816
EPS = 1e-6
ATTN_SCALE = 1.0 / math.sqrt(NOPE + ROPE)
LOG2E = 1.4426950408889634
IN_OFFS = (0, 384, 640, 672, 1184, 1696, 2208, 2720, 3232, 3744, 6816)

LANE = 128
VMEM_LIMIT = 56 * 1024 * 1024

TL_IN = 256
TQ = 256
TK = 512
FLASH_UNROLL = 8
TL_HG = 256
TL_MG = 256
TL_FF = 256
FF_CW = 768
HALO = 8


def _dot(a, b):
    return jnp.dot(a, b, preferred_element_type=F32)


def _dot_nt(a, b):
    return lax.dot_general(a, b, (((1,), (1,)), ((), ())), preferred_element_type=F32)


def _dot_tn(a, b):
    return lax.dot_general(a, b, (((0,), (0,)), ((), ())), preferred_element_type=F32)


def _sigmoid(x):
    return 1.0 / (1.0 + jnp.exp(-x))


def _gate(x):
    return 0.5 * jnp.tanh(0.5 * x) + 0.5


def _rms(x):
    return x * lax.rsqrt(jnp.mean(x * x, axis=-1, keepdims=True) + EPS)


def _gelu_tanh(x):
    return 0.5 * x * (1.0 + jnp.tanh(math.sqrt(2.0 / math.pi) * (x + 0.044715 * (x * x * x))))


def _params(*sem):
    return pltpu.CompilerParams(dimension_semantics=sem, vmem_limit_bytes=VMEM_LIMIT)


def _const_spec(shape):
    nd = len(shape)
    return pl.BlockSpec(shape, lambda *_: (0,) * nd, pipeline_mode=pl.Buffered(1))


def _ada_kernel(c_ref, w_ref, b_ref, o_ref):
    c = c_ref[...]
    a = (c * _sigmoid(c)).astype(BF16)
    o_ref[0] = _dot(a, w_ref[0].astype(BF16)) + b_ref[0]


def _ada_call(c_all, w_ada, b_ada):
    nb = 1536
    rows = c_all.shape[0]
    return pl.pallas_call(
        _ada_kernel,
        grid=(DEPTH, 6 * D_MODEL // nb),
        in_specs=[pl.BlockSpec((rows, D_MODEL), lambda l, j: (0, 0)),
                  pl.BlockSpec((1, D_MODEL, nb), lambda l, j: (l, 0, j)),
                  pl.BlockSpec((1, 1, nb), lambda l, j: (l, 0, j))],
        out_specs=pl.BlockSpec((1, rows, nb), lambda l, j: (l, 0, j)),
        out_shape=jax.ShapeDtypeStruct((DEPTH, rows, 6 * D_MODEL), F32),
        compiler_params=_params("arbitrary", "arbitrary"),
        name="ada_mod",
    )(c_all, w_ada, b_ada.reshape(DEPTH, 1, 6 * D_MODEL))


def _inproj_kernel(x_ref, mod_ref, gmix_ref, cos_ref, sin_ref,
                   wlat_ref, wkr_ref, wut_ref, whqv_ref, whog_ref, whff_ref, wgate_ref,
                   gq_ref, gkv_ref, wqa_ref, wqb_ref, wk_ref, wvt_ref,
                   q_ref, k_ref, vt_ref, ut_ref, hqv_ref, hog_ref, hff_ref, gp_ref):
    mod = mod_ref[0]
    h = _rms(x_ref[0]) * gmix_ref[...] * (1.0 + mod[1:2]) + mod[0:1]
    hb = h.astype(BF16)
    lat = _dot(hb, wlat_ref[...])
    qn = (_rms(lat[:, :Q_LORA]) * gq_ref[...]).astype(BF16)
    kvn = (_rms(lat[:, Q_LORA:]) * gkv_ref[...]).astype(BF16)
    cos = cos_ref[...]
    sin = sin_ref[...]
    qa = _dot(qn, wqa_ref[...])
    qb = _dot(qn, wqb_ref[...])
    krp = _dot(hb, wkr_ref[...])
    kr = krp[:, :LANE] * cos + krp[:, LANE:] * sin
    kn = _dot(kvn, wk_ref[...])
    for hd in range(N_HEADS):
        sl = slice(hd * LANE, (hd + 1) * LANE)
        q_ref[0, :, sl] = ((qa[:, sl] * cos + qb[:, sl] * sin) * (ATTN_SCALE * LOG2E)).astype(BF16)
        k_ref[0, :, sl] = (kn[:, sl] + kr).astype(BF16)
    vt_ref[0] = _dot_nt(wvt_ref[...], kvn).astype(BF16)
    ut_ref[0] = _dot_nt(wut_ref[...], hb).astype(BF16)
    hqv_ref[0] = _dot(hb, whqv_ref[...]).astype(BF16)
    hog_ref[0] = _dot(hb, whog_ref[...]).astype(BF16)
    hff_ref[0] = _dot(hb, whff_ref[...])
    gp_ref[0] = _dot(hb, wgate_ref[...]).astype(BF16)


def _inproj_call(x, mod, w, cos_t, sin_t):
    b, L, _ = x.shape
    tl = TL_IN
    tok = lambda n: pl.BlockSpec((1, tl, n), lambda bi, i: (bi, i, 0))
    tr = lambda n: pl.BlockSpec((1, n, tl), lambda bi, i: (bi, 0, i))
    weights = [w["gmix"], None, None, w["wlat"], w["wkr"], w["wut"], w["whqv"], w["whog"], w["whff"], w["wgate"],
               w["gq"], w["gkv"], w["wqa"], w["wqb"], w["wk"], w["wvt"]]
    in_specs = [tok(D_MODEL), pl.BlockSpec((1, 8, D_MODEL), lambda bi, i: (bi, 0, 0))]
    args = [x, mod]
    for a in weights:
        if a is None:
            continue
        in_specs.append(_const_spec(a.shape))
        args.append(a)
    in_specs[3:3] = [pl.BlockSpec((tl, LANE), lambda bi, i: (i, 0))] * 2
    args[3:3] = [cos_t, sin_t]
    out_shape = [jax.ShapeDtypeStruct((b, L, 1024), BF16), jax.ShapeDtypeStruct((b, L, 1024), BF16),
                 jax.ShapeDtypeStruct((b, MIX, L), BF16), jax.ShapeDtypeStruct((b, MIX, L), BF16),
                 jax.ShapeDtypeStruct((b, L, 1024), BF16), jax.ShapeDtypeStruct((b, L, MIX), BF16),
                 jax.ShapeDtypeStruct((b, L, 1024), F32), jax.ShapeDtypeStruct((b, L, 3 * D_MODEL), BF16)]
    out_specs = [tok(1024), tok(1024), tr(MIX), tr(MIX), tok(1024), tok(MIX), tok(1024), tok(3 * D_MODEL)]
    return pl.pallas_call(
        _inproj_kernel, grid=(b, L // tl), in_specs=in_specs, out_specs=out_specs, out_shape=out_shape,
        compiler_params=_params("parallel", "parallel"), name="inproj",
    )(*args)


def _flash_kernel(q_ref, k_ref, vt_ref, o_ref, s_ref, *, tk, nk):
    tq = q_ref.shape[1]
    half = NOPE
    ones = jnp.ones((16, tk), BF16)

    def scores(buf, j):
        off = pl.multiple_of(j * tk, tk)
        smax = []
        for hh in range(2):
            s = _dot_nt(k_ref[0, pl.ds(off, tk), hh * LANE:(hh + 1) * LANE],
                        q_ref[0, :, hh * LANE:(hh + 1) * LANE])
            s_ref[buf, hh] = s
            smax.append(jnp.max(s, axis=0, keepdims=True))
        return smax

    def process(buf, j, state, smax):
        off = pl.multiple_of(j * tk, tk)
        new = []
        for hh in range(2):
            m, acc = state[hh]
            mn = jnp.maximum(m, smax[hh])
            p = jnp.exp2(s_ref[buf, hh] - mn).astype(BF16)
            a = jnp.exp2(m - mn)
            v = jnp.concatenate([vt_ref[0, hh * half:(hh + 1) * half, pl.ds(off, tk)], ones], axis=0)
            new.append((mn, a * acc + _dot(v, p)))
        return tuple(new)

    def body(i, carry):
        state, smax = carry
        for u in range(FLASH_UNROLL):
            j = i * FLASH_UNROLL + u
            nmax = scores(1 - u % 2, j + 1)
            state = process(u % 2, j, state, smax)
            smax = nmax
        return state, smax

    smax = scores(0, 0)
    state = tuple((jnp.full((1, tq), -1e30, F32), jnp.zeros((half + 16, tq), F32)) for _ in range(2))
    n_loop = (nk - 1) // FLASH_UNROLL
    state, smax = lax.fori_loop(0, n_loop, body, (state, smax))
    for j in range(n_loop * FLASH_UNROLL, nk):
        nmax = scores(1 - j % 2, j + 1) if j + 1 < nk else None
        state = process(j % 2, j, state, smax)
        smax = nmax
    for hh in range(2):
        acc = state[hh][1]
        o_ref[0, hh * half:(hh + 1) * half, :] = (acc[:half] / acc[half:half + 1]).astype(BF16)


def _flash_call(q, k, vt):
    b, L, _ = q.shape
    tq, tk = min(TQ, L), min(TK, L)
    return pl.pallas_call(
        functools.partial(_flash_kernel, tk=tk, nk=L // tk),
        scratch_shapes=[pltpu.VMEM((2, 2, tk, tq), F32)],
        grid=(b, N_HEADS // 2, L // tq),
        in_specs=[pl.BlockSpec((1, tq, 2 * LANE), lambda bi, p, i: (bi, i, p)),
                  pl.BlockSpec((1, L, 2 * LANE), lambda bi, p, i: (bi, 0, p)),
                  pl.BlockSpec((1, LANE, L), lambda bi, p, i: (bi, p, 0))],
        out_specs=pl.BlockSpec((1, LANE, tq), lambda bi, p, i: (bi, p, i)),
        out_shape=jax.ShapeDtypeStruct((b, MIX, L), BF16),
        compiler_params=_params("parallel", "parallel", "arbitrary"), name="flash",
    )(q, k, vt)


def _s5_kernel(u_ref, t_ref, f_ref, e_ref, a_ref, y_ref, sloc_ref, st_ref, *, nb, nct):
    m = nb * nct
    ucat = jnp.concatenate([u_ref[:, 0, hi].reshape(m, S5_CHUNK) for hi in range(S5_GROUP)], axis=1)
    y = _dot(ucat, t_ref[0])
    sloc_ref[...] = _dot(ucat, f_ref[0])
    a = a_ref[0]
    arf, aif, arb, aib = a[0:1], a[1:2], a[2:3], a[3:4]

    sb = 8 if nct % 8 == 0 else nct
    nblk = nct // sb

    def body(blk, carry):
        new = []
        for bi in range(nb):
            xrf, xif, xrb, xib = carry[bi]
            rf = pl.multiple_of(bi * nct + blk * sb, sb)
            rb = pl.multiple_of(bi * nct + (nblk - 1 - blk) * sb, sb)
            lf = sloc_ref[pl.ds(rf, sb), 0:2 * LANE]
            lb = sloc_ref[pl.ds(rb, sb), 2 * LANE:4 * LANE]
            frows, brows = [], []
            for r in range(sb):
                frows.append((xrf, xif))
                xrf, xif = (arf * xrf - aif * xif + lf[r:r + 1, 0:LANE],
                            arf * xif + aif * xrf + lf[r:r + 1, LANE:2 * LANE])
            for r in range(sb - 1, -1, -1):
                brows.append((xrb, xib))
                xrb, xib = (arb * xrb - aib * xib + lb[r:r + 1, 0:LANE],
                            arb * xib + aib * xrb + lb[r:r + 1, LANE:2 * LANE])
            brows = brows[::-1]
            st_ref[pl.ds(rf, sb), 0:LANE] = jnp.concatenate([t[0] for t in frows], axis=0)
            st_ref[pl.ds(rf, sb), LANE:2 * LANE] = jnp.concatenate([t[1] for t in frows], axis=0)
            st_ref[pl.ds(rb, sb), 2 * LANE:3 * LANE] = jnp.concatenate([t[0] for t in brows], axis=0)
            st_ref[pl.ds(rb, sb), 3 * LANE:4 * LANE] = jnp.concatenate([t[1] for t in brows], axis=0)
            new.append((xrf, xif, xrb, xib))
        return tuple(new)

    z = jnp.zeros((1, LANE), F32)
    lax.fori_loop(0, nblk, body, tuple((z, z, z, z) for _ in range(nb)))
    y = y + _dot(st_ref[...].astype(BF16), e_ref[0])
    for ho in range(S5_GROUP):
        y_ref[:, 0, ho] = y[:, ho * S5_CHUNK:(ho + 1) * S5_CHUNK].reshape(nb, nct, S5_CHUNK)


def _s5_call(ut, ops):
    b, _, L = ut.shape
    nct = L // S5_CHUNK
    u5 = ut.reshape(b, S5_GROUPS, S5_GROUP, nct, S5_CHUNK)
    gw = S5_GROUP * S5_CHUNK
    blk = pl.BlockSpec((b, 1, S5_GROUP, nct, S5_CHUNK), lambda g: (0, g, 0, 0, 0))
    y = pl.pallas_call(
        functools.partial(_s5_kernel, nb=b, nct=nct),
        grid=(S5_GROUPS,),
        in_specs=[blk,
                  pl.BlockSpec((1, gw, gw), lambda g: (g, 0, 0)),
                  pl.BlockSpec((1, gw, 4 * LANE), lambda g: (g, 0, 0)),
                  pl.BlockSpec((1, 4 * LANE, gw), lambda g: (g, 0, 0)),
                  pl.BlockSpec((1, 8, LANE), lambda g: (g, 0, 0))],
        out_specs=blk,
        out_shape=jax.ShapeDtypeStruct(u5.shape, F32),
        scratch_shapes=[pltpu.VMEM((b * nct, 4 * LANE), F32), pltpu.VMEM((b * nct, 4 * LANE), F32)],
        compiler_params=_params("parallel"), name="s5",
    )(u5, ops["t"], ops["f"], ops["e"], ops["a"])
    return y.reshape(b, MIX, L)


def _toeplitz_kernel(w_ref, t_ref):
    c = S5_CHUNK

    def body(hi, _):
        r0 = pl.multiple_of(hi * c, c)
        for ho in range(S5_GROUP):
            row = w_ref[0, hi, ho:ho + 1, :]
            skew = pltpu.roll(jnp.broadcast_to(row, (c, 2 * c)), 0, 1, stride=1, stride_axis=0)
            t_ref[0, pl.ds(r0, c), ho * c:(ho + 1) * c] = skew[:, :c].astype(BF16)
        return 0

    lax.fori_loop(0, S5_GROUP, body, 0)


def _toeplitz_call(w):
    g = w.shape[0]
    gw = S5_GROUP * S5_CHUNK
    return pl.pallas_call(
        _toeplitz_kernel, grid=(g,),
        in_specs=[pl.BlockSpec((1, S5_GROUP, S5_GROUP, 2 * S5_CHUNK), lambda i: (i, 0, 0, 0))],
        out_specs=pl.BlockSpec((1, gw, gw), lambda i: (i, 0, 0)),
        out_shape=jax.ShapeDtypeStruct((g, gw, gw), BF16),
        compiler_params=_params("parallel"), name="s5_toeplitz",
    )(w)


def _s5_operators(lam_re, lam_im, log_dt, b_re, b_im, c_re, c_im, d):
    hp = lax.Precision.HIGHEST
    G, P, GS, C = S5_GROUPS, S5_STATE, S5_GROUP, S5_CHUNK
    lam_re, lam_im = lam_re.astype(F32), lam_im.astype(F32)
    dt = jnp.exp(log_dt.astype(F32))[..., None]
    zr, zi = lam_re * dt, lam_im * dt
    kk = jnp.arange(C + 1, dtype=F32)[None, None, :, None]
    mag = jnp.exp(zr[:, :, None, :] * kk)
    ang = zi[:, :, None, :] * kk
    pw_re, pw_im = mag * jnp.cos(ang), mag * jnp.sin(ang)
    lb_re, lb_im = pw_re[:, :, 1], pw_im[:, :, 1]
    den = lam_re * lam_re + lam_im * lam_im
    nr, ni = lb_re - 1.0, lb_im
    cr = (nr * lam_re + ni * lam_im) / den
    ci = (ni * lam_re - nr * lam_im) / den
    bb_re = cr[..., None] * b_re - ci[..., None] * b_im
    bb_im = cr[..., None] * b_im + ci[..., None] * b_re
    c_re, c_im = c_re.astype(F32), c_im.astype(F32)

    def kern(dr):
        pr, pi_ = pw_re[dr, :, :C], pw_im[dr, :, :C]
        cp_re = c_re[dr][:, None] * pr[:, :, None, :] - c_im[dr][:, None] * pi_[:, :, None, :]
        cp_im = c_re[dr][:, None] * pi_[:, :, None, :] + c_im[dr][:, None] * pr[:, :, None, :]
        return (jnp.einsum('gkhp,gpi->gkhi', cp_re, bb_re[dr], precision=hp)
                - jnp.einsum('gkhp,gpi->gkhi', cp_im, bb_im[dr], precision=hp))

    kf, kb = kern(0), kern(1)
    k0 = kf[:, 0] + kb[:, 0] + jnp.eye(GS, dtype=F32)[None] * d.astype(F32).reshape(G, GS)[:, :, None]
    kfull = jnp.concatenate([kb[:, 1:][:, ::-1], k0[:, None], kf[:, 1:], jnp.zeros((G, 1, GS, GS), F32)], axis=1)
    t_op = _toeplitz_call(jnp.roll(kfull, -(C - 1), axis=1).transpose(0, 3, 2, 1))

    def f_part(dr, idx):
        pr, pi_ = pw_re[dr][:, idx], pw_im[dr][:, idx]
        br, bi = bb_re[dr].transpose(0, 2, 1), bb_im[dr].transpose(0, 2, 1)
        re = pr[:, None] * br[:, :, None] - pi_[:, None] * bi[:, :, None]
        im = pr[:, None] * bi[:, :, None] + pi_[:, None] * br[:, :, None]
        return re.reshape(G, GS * C, P), im.reshape(G, GS * C, P)

    tau = jnp.arange(C)
    ffr, ffi = f_part(0, C - 1 - tau)
    fbr, fbi = f_part(1, tau)
    padl = lambda a: jnp.pad(a, ((0, 0), (0, 0), (0, LANE - P)))
    f_op = jnp.concatenate([padl(ffr), padl(ffi), padl(fbr), padl(fbi)], axis=-1).astype(BF16)

    def e_part(dr, idx):
        pr, pi_ = pw_re[dr][:, idx], pw_im[dr][:, idx]
        cr_, ci_ = c_re[dr].transpose(0, 2, 1), c_im[dr].transpose(0, 2, 1)
        prt, pit = pr.transpose(0, 2, 1), pi_.transpose(0, 2, 1)
        re = cr_[:, :, :, None] * prt[:, :, None, :] - ci_[:, :, :, None] * pit[:, :, None, :]
        im = cr_[:, :, :, None] * pit[:, :, None, :] + ci_[:, :, :, None] * prt[:, :, None, :]
        return re.reshape(G, P, GS * C), -im.reshape(G, P, GS * C)

    efr, efi = e_part(0, tau + 1)
    ebr, ebi = e_part(1, C - tau)
    padr = lambda a: jnp.pad(a, ((0, 0), (0, LANE - P), (0, 0)))
    e_op = jnp.concatenate([padr(efr), padr(efi), padr(ebr), padr(ebi)], axis=1).astype(BF16)

    arows = [pw_re[0, :, C], pw_im[0, :, C], pw_re[1, :, C], pw_im[1, :, C]]
    a_op = jnp.stack([jnp.pad(r, ((0, 0), (0, LANE - P))) for r in arows]
                     + [jnp.zeros((G, LANE), F32)] * 4, axis=1)
    return dict(t=t_op, f=f_op, e=e_op, a=a_op)


def _split3(x):
    hi = x.astype(BF16)
    r1 = x - hi.astype(F32)
    mid = r1.astype(BF16)
    lo = (r1 - mid.astype(F32)).astype(BF16)
    return hi, mid, lo


HG_HALVES = (32, 16, 8, 4, 2, 1)
HG_SEL_HALVES = (8, 4, 2)


def _hg_later(idx, h, fwd):
    return (idx % (2 * h) >= h) if fwd else (idx % (2 * h) < h)


def _hg_anchor(idx, h, fwd):
    base = idx // (2 * h) * (2 * h)
    return base + h if fwd else base + h - 1


def _hg_consts(fwd):
    c = HG_CHUNK
    t = lax.broadcasted_iota(jnp.int32, (c, c), 0)
    s = lax.broadcasted_iota(jnp.int32, (c, c), 1)
    tri = jnp.where((s <= t) if fwd else (s >= t), 1.0, 0.0).astype(BF16)
    masks = {h: (t // (2 * h) == s // (2 * h)) & _hg_later(t, h, fwd) & ~_hg_later(s, h, fwd) for h in HG_HALVES}
    rows = lax.broadcasted_iota(jnp.int32, (c, HG_D), 0)
    later = {h: _hg_later(rows, h, fwd) for h in HG_HALVES}
    sr = lax.broadcasted_iota(jnp.int32, (len(HG_SEL_HALVES) * c, c), 0)
    sc = lax.broadcasted_iota(jnp.int32, (len(HG_SEL_HALVES) * c, c), 1)
    hit = jnp.zeros(sr.shape, jnp.bool_)
    for i, h in enumerate(HG_SEL_HALVES):
        hit = hit | ((sr // c == i) & (sc == _hg_anchor(sr % c, h, fwd)))
    sel = jnp.where(hit, 1.0, 0.0).astype(BF16)
    return dict(tri=tri, masks=masks, diag=(t == s), later=later, sel=sel)


def _hg_chunks(streams):
    c = HG_CHUNK

    gates = []
    for hq, hv, hf, lb, st, fwd, consts in streams:
        q = hq * _gate(hq)
        f = lb + (1.0 - lb) * _sigmoid(hf)
        logf = jnp.log(f)
        p1, p2, p3 = _split3(logf)
        tri = consts["tri"]
        cum = _dot(tri, p1) + _dot(tri, p2) + _dot(tri, p3)
        gates.append((q, 1.0 - f, f, logf, cum))

    refs = []
    for (hq, hv, hf, lb, st, fwd, consts), (q, k, f, logf, cum) in zip(streams, gates):
        cexc = cum - logf
        ref = {}
        for h in HG_HALVES:
            if 2 * h >= 16 and h not in HG_SEL_HALVES:
                ref[h] = jnp.concatenate(
                    [jnp.broadcast_to(cexc[a:a + 1, :], (2 * h, HG_D))
                     for a in (_hg_anchor(b0, h, fwd) for b0 in range(0, c, 2 * h))], axis=0)
        c_hi = cexc.astype(BF16)
        c_mid = (cexc - c_hi.astype(F32)).astype(BF16)
        gathered = _dot(consts["sel"], jnp.concatenate([c_hi, c_mid], axis=1))
        for i, h in enumerate(HG_SEL_HALVES):
            ref[h] = gathered[i * c:(i + 1) * c, :HG_D] + gathered[i * c:(i + 1) * c, HG_D:]
        refs.append(ref)

    scores = []
    for (hq, hv, hf, lb, st, fwd, consts), (q, k, f, logf, cum), ref in zip(streams, gates, refs):
        a = jnp.where(consts["diag"], _dot_nt(q.astype(BF16), k.astype(BF16)), 0.0)
        for h in HG_HALVES:
            late = consts["later"][h]
            if h == 1:
                x = jnp.where(late, f, 1.0)
            else:
                d = cum - ref[h]
                x = jnp.exp(jnp.where(late, d, -d))
            qs = jnp.where(late, q * x, 0.0).astype(BF16)
            ks = jnp.where(late, 0.0, k * x).astype(BF16)
            a = jnp.where(consts["masks"][h], _dot_nt(qs, ks), a)
        scores.append(a)

    outs = []
    for (hq, hv, hf, lb, st, fwd, consts), (q, k, f, logf, cum), a in zip(streams, gates, scores):
        vb = hv.astype(BF16)
        o = _dot(a.astype(BF16), vb) + _dot_nt((q * jnp.exp(cum)).astype(BF16), st.astype(BF16))
        last = cum[c - 1:c] if fwd else cum[0:1]
        kst = (k * jnp.exp(last - cum)).astype(BF16)
        outs.append((o, st * jnp.exp(last) + _dot_tn(vb, kst)))
    return outs


def _hg_kernel(qvf_ref, ff_ref, qvb_ref, fb_ref, lbf_ref, lbb_ref, of_ref, ob_ref, sf_ref, sb_ref, *, ncc):
    @pl.when(pl.program_id(1) == 0)
    def _():
        sf_ref[...] = jnp.zeros_like(sf_ref)
        sb_ref[...] = jnp.zeros_like(sb_ref)

    cf = _hg_consts(True)
    cb = _hg_consts(False)

    def body(cc, _):
        rf = pl.multiple_of(cc * HG_CHUNK, HG_CHUNK)
        rb = pl.multiple_of((ncc - 1 - cc) * HG_CHUNK, HG_CHUNK)
        streams = []
        for hd in range(HG_HEADS):
            sl = slice(hd * HG_D, (hd + 1) * HG_D)
            sv = slice(MIX + hd * HG_D, MIX + (hd + 1) * HG_D)
            streams.append((qvf_ref[0, pl.ds(rf, HG_CHUNK), sl].astype(F32),
                            qvf_ref[0, pl.ds(rf, HG_CHUNK), sv].astype(F32),
                            ff_ref[0, pl.ds(rf, HG_CHUNK), sl], lbf_ref[:, sl], sf_ref[hd], True, cf))
            streams.append((qvb_ref[0, pl.ds(rb, HG_CHUNK), sl].astype(F32),
                            qvb_ref[0, pl.ds(rb, HG_CHUNK), sv].astype(F32),
                            fb_ref[0, pl.ds(rb, HG_CHUNK), sl], lbb_ref[:, sl], sb_ref[hd], False, cb))
        outs = _hg_chunks(streams)
        for hd in range(HG_HEADS):
            sl = slice(hd * HG_D, (hd + 1) * HG_D)
            of_ref[0, pl.ds(rf, HG_CHUNK), sl], sf_ref[hd] = outs[2 * hd]
            ob_ref[0, pl.ds(rb, HG_CHUNK), sl], sb_ref[hd] = outs[2 * hd + 1]
        return 0

    lax.fori_loop(0, ncc, body, 0)


def _hg_call(hqv, hff, lbf, lbb):
    b, L, _ = hqv.shape
    tl = min(TL_HG, L)
    n = L // tl
    fwd = lambda w: pl.BlockSpec((1, tl, w), lambda bi, i: (bi, i, 0))
    bwd = lambda w, j: pl.BlockSpec((1, tl, w), lambda bi, i: (bi, n - 1 - i, j))
    return pl.pallas_call(
        functools.partial(_hg_kernel, ncc=tl // HG_CHUNK),
        grid=(b, n),
        in_specs=[fwd(2 * MIX), fwd(MIX), bwd(2 * MIX, 0), bwd(MIX, 1),
                  pl.BlockSpec((1, MIX), lambda bi, i: (0, 0)), pl.BlockSpec((1, MIX), lambda bi, i: (0, 0))],
        out_specs=[fwd(MIX), bwd(MIX, 0)],
        out_shape=[jax.ShapeDtypeStruct((b, L, MIX), F32)] * 2,
        scratch_shapes=[pltpu.VMEM((HG_HEADS, HG_D, HG_D), F32), pltpu.VMEM((HG_HEADS, HG_D, HG_D), F32)],
        compiler_params=_params("parallel", "arbitrary"), name="hgrn2",
    )(hqv, hff, hqv, hff, lbf, lbb)


def _merge_kernel(x_ref, mod_ref, zat_ref, yt_ref, of_ref, ob_ref, og_ref, gp_ref,
                  wglut_ref, bglu_ref, ghg_ref, wbr_ref, wout_ref, o_ref):
    g = _gelu_tanh(yt_ref[0])
    glu = _dot(wglut_ref[...], g.astype(BF16)) + bglu_ref[...]
    zbt = (g * _gate(glu)).astype(BF16)
    o = of_ref[0] + ob_ref[0]
    on = jnp.concatenate([_rms(o[:, hd * HG_D:(hd + 1) * HG_D]) for hd in range(HG_HEADS)], axis=1)
    og = og_ref[0].astype(F32)
    zc = (on * ghg_ref[...] * (og * _gate(og))).astype(BF16)
    bra = _dot_tn(zat_ref[0], wbr_ref[0])
    brb = _dot_tn(zbt, wbr_ref[1])
    brc = _dot(zc, wbr_ref[2])
    gp = gp_ref[0].astype(F32)
    mix = (_gate(gp[:, :D_MODEL]) * bra + _gate(gp[:, D_MODEL:2 * D_MODEL]) * brb
           + _gate(gp[:, 2 * D_MODEL:]) * brc)
    out = _dot(mix.astype(BF16), wout_ref[...])
    o_ref[0] = x_ref[0] + mod_ref[0][2:3] * out


def _merge_call(x, mod, zat, yt, of, ob, hog, gp, w):
    b, L, _ = x.shape
    tl = TL_MG
    tok = lambda n: pl.BlockSpec((1, tl, n), lambda bi, i: (bi, i, 0))
    tr = pl.BlockSpec((1, MIX, tl), lambda bi, i: (bi, 0, i))
    consts = [w["wglut"], w["bglu"], w["ghg"], w["wbr"], w["wout"]]
    return pl.pallas_call(
        _merge_kernel, grid=(b, L // tl),
        in_specs=[tok(D_MODEL), pl.BlockSpec((1, 8, D_MODEL), lambda bi, i: (bi, 0, 0)), tr, tr,
                  tok(MIX), tok(MIX), tok(MIX), tok(3 * D_MODEL)] + [_const_spec(a.shape) for a in consts],
        out_specs=tok(D_MODEL),
        out_shape=jax.ShapeDtypeStruct(x.shape, F32),
        compiler_params=_params("parallel", "parallel"), name="merge",
    )(x, mod, zat, yt, of, ob, hog, gp, *consts)


def _ff_chunks():
    out, off = [], 0
    while off < D_FF:
        w = min(FF_CW, D_FF - off)
        out.append((off, w))
        off += w
    return out


def _ffn_kernel(x_ref, xp_ref, xn_ref, mod_ref, gffn_ref, wup_ref, wconv_ref, bconv_ref, wdn_ref, o_ref,
                up_ref, h_ref, acc_ref, *, tl):
    i = pl.program_id(1)
    mod = mod_ref[0]

    def modulate(xv):
        return _rms(xv) * gffn_ref[...] * (1.0 + mod[4:5]) + mod[3:4]

    keep_prev = jnp.where(i > 0, 1.0, 0.0)
    keep_next = jnp.where(i < pl.num_programs(1) - 1, 1.0, 0.0)
    h_ref[...] = jnp.concatenate([modulate(xp_ref[0]) * keep_prev, modulate(x_ref[0]),
                                  modulate(xn_ref[0]) * keep_next], axis=0).astype(BF16)
    chunks = _ff_chunks()

    def up_proj(ci):
        off, cw = chunks[ci]
        for part in range(2):
            col = off + part * D_FF
            up_ref[ci % 2, part, :, 0:cw] = _dot(h_ref[...], wup_ref[:, col:col + cw])

    up_proj(0)
    for ci, (off, cw) in enumerate(chunks):
        if ci + 1 < len(chunks):
            up_proj(ci + 1)
        act = None
        for part in range(2):
            col = off + part * D_FF
            wc = wconv_ref[:, col:col + cw]
            buf = up_ref.at[ci % 2, part]
            y = (buf[pl.ds(HALO - 1, tl), 0:cw] * wc[0:1] + buf[pl.ds(HALO, tl), 0:cw] * wc[1:2]
                 + buf[pl.ds(HALO + 1, tl), 0:cw] * wc[2:3] + bconv_ref[:, col:col + cw])
            act = y * _gate(y) if part == 0 else act * y
        dn = _dot(act.astype(BF16), wdn_ref[off:off + cw, :])
        if ci == 0:
            acc_ref[...] = dn
        else:
            acc_ref[...] += dn
    o_ref[0] = x_ref[0] + mod[5:6] * acc_ref[...]


def _ffn_call(x, mod, w):
    b, L, _ = x.shape
    tl = TL_FF
    hb = tl // HALO
    nh = L // HALO
    consts = [w["gffn"], w["wup"], w["wconv"], w["bconv"], w["wdn"]]
    return pl.pallas_call(
        functools.partial(_ffn_kernel, tl=tl), grid=(b, L // tl),
        in_specs=[pl.BlockSpec((1, tl, D_MODEL), lambda bi, i: (bi, i, 0)),
                  pl.BlockSpec((1, HALO, D_MODEL), lambda bi, i: (bi, jnp.maximum(i * hb - 1, 0), 0)),
                  pl.BlockSpec((1, HALO, D_MODEL), lambda bi, i: (bi, jnp.minimum((i + 1) * hb, nh - 1), 0)),
                  pl.BlockSpec((1, 8, D_MODEL), lambda bi, i: (bi, 0, 0))] + [_const_spec(a.shape) for a in consts],
        out_specs=pl.BlockSpec((1, tl, D_MODEL), lambda bi, i: (bi, i, 0)),
        out_shape=jax.ShapeDtypeStruct(x.shape, F32),
        scratch_shapes=[pltpu.VMEM((2, 2, tl + 2 * HALO, FF_CW), F32), pltpu.VMEM((tl + 2 * HALO, D_MODEL), BF16),
                        pltpu.VMEM((tl, D_MODEL), F32)],
        compiler_params=_params("parallel", "parallel"), name="ffn",
    )(x, x, x, mod, *consts)


def _final_kernel(x_ref, g_ref, o_ref):
    o_ref[0] = _rms(x_ref[0]) * g_ref[...]


def _final_call(x, g):
    b, L, _ = x.shape
    tl = 512
    spec = pl.BlockSpec((1, tl, D_MODEL), lambda bi, i: (bi, i, 0))
    return pl.pallas_call(
        _final_kernel, grid=(b, L // tl), in_specs=[spec, pl.BlockSpec((1, D_MODEL), lambda bi, i: (0, 0))],
        out_specs=spec, out_shape=jax.ShapeDtypeStruct(x.shape, F32),
        compiler_params=_params("parallel", "parallel"), name="final_norm",
    )(x, g)


def _layer_weights(p, l):
    w_in = p["w_in"][l]
    col = lambda i: w_in[:, IN_OFFS[i]:IN_OFFS[i + 1]]
    half = ROPE // 2
    kr = col(2)
    z = lambda n: jnp.zeros((D_MODEL, n), F32)
    wkr = jnp.concatenate([z(NOPE), kr[:, :half], kr[:, half:], z(LANE - NOPE - ROPE),
                           z(NOPE), kr[:, half:], kr[:, :half], z(LANE - NOPE - ROPE)], axis=1)
    wq = p["w_q_up"][l].reshape(Q_LORA, N_HEADS, NOPE + ROPE)
    zq = lambda n: jnp.zeros((Q_LORA, N_HEADS, n), F32)
    x1, x2 = wq[:, :, NOPE:NOPE + half], wq[:, :, NOPE + half:]
    wqa = jnp.concatenate([wq[:, :, :NOPE], x1, x2, zq(LANE - NOPE - ROPE)], axis=2).reshape(Q_LORA, N_HEADS * LANE)
    wqb = jnp.concatenate([zq(NOPE), x2, x1, zq(LANE - NOPE - ROPE)], axis=2).reshape(Q_LORA, N_HEADS * LANE)
    wkv = p["w_kv_up"][l].reshape(KV_LORA, N_HEADS, 2 * NOPE)
    wk = jnp.concatenate([wkv[:, :, :NOPE], jnp.zeros((KV_LORA, N_HEADS, LANE - NOPE), F32)],
                         axis=2).reshape(KV_LORA, N_HEADS * LANE)
    wvt = wkv[:, :, NOPE:].reshape(KV_LORA, MIX).T
    bf = lambda a: a.astype(BF16)
    return dict(
        gmix=p["g_mix"][l][None], gq=p["g_q_lat"][l][None], gkv=p["g_kv_lat"][l][None],
        wlat=bf(jnp.concatenate([col(0), col(1)], axis=1)), wkr=bf(wkr), wut=bf(col(3).T),
        whqv=bf(jnp.concatenate([col(4), col(7)], axis=1)), whog=bf(col(8)),
        whff=bf(jnp.concatenate([col(5), col(6)], axis=1)), wgate=bf(col(9)),
        wqa=bf(wqa), wqb=bf(wqb), wk=bf(wk), wvt=bf(wvt),
        wglut=bf(p["w_glu"][l].T), bglu=p["b_glu"][l][:, None], ghg=p["g_hg_out"][l][None],
        wbr=bf(p["w_branch"][l]), wout=bf(p["w_out"][l]),
        gffn=p["g_ffn"][l][None], wup=bf(p["w_ffn_up"][l]), wconv=p["w_ffn_conv"][l],
        bconv=p["b_ffn_conv"][l][None], wdn=bf(p["w_ffn_down"][l]),
    )


def _rope_tables(L):
    half = ROPE // 2
    inv_freq = 1.0 / (ROPE_BASE ** (jnp.arange(0, ROPE, 2, dtype=F32) / ROPE))
    ang = jnp.arange(L, dtype=F32)[:, None] * inv_freq[None, :]
    cos, sin = jnp.cos(ang), jnp.sin(ang)
    one, zero = jnp.ones((L, NOPE), F32), jnp.zeros((L, NOPE), F32)
    pad = jnp.zeros((L, LANE - NOPE - ROPE), F32)
    return (jnp.concatenate([one, cos, cos, pad], axis=1), jnp.concatenate([zero, -sin, sin, pad], axis=1))


def _trunk(x, mods, weights, s5ops, lb, g_final):
    b, L, _ = x.shape
    cos_t, sin_t = _rope_tables(L)
    for l in range(DEPTH):
        w = weights[l]
        q, k, vt, ut, hqv, hog, hff, gp = _inproj_call(x, mods[l], w, cos_t, sin_t)
        zat = _flash_call(q, k, vt)
        yt = _s5_call(ut, s5ops[l])
        of, ob = _hg_call(hqv, hff, lb[0, l][None], lb[1, l][None])
        x = _merge_call(x, mods[l], zat, yt, of, ob, hog, gp, w)
        x = _ffn_call(x, mods[l], w)
    return _final_call(x, g_final[None])


def kernel(x_prompt, x_sample, c_prompt, c_sample, w_ada, b_ada, g_mix, w_in, g_q_lat, w_q_up, g_kv_lat, w_kv_up,
           s5_lam_re, s5_lam_im, s5_log_dt, s5_b_re, s5_b_im, s5_c_re, s5_c_im, s5_d, w_glu, b_glu, hg_lb_logits,
           g_hg_out, w_branch, w_out, g_ffn, w_ffn_up, w_ffn_conv, b_ffn_conv, w_ffn_down, g_final):
    p = dict(g_mix=g_mix, w_in=w_in, g_q_lat=g_q_lat, w_q_up=w_q_up, g_kv_lat=g_kv_lat, w_kv_up=w_kv_up,
             w_glu=w_glu, b_glu=b_glu, g_hg_out=g_hg_out, w_branch=w_branch, w_out=w_out, g_ffn=g_ffn,
             w_ffn_up=w_ffn_up, w_ffn_conv=w_ffn_conv, b_ffn_conv=b_ffn_conv, w_ffn_down=w_ffn_down)
    depth = w_in.shape[0]
    assert depth == DEPTH
    bp, bs = c_prompt.shape[0], c_sample.shape[0]
    rows = -(-(bp + bs) // 8) * 8
    c_all = jnp.concatenate([c_prompt, c_sample, jnp.zeros((rows - bp - bs, D_MODEL), F32)], axis=0)
    mod_all = _ada_call(c_all, w_ada, b_ada)

    def mods_for(lo, n):
        m = mod_all[:, lo:lo + n].reshape(DEPTH, n, 6, D_MODEL)
        return jnp.pad(m, ((0, 0), (0, 0), (0, 2), (0, 0)))

    weights = [_layer_weights(p, l) for l in range(DEPTH)]
    s5ops = [_s5_operators(s5_lam_re[l], s5_lam_im[l], s5_log_dt[l], s5_b_re[l], s5_b_im[l],
                           s5_c_re[l], s5_c_im[l], s5_d[l]) for l in range(DEPTH)]
    gam = jax.nn.softmax(hg_lb_logits.astype(F32), axis=1)
    lb = jnp.cumsum(gam, axis=1) - gam[:, :1]
    y_prompt = _trunk(x_prompt, mods_for(0, bp), weights, s5ops, lb, g_final)
    y_sample = _trunk(x_sample, mods_for(bp, bs), weights, s5ops, lb, g_final)
    return (y_prompt, y_sample)
```

```python
import functools
import math

import jax
import jax.numpy as jnp
from jax import lax
from jax.experimental import pallas as pl
from jax.experimental.pallas import tpu as pltpu

F32 = jnp.float32
BF16 = jnp.bfloat16

D_MODEL = 1024
DEPTH = 4
MIX = 512
N_HEADS = 8
NOPE = 64
ROPE = 32
Q_LORA = 384
KV_LORA = 256
ROPE_BASE = 10000.0
S5_GROUPS = 32
S5_GROUP = 16
S5_STATE = 64
S5_CHUNK = 128
HG_HEADS = 4
HG_D = 128
HG_CHUNK = 64
HG_UNROLL = 2
D_FF = 2816
EPS = 1e-6
ATTN_SCALE = 1.0 / math.sqrt(NOPE + ROPE)
LOG2E = 1.4426950408889634
IN_OFFS = (0, 384, 640, 672, 1184, 1696, 2208, 2720, 3232, 3744, 6816)

LANE = 128
VMEM_LIMIT = 56 * 1024 * 1024

TL_IN = 512
TQ = 512
TK = 512
FLASH_UNROLL = 8
TL_HG = 256
TL_MG = 512
TL_FF = 512
FF_CW = 768
HALO = 8


def _dot(a, b):
    return jnp.dot(a, b, preferred_element_type=F32)


def _dot_nt(a, b):
    return lax.dot_general(a, b, (((1,), (1,)), ((), ())), preferred_element_type=F32)


def _dot_tn(a, b):
    return lax.dot_general(a, b, (((0,), (0,)), ((), ())), preferred_element_type=F32)


def _sigmoid(x):
    return 1.0 / (1.0 + jnp.exp(-x))


def _gate(x):
    return 0.5 * jnp.tanh(0.5 * x) + 0.5


def _rms(x):
    return x * lax.rsqrt(jnp.mean(x * x, axis=-1, keepdims=True) + EPS)


def _gelu_tanh(x):
    return 0.5 * x * (1.0 + jnp.tanh(math.sqrt(2.0 / math.pi) * (x + 0.044715 * (x * x * x))))


def _params(*sem):
    return pltpu.CompilerParams(dimension_semantics=sem, vmem_limit_bytes=VMEM_LIMIT)


def _const_spec(shape):
    nd = len(shape)
    return pl.BlockSpec(shape, lambda *_: (0,) * nd, pipeline_mode=pl.Buffered(1))


def _ada_kernel(c_ref, w_ref, b_ref, o_ref):
    c = c_ref[...]
    a = (c * _sigmoid(c)).astype(BF16)
    o_ref[0] = _dot(a, w_ref[0].astype(BF16)) + b_ref[0]


def _ada_call(c_all, w_ada, b_ada):
    nb = 1536
    rows = c_all.shape[0]
    return pl.pallas_call(
        _ada_kernel,
        grid=(DEPTH, 6 * D_MODEL // nb),
        in_specs=[pl.BlockSpec((rows, D_MODEL), lambda l, j: (0, 0)),
                  pl.BlockSpec((1, D_MODEL, nb), lambda l, j: (l, 0, j)),
                  pl.BlockSpec((1, 1, nb), lambda l, j: (l, 0, j))],
        out_specs=pl.BlockSpec((1, rows, nb), lambda l, j: (l, 0, j)),
        out_shape=jax.ShapeDtypeStruct((DEPTH, rows, 6 * D_MODEL), F32),
        compiler_params=_params("arbitrary", "arbitrary"),
        name="ada_mod",
    )(c_all, w_ada, b_ada.reshape(DEPTH, 1, 6 * D_MODEL))


def _inproj_kernel(x_ref, mod_ref, gmix_ref, cos_ref, sin_ref,
                   wlat_ref, wkr_ref, wut_ref, whqv_ref, whog_ref, whff_ref, wgate_ref,
                   gq_ref, gkv_ref, wqa_ref, wqb_ref, wk_ref, wvt_ref,
                   q_ref, k_ref, vt_ref, ut_ref, hqv_ref, hog_ref, hff_ref, gp_ref):
    mod = mod_ref[0]
    h = _rms(x_ref[0]) * gmix_ref[...] * (1.0 + mod[1:2]) + mod[0:1]
    hb = h.astype(BF16)
    lat = _dot(hb, wlat_ref[...])
    qn = (_rms(lat[:, :Q_LORA]) * gq_ref[...]).astype(BF16)
    kvn = (_rms(lat[:, Q_LORA:]) * gkv_ref[...]).astype(BF16)
    cos = cos_ref[...]
    sin = sin_ref[...]
    qa = _dot(qn, wqa_ref[...])
    qb = _dot(qn, wqb_ref[...])
    krp = _dot(hb, wkr_ref[...])
    kr = krp[:, :LANE] * cos + krp[:, LANE:] * sin
    kn = _dot(kvn, wk_ref[...])
    for hd in range(N_HEADS):
        sl = slice(hd * LANE, (hd + 1) * LANE)
        q_ref[0, :, sl] = ((qa[:, sl] * cos + qb[:, sl] * sin) * (ATTN_SCALE * LOG2E)).astype(BF16)
        k_ref[0, :, sl] = (kn[:, sl] + kr).astype(BF16)
    vt_ref[0] = _dot_nt(wvt_ref[...], kvn).astype(BF16)
    ut_ref[0] = _dot_nt(wut_ref[...], hb).astype(BF16)
    hqv_ref[0] = _dot(hb, whqv_ref[...]).astype(BF16)
    hog_ref[0] = _dot(hb, whog_ref[...]).astype(BF16)
    hff_ref[0] = _dot(hb, whff_ref[...])
    gp_ref[0] = _dot(hb, wgate_ref[...]).astype(BF16)


def _inproj_call(x, mod, w, cos_t, sin_t):
    b, L, _ = x.shape
    tl = TL_IN
    tok = lambda n: pl.BlockSpec((1, tl, n), lambda bi, i: (bi, i, 0))
    tr = lambda n: pl.BlockSpec((1, n, tl), lambda bi, i: (bi, 0, i))
    weights = [w["gmix"], None, None, w["wlat"], w["wkr"], w["wut"], w["whqv"], w["whog"], w["whff"], w["wgate"],
               w["gq"], w["gkv"], w["wqa"], w["wqb"], w["wk"], w["wvt"]]
    in_specs = [tok(D_MODEL), pl.BlockSpec((1, 8, D_MODEL), lambda bi, i: (bi, 0, 0))]
    args = [x, mod]
    for a in weights:
        if a is None:
            continue
        in_specs.append(_const_spec(a.shape))
        args.append(a)
    in_specs[3:3] = [pl.BlockSpec((tl, LANE), lambda bi, i: (i, 0))] * 2
    args[3:3] = [cos_t, sin_t]
    out_shape = [jax.ShapeDtypeStruct((b, L, 1024), BF16), jax.ShapeDtypeStruct((b, L, 1024), BF16),
                 jax.ShapeDtypeStruct((b, MIX, L), BF16), jax.ShapeDtypeStruct((b, MIX, L), BF16),
                 jax.ShapeDtypeStruct((b, L, 1024), BF16), jax.ShapeDtypeStruct((b, L, MIX), BF16),
                 jax.ShapeDtypeStruct((b, L, 1024), F32), jax.ShapeDtypeStruct((b, L, 3 * D_MODEL), BF16)]
    out_specs = [tok(1024), tok(1024), tr(MIX), tr(MIX), tok(1024), tok(MIX), tok(1024), tok(3 * D_MODEL)]
    return pl.pallas_call(
        _inproj_kernel, grid=(b, L // tl), in_specs=in_specs, out_specs=out_specs, out_shape=out_shape,
        compiler_params=_params("parallel", "parallel"), name="inproj",
    )(*args)


def _flash_kernel(q_ref, k_ref, vt_ref, o_ref, s_ref, *, tk, nk):
    tq = q_ref.shape[1]
    half = NOPE
    ones = jnp.ones((16, tk), BF16)

    def scores(buf, j):
        off = pl.multiple_of(j * tk, tk)
        for hh in range(2):
            s_ref[buf, hh] = _dot_nt(k_ref[0, pl.ds(off, tk), hh * LANE:(hh + 1) * LANE],
                                     q_ref[0, :, hh * LANE:(hh + 1) * LANE])

    def process(buf, j, carry):
        off = pl.multiple_of(j * tk, tk)
        new = []
        for hh in range(2):
            m, acc = carry[hh]
            s = s_ref[buf, hh]
            mn = jnp.maximum(m, jnp.max(s, axis=0, keepdims=True))
            p = jnp.exp2(s - mn).astype(BF16)
            a = jnp.exp2(m - mn)
            v = jnp.concatenate([vt_ref[0, hh * half:(hh + 1) * half, pl.ds(off, tk)], ones], axis=0)
            new.append((mn, a * acc + _dot(v, p)))
        return tuple(new)

    def body(i, carry):
        for u in range(FLASH_UNROLL):
            j = i * FLASH_UNROLL + u
            scores(1 - u % 2, j + 1)
            carry = process(u % 2, j, carry)
        return carry

    scores(0, 0)
    init = tuple((jnp.full((1, tq), -1e30, F32), jnp.zeros((half + 16, tq), F32)) for _ in range(2))
    n_loop = (nk - 1) // FLASH_UNROLL
    res = lax.fori_loop(0, n_loop, body, init)
    for j in range(n_loop * FLASH_UNROLL, nk):
        if j + 1 < nk:
            scores(1 - j % 2, j + 1)
        res = process(j % 2, j, res)
    for hh in range(2):
        acc = res[hh][1]
        o_ref[0, hh * half:(hh + 1) * half, :] = (acc[:half] / acc[half:half + 1]).astype(BF16)


def _flash_call(q, k, vt):
    b, L, _ = q.shape
    tq, tk = min(TQ, L), min(TK, L)
    return pl.pallas_call(
        functools.partial(_flash_kernel, tk=tk, nk=L // tk),
        scratch_shapes=[pltpu.VMEM((2, 2, tk, tq), F32)],
        grid=(b, N_HEADS // 2, L // tq),
        in_specs=[pl.BlockSpec((1, tq, 2 * LANE), lambda bi, p, i: (bi, i, p)),
                  pl.BlockSpec((1, L, 2 * LANE), lambda bi, p, i: (bi, 0, p)),
                  pl.BlockSpec((1, LANE, L), lambda bi, p, i: (bi, p, 0))],
        out_specs=pl.BlockSpec((1, LANE, tq), lambda bi, p, i: (bi, p, i)),
        out_shape=jax.ShapeDtypeStruct((b, MIX, L), BF16),
        compiler_params=_params("parallel", "parallel", "arbitrary"), name="flash",
    )(q, k, vt)


def _s5_kernel(u_ref, t_ref, f_ref, e_ref, a_ref, y_ref, sloc_ref, st_ref, *, nb, nct):
    m = nb * nct
    ucat = jnp.concatenate([u_ref[:, 0, hi].reshape(m, S5_CHUNK) for hi in range(S5_GROUP)], axis=1)
    y = _dot(ucat, t_ref[0])
    sloc_ref[...] = _dot(ucat, f_ref[0])
    a = a_ref[0]
    arf, aif, arb, aib = a[0:1], a[1:2], a[2:3], a[3:4]

    sb = 8 if nct % 8 == 0 else nct
    nblk = nct // sb

    def body(blk, carry):
        new = []
        for bi in range(nb):
            xrf, xif, xrb, xib = carry[bi]
            rf = pl.multiple_of(bi * nct + blk * sb, sb)
            rb = pl.multiple_of(bi * nct + (nblk - 1 - blk) * sb, sb)
            lf = sloc_ref[pl.ds(rf, sb), 0:2 * LANE]
            lb = sloc_ref[pl.ds(rb, sb), 2 * LANE:4 * LANE]
            frows, brows = [], []
            for r in range(sb):
                frows.append((xrf, xif))
                xrf, xif = (arf * xrf - aif * xif + lf[r:r + 1, 0:LANE],
                            arf * xif + aif * xrf + lf[r:r + 1, LANE:2 * LANE])
            for r in range(sb - 1, -1, -1):
                brows.append((xrb, xib))
                xrb, xib = (arb * xrb - aib * xib + lb[r:r + 1, 0:LANE],
                            arb * xib + aib * xrb + lb[r:r + 1, LANE:2 * LANE])
            brows = brows[::-1]
            st_ref[pl.ds(rf, sb), 0:LANE] = jnp.concatenate([t[0] for t in frows], axis=0)
            st_ref[pl.ds(rf, sb), LANE:2 * LANE] = jnp.concatenate([t[1] for t in frows], axis=0)
            st_ref[pl.ds(rb, sb), 2 * LANE:3 * LANE] = jnp.concatenate([t[0] for t in brows], axis=0)
            st_ref[pl.ds(rb, sb), 3 * LANE:4 * LANE] = jnp.concatenate([t[1] for t in brows], axis=0)
            new.append((xrf, xif, xrb, xib))
        return tuple(new)

    z = jnp.zeros((1, LANE), F32)
    lax.fori_loop(0, nblk, body, tuple((z, z, z, z) for _ in range(nb)))
    y = y + _dot(st_ref[...].astype(BF16), e_ref[0])
    for ho in range(S5_GROUP):
        y_ref[:, 0, ho] = y[:, ho * S5_CHUNK:(ho + 1) * S5_CHUNK].reshape(nb, nct, S5_CHUNK)


def _s5_call(ut, ops):
    b, _, L = ut.shape
    nct = L // S5_CHUNK
    u5 = ut.reshape(b, S5_GROUPS, S5_GROUP, nct, S5_CHUNK)
    gw = S5_GROUP * S5_CHUNK
    blk = pl.BlockSpec((b, 1, S5_GROUP, nct, S5_CHUNK), lambda g: (0, g, 0, 0, 0))
    y = pl.pallas_call(
        functools.partial(_s5_kernel, nb=b, nct=nct),
        grid=(S5_GROUPS,),
        in_specs=[blk,
                  pl.BlockSpec((1, gw, gw), lambda g: (g, 0, 0)),
                  pl.BlockSpec((1, gw, 4 * LANE), lambda g: (g, 0, 0)),
                  pl.BlockSpec((1, 4 * LANE, gw), lambda g: (g, 0, 0)),
                  pl.BlockSpec((1, 8, LANE), lambda g: (g, 0, 0))],
        out_specs=blk,
        out_shape=jax.ShapeDtypeStruct(u5.shape, F32),
        scratch_shapes=[pltpu.VMEM((b * nct, 4 * LANE), F32), pltpu.VMEM((b * nct, 4 * LANE), F32)],
        compiler_params=_params("parallel"), name="s5",
    )(u5, ops["t"], ops["f"], ops["e"], ops["a"])
    return y.reshape(b, MIX, L)


def _toeplitz_kernel(w_ref, t_ref):
    c = S5_CHUNK

    def body(hi, _):
        r0 = pl.multiple_of(hi * c, c)
        for ho in range(S5_GROUP):
            row = w_ref[0, hi, ho:ho + 1, :]
            skew = pltpu.roll(jnp.broadcast_to(row, (c, 2 * c)), 0, 1, stride=1, stride_axis=0)
            t_ref[0, pl.ds(r0, c), ho * c:(ho + 1) * c] = skew[:, :c].astype(BF16)
        return 0

    lax.fori_loop(0, S5_GROUP, body, 0)


def _toeplitz_call(w):
    g = w.shape[0]
    gw = S5_GROUP * S5_CHUNK
    return pl.pallas_call(
        _toeplitz_kernel, grid=(g,),
        in_specs=[pl.BlockSpec((1, S5_GROUP, S5_GROUP, 2 * S5_CHUNK), lambda i: (i, 0, 0, 0))],
        out_specs=pl.BlockSpec((1, gw, gw), lambda i: (i, 0, 0)),
        out_shape=jax.ShapeDtypeStruct((g, gw, gw), BF16),
        compiler_params=_params("parallel"), name="s5_toeplitz",
    )(w)


def _s5_operators(lam_re, lam_im, log_dt, b_re, b_im, c_re, c_im, d):
    hp = lax.Precision.HIGHEST
    G, P, GS, C = S5_GROUPS, S5_STATE, S5_GROUP, S5_CHUNK
    lam_re, lam_im = lam_re.astype(F32), lam_im.astype(F32)
    dt = jnp.exp(log_dt.astype(F32))[..., None]
    zr, zi = lam_re * dt, lam_im * dt
    kk = jnp.arange(C + 1, dtype=F32)[None, None, :, None]
    mag = jnp.exp(zr[:, :, None, :] * kk)
    ang = zi[:, :, None, :] * kk
    pw_re, pw_im = mag * jnp.cos(ang), mag * jnp.sin(ang)
    lb_re, lb_im = pw_re[:, :, 1], pw_im[:, :, 1]
    den = lam_re * lam_re + lam_im * lam_im
    nr, ni = lb_re - 1.0, lb_im
    cr = (nr * lam_re + ni * lam_im) / den
    ci = (ni * lam_re - nr * lam_im) / den
    bb_re = cr[..., None] * b_re - ci[..., None] * b_im
    bb_im = cr[..., None] * b_im + ci[..., None] * b_re
    c_re, c_im = c_re.astype(F32), c_im.astype(F32)

    def kern(dr):
        pr, pi_ = pw_re[dr, :, :C], pw_im[dr, :, :C]
        cp_re = c_re[dr][:, None] * pr[:, :, None, :] - c_im[dr][:, None] * pi_[:, :, None, :]
        cp_im = c_re[dr][:, None] * pi_[:, :, None, :] + c_im[dr][:, None] * pr[:, :, None, :]
        return (jnp.einsum('gkhp,gpi->gkhi', cp_re, bb_re[dr], precision=hp)
                - jnp.einsum('gkhp,gpi->gkhi', cp_im, bb_im[dr], precision=hp))

    kf, kb = kern(0), kern(1)
    k0 = kf[:, 0] + kb[:, 0] + jnp.eye(GS, dtype=F32)[None] * d.astype(F32).reshape(G, GS)[:, :, None]
    kfull = jnp.concatenate([kb[:, 1:][:, ::-1], k0[:, None], kf[:, 1:], jnp.zeros((G, 1, GS, GS), F32)], axis=1)
    t_op = _toeplitz_call(jnp.roll(kfull, -(C - 1), axis=1).transpose(0, 3, 2, 1))

    def f_part(dr, idx):
        pr, pi_ = pw_re[dr][:, idx], pw_im[dr][:, idx]
        br, bi = bb_re[dr].transpose(0, 2, 1), bb_im[dr].transpose(0, 2, 1)
        re = pr[:, None] * br[:, :, None] - pi_[:, None] * bi[:, :, None]
        im = pr[:, None] * bi[:, :, None] + pi_[:, None] * br[:, :, None]
        return re.reshape(G, GS * C, P), im.reshape(G, GS * C, P)

    tau = jnp.arange(C)
    ffr, ffi = f_part(0, C - 1 - tau)
    fbr, fbi = f_part(1, tau)
    padl = lambda a: jnp.pad(a, ((0, 0), (0, 0), (0, LANE - P)))
    f_op = jnp.concatenate([padl(ffr), padl(ffi), padl(fbr), padl(fbi)], axis=-1).astype(BF16)

    def e_part(dr, idx):
        pr, pi_ = pw_re[dr][:, idx], pw_im[dr][:, idx]
        cr_, ci_ = c_re[dr].transpose(0, 2, 1), c_im[dr].transpose(0, 2, 1)
        prt, pit = pr.transpose(0, 2, 1), pi_.transpose(0, 2, 1)
        re = cr_[:, :, :, None] * prt[:, :, None, :] - ci_[:, :, :, None] * pit[:, :, None, :]
        im = cr_[:, :, :, None] * pit[:, :, None, :] + ci_[:, :, :, None] * prt[:, :, None, :]
        return re.reshape(G, P, GS * C), -im.reshape(G, P, GS * C)

    efr, efi = e_part(0, tau + 1)
    ebr, ebi = e_part(1, C - tau)
    padr = lambda a: jnp.pad(a, ((0, 0), (0, LANE - P), (0, 0)))
    e_op = jnp.concatenate([padr(efr), padr(efi), padr(ebr), padr(ebi)], axis=1).astype(BF16)

    arows = [pw_re[0, :, C], pw_im[0, :, C], pw_re[1, :, C], pw_im[1, :, C]]
    a_op = jnp.stack([jnp.pad(r, ((0, 0), (0, LANE - P))) for r in arows]
                     + [jnp.zeros((G, LANE), F32)] * 4, axis=1)
    return dict(t=t_op, f=f_op, e=e_op, a=a_op)


def _split3(x):
    hi = x.astype(BF16)
    r1 = x - hi.astype(F32)
    mid = r1.astype(BF16)
    lo = (r1 - mid.astype(F32)).astype(BF16)
    return hi, mid, lo


HG_HALVES = (32, 16, 8, 4, 2, 1)
HG_SEL_HALVES = (8, 4, 2)


def _hg_later(idx, h, fwd):
    return (idx % (2 * h) >= h) if fwd else (idx % (2 * h) < h)


def _hg_anchor(idx, h, fwd):
    base = idx // (2 * h) * (2 * h)
    return base + h if fwd else base + h - 1


def _hg_consts(fwd):
    c = HG_CHUNK
    t = lax.broadcasted_iota(jnp.int32, (c, c), 0)
    s = lax.broadcasted_iota(jnp.int32, (c, c), 1)
    tri = jnp.where((s <= t) if fwd else (s >= t), 1.0, 0.0).astype(BF16)
    masks = {h: (t // (2 * h) == s // (2 * h)) & _hg_later(t, h, fwd) & ~_hg_later(s, h, fwd) for h in HG_HALVES}
    rows = lax.broadcasted_iota(jnp.int32, (c, HG_D), 0)
    later = {h: _hg_later(rows, h, fwd) for h in HG_HALVES}
    sign = {h: jnp.where(later[h], 1.0, -1.0) for h in HG_HALVES}
    sr = lax.broadcasted_iota(jnp.int32, (len(HG_SEL_HALVES) * c, c), 0)
    sc = lax.broadcasted_iota(jnp.int32, (len(HG_SEL_HALVES) * c, c), 1)
    hit = jnp.zeros(sr.shape, jnp.bool_)
    for i, h in enumerate(HG_SEL_HALVES):
        hit = hit | ((sr // c == i) & (sc == _hg_anchor(sr % c, h, fwd)))
    sel = jnp.where(hit, 1.0, 0.0).astype(BF16)
    return dict(tri=tri, masks=masks, diag=(t == s), later=later, sign=sign, sel=sel)


def _hg_chunks(streams):
    c = HG_CHUNK

    gates = []
    for hq, hv, hf, lb, st, fwd, consts in streams:
        q = hq * _gate(hq)
        f = lb + (1.0 - lb) * _sigmoid(hf)
        logf = jnp.log(f)
        p1, p2, p3 = _split3(logf)
        tri = consts["tri"]
        cum = _dot(tri, p1) + _dot(tri, p2) + _dot(tri, p3)
        gates.append((q, 1.0 - f, f, logf, cum))

    refs = []
    for (hq, hv, hf, lb, st, fwd, consts), (q, k, f, logf, cum) in zip(streams, gates):
        cexc = cum - logf
        ref = {}
        for h in HG_HALVES:
            if 2 * h >= 16 and h not in HG_SEL_HALVES:
                ref[h] = jnp.concatenate(
                    [jnp.broadcast_to(cexc[a:a + 1, :], (2 * h, HG_D))
                     for a in (_hg_anchor(b0, h, fwd) for b0 in range(0, c, 2 * h))], axis=0)
        c_hi = cexc.astype(BF16)
        c_mid = (cexc - c_hi.astype(F32)).astype(BF16)
        gathered = _dot(consts["sel"], jnp.concatenate([c_hi, c_mid], axis=1))
        for i, h in enumerate(HG_SEL_HALVES):
            ref[h] = gathered[i * c:(i + 1) * c, :HG_D] + gathered[i * c:(i + 1) * c, HG_D:]
        refs.append(ref)

    scores = []
    for (hq, hv, hf, lb, st, fwd, consts), (q, k, f, logf, cum), ref in zip(streams, gates, refs):
        a = jnp.where(consts["diag"], _dot_nt(q.astype(BF16), k.astype(BF16)), 0.0)
        qf = q * f
        for h in HG_HALVES:
            late = consts["later"][h]
            if h == 1:
                y = jnp.where(late, qf, k)
            else:
                y = jnp.where(late, q, k) * jnp.exp((cum - ref[h]) * consts["sign"][h])
            y = y.astype(BF16)
            a = jnp.where(consts["masks"][h], _dot_nt(y, y), a)
        scores.append(a)

    outs = []
    for (hq, hv, hf, lb, st, fwd, consts), (q, k, f, logf, cum), a in zip(streams, gates, scores):
        if isinstance(st, int):
            st = outs[st][1]
        vb = hv.astype(BF16)
        o = _dot(a.astype(BF16), vb) + _dot_nt((q * jnp.exp(cum)).astype(BF16), st.astype(BF16))
        last = cum[c - 1:c] if fwd else cum[0:1]
        kst = (k * jnp.exp(last - cum)).astype(BF16)
        outs.append((o, st * jnp.exp(last) + _dot_tn(vb, kst)))
    return outs


def _hg_kernel(qvf_ref, ff_ref, qvb_ref, fb_ref, lbf_ref, lbb_ref, of_ref, ob_ref, sf_ref, sb_ref, *, ncc):
    @pl.when(pl.program_id(1) == 0)
    def _():
        sf_ref[...] = jnp.zeros_like(sf_ref)
        sb_ref[...] = jnp.zeros_like(sb_ref)

    cf = _hg_consts(True)
    cb = _hg_consts(False)
    unroll = HG_UNROLL if ncc % HG_UNROLL == 0 else 1
    per = 2 * HG_HEADS

    def body(it, _):
        streams, rows = [], []
        for u in range(unroll):
            cc = it * unroll + u
            rf = pl.multiple_of(cc * HG_CHUNK, HG_CHUNK)
            rb = pl.multiple_of((ncc - 1 - cc) * HG_CHUNK, HG_CHUNK)
            rows.append((rf, rb))
            for hd in range(HG_HEADS):
                sl = slice(hd * HG_D, (hd + 1) * HG_D)
                sv = slice(MIX + hd * HG_D, MIX + (hd + 1) * HG_D)
                prev = len(streams) - per
                streams.append((qvf_ref[0, pl.ds(rf, HG_CHUNK), sl].astype(F32),
                                qvf_ref[0, pl.ds(rf, HG_CHUNK), sv].astype(F32),
                                ff_ref[0, pl.ds(rf, HG_CHUNK), sl], lbf_ref[:, sl],
                                sf_ref[hd] if u == 0 else prev, True, cf))
                prev = len(streams) - per
                streams.append((qvb_ref[0, pl.ds(rb, HG_CHUNK), sl].astype(F32),
                                qvb_ref[0, pl.ds(rb, HG_CHUNK), sv].astype(F32),
                                fb_ref[0, pl.ds(rb, HG_CHUNK), sl], lbb_ref[:, sl],
                                sb_ref[hd] if u == 0 else prev, False, cb))
        outs = _hg_chunks(streams)
        for u, (rf, rb) in enumerate(rows):
            for hd in range(HG_HEADS):
                sl = slice(hd * HG_D, (hd + 1) * HG_D)
                of_ref[0, pl.ds(rf, HG_CHUNK), sl] = outs[u * per + 2 * hd][0]
                ob_ref[0, pl.ds(rb, HG_CHUNK), sl] = outs[u * per + 2 * hd + 1][0]
        for hd in range(HG_HEADS):
            sf_ref[hd] = outs[(unroll - 1) * per + 2 * hd][1]
            sb_ref[hd] = outs[(unroll - 1) * per + 2 * hd + 1][1]
        return 0

    lax.fori_loop(0, ncc // unroll, body, 0)


def _hg_call(hqv, hff, lbf, lbb):
    b, L, _ = hqv.shape
    tl = min(TL_HG, L)
    n = L // tl
    fwd = lambda w: pl.BlockSpec((1, tl, w), lambda bi, i: (bi, i, 0))
    bwd = lambda w, j: pl.BlockSpec((1, tl, w), lambda bi, i: (bi, n - 1 - i, j))
    return pl.pallas_call(
        functools.partial(_hg_kernel, ncc=tl // HG_CHUNK),
        grid=(b, n),
        in_specs=[fwd(2 * MIX), fwd(MIX), bwd(2 * MIX, 0), bwd(MIX, 1),
                  pl.BlockSpec((1, MIX), lambda bi, i: (0, 0)), pl.BlockSpec((1, MIX), lambda bi, i: (0, 0))],
        out_specs=[fwd(MIX), bwd(MIX, 0)],
        out_shape=[jax.ShapeDtypeStruct((b, L, MIX), F32)] * 2,
        scratch_shapes=[pltpu.VMEM((HG_HEADS, HG_D, HG_D), F32), pltpu.VMEM((HG_HEADS, HG_D, HG_D), F32)],
        compiler_params=_params("parallel", "arbitrary"), name="hgrn2",
    )(hqv, hff, hqv, hff, lbf, lbb)


def _merge_kernel(x_ref, mod_ref, zat_ref, yt_ref, of_ref, ob_ref, og_ref, gp_ref,
                  wglut_ref, bglu_ref, ghg_ref, wbr_ref, wout_ref, o_ref):
    g = _gelu_tanh(yt_ref[0])
    glu = _dot(wglut_ref[...], g.astype(BF16)) + bglu_ref[...]
    zbt = (g * _gate(glu)).astype(BF16)
    o = of_ref[0] + ob_ref[0]
    on = jnp.concatenate([_rms(o[:, hd * HG_D:(hd + 1) * HG_D]) for hd in range(HG_HEADS)], axis=1)
    og = og_ref[0].astype(F32)
    zc = (on * ghg_ref[...] * (og * _gate(og))).astype(BF16)
    bra = _dot_tn(zat_ref[0], wbr_ref[0])
    brb = _dot_tn(zbt, wbr_ref[1])
    brc = _dot(zc, wbr_ref[2])
    gp = gp_ref[0].astype(F32)
    mix = (_gate(gp[:, :D_MODEL]) * bra + _gate(gp[:, D_MODEL:2 * D_MODEL]) * brb
           + _gate(gp[:, 2 * D_MODEL:]) * brc)
    out = _dot(mix.astype(BF16), wout_ref[...])
    o_ref[0] = x_ref[0] + mod_ref[0][2:3] * out


def _merge_call(x, mod, zat, yt, of, ob, hog, gp, w):
    b, L, _ = x.shape
    tl = TL_MG
    tok = lambda n: pl.BlockSpec((1, tl, n), lambda bi, i: (bi, i, 0))
    tr = pl.BlockSpec((1, MIX, tl), lambda bi, i: (bi, 0, i))
    consts = [w["wglut"], w["bglu"], w["ghg"], w["wbr"], w["wout"]]
    return pl.pallas_call(
        _merge_kernel, grid=(b, L // tl),
        in_specs=[tok(D_MODEL), pl.BlockSpec((1, 8, D_MODEL), lambda bi, i: (bi, 0, 0)), tr, tr,
                  tok(MIX), tok(MIX), tok(MIX), tok(3 * D_MODEL)] + [_const_spec(a.shape) for a in consts],
        out_specs=tok(D_MODEL),
        out_shape=jax.ShapeDtypeStruct(x.shape, F32),
        compiler_params=_params("parallel", "parallel"), name="merge",
    )(x, mod, zat, yt, of, ob, hog, gp, *consts)


def _ff_chunks():
    out, off = [], 0
    while off < D_FF:
        w = min(FF_CW, D_FF - off)
        out.append((off, w))
        off += w
    return out


def _ffn_kernel(x_ref, xp_ref, xn_ref, mod_ref, gffn_ref, wup_ref, wconv_ref, bconv_ref, wdn_ref, o_ref,
                up_ref, h_ref, acc_ref, *, tl):
    i = pl.program_id(1)
    mod = mod_ref[0]

    def modulate(xv):
        return _rms(xv) * gffn_ref[...] * (1.0 + mod[4:5]) + mod[3:4]

    keep_prev = jnp.where(i > 0, 1.0, 0.0)
    keep_next = jnp.where(i < pl.num_programs(1) - 1, 1.0, 0.0)
    h_ref[...] = jnp.concatenate([modulate(xp_ref[0]) * keep_prev, modulate(x_ref[0]),
                                  modulate(xn_ref[0]) * keep_next], axis=0).astype(BF16)
    chunks = _ff_chunks()

    def up_proj(ci):
        off, cw = chunks[ci]
        for part in range(2):
            col = off + part * D_FF
            up_ref[ci % 2, part, :, 0:cw] = _dot(h_ref[...], wup_ref[:, col:col + cw])

    up_proj(0)
    for ci, (off, cw) in enumerate(chunks):
        if ci + 1 < len(chunks):
            up_proj(ci + 1)
        act = None
        for part in range(2):
            col = off + part * D_FF
            wc = wconv_ref[:, col:col + cw]
            buf = up_ref.at[ci % 2, part]
            y = (buf[pl.ds(HALO - 1, tl), 0:cw] * wc[0:1] + buf[pl.ds(HALO, tl), 0:cw] * wc[1:2]
                 + buf[pl.ds(HALO + 1, tl), 0:cw] * wc[2:3] + bconv_ref[:, col:col + cw])
            act = y * _gate(y) if part == 0 else act * y
        dn = _dot(act.astype(BF16), wdn_ref[off:off + cw, :])
        if ci == 0:
            acc_ref[...] = dn
        else:
            acc_ref[...] += dn
    o_ref[0] = x_ref[0] + mod[5:6] * acc_ref[...]


def _ffn_call(x, mod, w):
    b, L, _ = x.shape
    tl = TL_FF
    hb = tl // HALO
    nh = L // HALO
    consts = [w["gffn"], w["wup"], w["wconv"], w["bconv"], w["wdn"]]
    return pl.pallas_call(
        functools.partial(_ffn_kernel, tl=tl), grid=(b, L // tl),
        in_specs=[pl.BlockSpec((1, tl, D_MODEL), lambda bi, i: (bi, i, 0)),
                  pl.BlockSpec((1, HALO, D_MODEL), lambda bi, i: (bi, jnp.maximum(i * hb - 1, 0), 0)),
                  pl.BlockSpec((1, HALO, D_MODEL), lambda bi, i: (bi, jnp.minimum((i + 1) * hb, nh - 1), 0)),
                  pl.BlockSpec((1, 8, D_MODEL), lambda bi, i: (bi, 0, 0))] + [_const_spec(a.shape) for a in consts],
        out_specs=pl.BlockSpec((1, tl, D_MODEL), lambda bi, i: (bi, i, 0)),
        out_shape=jax.ShapeDtypeStruct(x.shape, F32),
        scratch_shapes=[pltpu.VMEM((2, 2, tl + 2 * HALO, FF_CW), F32), pltpu.VMEM((tl + 2 * HALO, D_MODEL), BF16),
                        pltpu.VMEM((tl, D_MODEL), F32)],
        compiler_params=_params("parallel", "parallel"), name="ffn",
    )(x, x, x, mod, *consts)


def _final_kernel(x_ref, g_ref, o_ref):
    o_ref[0] = _rms(x_ref[0]) * g_ref[...]


def _final_call(x, g):
    b, L, _ = x.shape
    tl = 512
    spec = pl.BlockSpec((1, tl, D_MODEL), lambda bi, i: (bi, i, 0))
    return pl.pallas_call(
        _final_kernel, grid=(b, L // tl), in_specs=[spec, pl.BlockSpec((1, D_MODEL), lambda bi, i: (0, 0))],
        out_specs=spec, out_shape=jax.ShapeDtypeStruct(x.shape, F32),
        compiler_params=_params("parallel", "parallel"), name="final_norm",
    )(x, g)


def _layer_weights(p, l):
    w_in = p["w_in"][l]
    col = lambda i: w_in[:, IN_OFFS[i]:IN_OFFS[i + 1]]
    half = ROPE // 2
    kr = col(2)
    z = lambda n: jnp.zeros((D_MODEL, n), F32)
    wkr = jnp.concatenate([z(NOPE), kr[:, :half], kr[:, half:], z(LANE - NOPE - ROPE),
                           z(NOPE), kr[:, half:], kr[:, :half], z(LANE - NOPE - ROPE)], axis=1)
    wq = p["w_q_up"][l].reshape(Q_LORA, N_HEADS, NOPE + ROPE)
    zq = lambda n: jnp.zeros((Q_LORA, N_HEADS, n), F32)
    x1, x2 = wq[:, :, NOPE:NOPE + half], wq[:, :, NOPE + half:]
    wqa = jnp.concatenate([wq[:, :, :NOPE], x1, x2, zq(LANE - NOPE - ROPE)], axis=2).reshape(Q_LORA, N_HEADS * LANE)
    wqb = jnp.concatenate([zq(NOPE), x2, x1, zq(LANE - NOPE - ROPE)], axis=2).reshape(Q_LORA, N_HEADS * LANE)
    wkv = p["w_kv_up"][l].reshape(KV_LORA, N_HEADS, 2 * NOPE)
    wk = jnp.concatenate([wkv[:, :, :NOPE], jnp.zeros((KV_LORA, N_HEADS, LANE - NOPE), F32)],
                         axis=2).reshape(KV_LORA, N_HEADS * LANE)
    wvt = wkv[:, :, NOPE:].reshape(KV_LORA, MIX).T
    bf = lambda a: a.astype(BF16)
    return dict(
        gmix=p["g_mix"][l][None], gq=p["g_q_lat"][l][None], gkv=p["g_kv_lat"][l][None],
        wlat=bf(jnp.concatenate([col(0), col(1)], axis=1)), wkr=bf(wkr), wut=bf(col(3).T),
        whqv=bf(jnp.concatenate([col(4), col(7)], axis=1)), whog=bf(col(8)),
        whff=bf(jnp.concatenate([col(5), col(6)], axis=1)), wgate=bf(col(9)),
        wqa=bf(wqa), wqb=bf(wqb), wk=bf(wk), wvt=bf(wvt),
        wglut=bf(p["w_glu"][l].T), bglu=p["b_glu"][l][:, None], ghg=p["g_hg_out"][l][None],
        wbr=bf(p["w_branch"][l]), wout=bf(p["w_out"][l]),
        gffn=p["g_ffn"][l][None], wup=bf(p["w_ffn_up"][l]), wconv=p["w_ffn_conv"][l],
        bconv=p["b_ffn_conv"][l][None], wdn=bf(p["w_ffn_down"][l]),
    )


def _rope_tables(L):
    half = ROPE // 2
    inv_freq = 1.0 / (ROPE_BASE ** (jnp.arange(0, ROPE, 2, dtype=F32) / ROPE))
    ang = jnp.arange(L, dtype=F32)[:, None] * inv_freq[None, :]
    cos, sin = jnp.cos(ang), jnp.sin(ang)
    one, zero = jnp.ones((L, NOPE), F32), jnp.zeros((L, NOPE), F32)
    pad = jnp.zeros((L, LANE - NOPE - ROPE), F32)
    return (jnp.concatenate([one, cos, cos, pad], axis=1), jnp.concatenate([zero, -sin, sin, pad], axis=1))


def _trunk(x, mods, weights, s5ops, lb, g_final):
    b, L, _ = x.shape
    cos_t, sin_t = _rope_tables(L)
    for l in range(DEPTH):
        w = weights[l]
        q, k, vt, ut, hqv, hog, hff, gp = _inproj_call(x, mods[l], w, cos_t, sin_t)
        zat = _flash_call(q, k, vt)
        yt = _s5_call(ut, s5ops[l])
        of, ob = _hg_call(hqv, hff, lb[0, l][None], lb[1, l][None])
        x = _merge_call(x, mods[l], zat, yt, of, ob, hog, gp, w)
        x = _ffn_call(x, mods[l], w)
    return _final_call(x, g_final[None])


def kernel(x_prompt, x_sample, c_prompt, c_sample, w_ada, b_ada, g_mix, w_in, g_q_lat, w_q_up, g_kv_lat, w_kv_up,
           s5_lam_re, s5_lam_im, s5_log_dt, s5_b_re, s5_b_im, s5_c_re, s5_c_im, s5_d, w_glu, b_glu, hg_lb_logits,
           g_hg_out, w_branch, w_out, g_ffn, w_ffn_up, w_ffn_conv, b_ffn_conv, w_ffn_down, g_final):
    p = dict(g_mix=g_mix, w_in=w_in, g_q_lat=g_q_lat, w_q_up=w_q_up, g_kv_lat=g_kv_lat, w_kv_up=w_kv_up,
             w_glu=w_glu, b_glu=b_glu, g_hg_out=g_hg_out, w_branch=w_branch, w_out=w_out, g_ffn=g_ffn,
             w_ffn_up=w_ffn_up, w_ffn_conv=w_ffn_conv, b_ffn_conv=b_ffn_conv, w_ffn_down=w_ffn_down)
    depth = w_in.shape[0]
    assert depth == DEPTH
    bp, bs = c_prompt.shape[0], c_sample.shape[0]
    rows = -(-(bp + bs) // 8) * 8
    c_all = jnp.concatenate([c_prompt, c_sample, jnp.zeros((rows - bp - bs, D_MODEL), F32)], axis=0)
    mod_all = _ada_call(c_all, w_ada, b_ada)

    def mods_for(lo, n):
        m = mod_all[:, lo:lo + n].reshape(DEPTH, n, 6, D_MODEL)
        return jnp.pad(m, ((0, 0), (0, 0), (0, 2), (0, 0)))

    weights = [_layer_weights(p, l) for l in range(DEPTH)]
    s5ops = [_s5_operators(s5_lam_re[l], s5_lam_im[l], s5_log_dt[l], s5_b_re[l], s5_b_im[l],
                           s5_c_re[l], s5_c_im[l], s5_d[l]) for l in range(DEPTH)]
    gam = jax.nn.softmax(hg_lb_logits.astype(F32), axis=1)
    lb = jnp.cumsum(gam, axis=1) - gam[:, :1]
    y_prompt = _trunk(x_prompt, mods_for(0, bp), weights, s5ops, lb, g_final)
    y_sample = _trunk(x_sample, mods_for(bp, bs), weights, s5ops, lb, g_final)
    return (y_prompt, y_sample)
```

```python
import functools
import math

import jax
import jax.numpy as jnp
from jax import lax
from jax.experimental import pallas as pl
from jax.experimental.pallas import tpu as pltpu

F32 = jnp.float32
BF16 = jnp.bfloat16

D_MODEL = 1024
DEPTH = 4
MIX = 512
N_HEADS = 8
NOPE = 64
ROPE = 32
Q_LORA = 384
KV_LORA = 256
ROPE_BASE = 10000.0
S5_GROUPS = 32
S5_GROUP = 16
S5_STATE = 64
S5_CHUNK = 128
HG_HEADS = 4
HG_D = 128
HG_CHUNK = 64
HG_UNROLL = 4
D_FF = 2816
EPS = 1e-6
ATTN_SCALE = 1.0 / math.sqrt(NOPE + ROPE)
LOG2E = 1.4426950408889634
IN_OFFS = (0, 384, 640, 672, 1184, 1696, 2208, 2720, 3232, 3744, 6816)

LANE = 128
VMEM_LIMIT = 56 * 1024 * 1024

TL_IN = 512
TQ = 512
TK = 512
FLASH_UNROLL = 8
TL_HG = 256
TL_MG = 512
TL_FF = 512
FF_CW = 768
HALO = 8


def _dot(a, b):
    return jnp.dot(a, b, preferred_element_type=F32)


def _dot_nt(a, b):
    return lax.dot_general(a, b, (((1,), (1,)), ((), ())), preferred_element_type=F32)


def _dot_tn(a, b):
    return lax.dot_general(a, b, (((0,), (0,)), ((), ())), preferred_element_type=F32)


def _sigmoid(x):
    return 1.0 / (1.0 + jnp.exp(-x))


def _gate(x):
    return 0.5 * jnp.tanh(0.5 * x) + 0.5


def _rms(x):
    return x * lax.rsqrt(jnp.mean(x * x, axis=-1, keepdims=True) + EPS)


def _gelu_tanh(x):
    return 0.5 * x * (1.0 + jnp.tanh(math.sqrt(2.0 / math.pi) * (x + 0.044715 * (x * x * x))))


def _params(*sem):
    return pltpu.CompilerParams(dimension_semantics=sem, vmem_limit_bytes=VMEM_LIMIT)


def _const_spec(shape):
    nd = len(shape)
    return pl.BlockSpec(shape, lambda *_: (0,) * nd, pipeline_mode=pl.Buffered(1))


def _ada_kernel(c_ref, w_ref, b_ref, o_ref):
    c = c_ref[...]
    a = (c * _sigmoid(c)).astype(BF16)
    o_ref[0] = _dot(a, w_ref[0].astype(BF16)) + b_ref[0]


def _ada_call(c_all, w_ada, b_ada):
    nb = 1536
    rows = c_all.shape[0]
    return pl.pallas_call(
        _ada_kernel,
        grid=(DEPTH, 6 * D_MODEL // nb),
        in_specs=[pl.BlockSpec((rows, D_MODEL), lambda l, j: (0, 0)),
                  pl.BlockSpec((1, D_MODEL, nb), lambda l, j: (l, 0, j)),
                  pl.BlockSpec((1, 1, nb), lambda l, j: (l, 0, j))],
        out_specs=pl.BlockSpec((1, rows, nb), lambda l, j: (l, 0, j)),
        out_shape=jax.ShapeDtypeStruct((DEPTH, rows, 6 * D_MODEL), F32),
        compiler_params=_params("arbitrary", "arbitrary"),
        name="ada_mod",
    )(c_all, w_ada, b_ada.reshape(DEPTH, 1, 6 * D_MODEL))


def _inproj_kernel(x_ref, mod_ref, gmix_ref, cos_ref, sin_ref,
                   wlat_ref, wkr_ref, wut_ref, whqv_ref, whog_ref, whff_ref, wgate_ref,
                   gq_ref, gkv_ref, wqa_ref, wqb_ref, wk_ref, wvt_ref,
                   q_ref, k_ref, vt_ref, ut_ref, hqv_ref, hog_ref, hff_ref, gp_ref):
    mod = mod_ref[0]
    h = _rms(x_ref[0]) * gmix_ref[...] * (1.0 + mod[1:2]) + mod[0:1]
    hb = h.astype(BF16)
    lat = _dot(hb, wlat_ref[...])
    qn = (_rms(lat[:, :Q_LORA]) * gq_ref[...]).astype(BF16)
    kvn = (_rms(lat[:, Q_LORA:]) * gkv_ref[...]).astype(BF16)
    cos = cos_ref[...]
    sin = sin_ref[...]
    qa = _dot(qn, wqa_ref[...])
    qb = _dot(qn, wqb_ref[...])
    krp = _dot(hb, wkr_ref[...])
    kr = krp[:, :LANE] * cos + krp[:, LANE:] * sin
    kn = _dot(kvn, wk_ref[...])
    for hd in range(N_HEADS):
        sl = slice(hd * LANE, (hd + 1) * LANE)
        q_ref[0, :, sl] = ((qa[:, sl] * cos + qb[:, sl] * sin) * (ATTN_SCALE * LOG2E)).astype(BF16)
        k_ref[0, :, sl] = (kn[:, sl] + kr).astype(BF16)
    vt_ref[0] = _dot_nt(wvt_ref[...], kvn).astype(BF16)
    ut_ref[0] = _dot_nt(wut_ref[...], hb).astype(BF16)
    hqv_ref[0] = _dot(hb, whqv_ref[...]).astype(BF16)
    hog_ref[0] = _dot(hb, whog_ref[...]).astype(BF16)
    hff_ref[0] = _dot(hb, whff_ref[...])
    gp_ref[0] = _dot(hb, wgate_ref[...]).astype(BF16)


def _inproj_call(x, mod, w, cos_t, sin_t):
    b, L, _ = x.shape
    tl = TL_IN
    tok = lambda n: pl.BlockSpec((1, tl, n), lambda bi, i: (bi, i, 0))
    tr = lambda n: pl.BlockSpec((1, n, tl), lambda bi, i: (bi, 0, i))
    weights = [w["gmix"], None, None, w["wlat"], w["wkr"], w["wut"], w["whqv"], w["whog"], w["whff"], w["wgate"],
               w["gq"], w["gkv"], w["wqa"], w["wqb"], w["wk"], w["wvt"]]
    in_specs = [tok(D_MODEL), pl.BlockSpec((1, 8, D_MODEL), lambda bi, i: (bi, 0, 0))]
    args = [x, mod]
    for a in weights:
        if a is None:
            continue
        in_specs.append(_const_spec(a.shape))
        args.append(a)
    in_specs[3:3] = [pl.BlockSpec((tl, LANE), lambda bi, i: (i, 0))] * 2
    args[3:3] = [cos_t, sin_t]
    out_shape = [jax.ShapeDtypeStruct((b, L, 1024), BF16), jax.ShapeDtypeStruct((b, L, 1024), BF16),
                 jax.ShapeDtypeStruct((b, MIX, L), BF16), jax.ShapeDtypeStruct((b, MIX, L), BF16),
                 jax.ShapeDtypeStruct((b, L, 1024), BF16), jax.ShapeDtypeStruct((b, L, MIX), BF16),
                 jax.ShapeDtypeStruct((b, L, 1024), F32), jax.ShapeDtypeStruct((b, L, 3 * D_MODEL), BF16)]
    out_specs = [tok(1024), tok(1024), tr(MIX), tr(MIX), tok(1024), tok(MIX), tok(1024), tok(3 * D_MODEL)]
    return pl.pallas_call(
        _inproj_kernel, grid=(b, L // tl), in_specs=in_specs, out_specs=out_specs, out_shape=out_shape,
        compiler_params=_params("parallel", "parallel"), name="inproj",
    )(*args)


def _flash_kernel(q_ref, k_ref, vt_ref, o_ref, s_ref, *, tk, nk):
    tq = q_ref.shape[1]
    half = NOPE
    ones = jnp.ones((16, tk), BF16)

    def scores(buf, j):
        off = pl.multiple_of(j * tk, tk)
        for hh in range(2):
            s_ref[buf, hh] = _dot_nt(k_ref[0, pl.ds(off, tk), hh * LANE:(hh + 1) * LANE],
                                     q_ref[0, :, hh * LANE:(hh + 1) * LANE])

    def process(buf, j, carry):
        off = pl.multiple_of(j * tk, tk)
        new = []
        for hh in range(2):
            m, acc = carry[hh]
            s = s_ref[buf, hh]
            mn = jnp.maximum(m, jnp.max(s, axis=0, keepdims=True))
            p = jnp.exp2(s - mn).astype(BF16)
            a = jnp.exp2(m - mn)
            v = jnp.concatenate([vt_ref[0, hh * half:(hh + 1) * half, pl.ds(off, tk)], ones], axis=0)
            new.append((mn, a * acc + _dot(v, p)))
        return tuple(new)

    def body(i, carry):
        for u in range(FLASH_UNROLL):
            j = i * FLASH_UNROLL + u
            scores(1 - u % 2, j + 1)
            carry = process(u % 2, j, carry)
        return carry

    scores(0, 0)
    init = tuple((jnp.full((1, tq), -1e30, F32), jnp.zeros((half + 16, tq), F32)) for _ in range(2))
    n_loop = (nk - 1) // FLASH_UNROLL
    res = lax.fori_loop(0, n_loop, body, init)
    for j in range(n_loop * FLASH_UNROLL, nk):
        if j + 1 < nk:
            scores(1 - j % 2, j + 1)
        res = process(j % 2, j, res)
    for hh in range(2):
        acc = res[hh][1]
        o_ref[0, hh * half:(hh + 1) * half, :] = (acc[:half] / acc[half:half + 1]).astype(BF16)


def _flash_call(q, k, vt):
    b, L, _ = q.shape
    tq, tk = min(TQ, L), min(TK, L)
    return pl.pallas_call(
        functools.partial(_flash_kernel, tk=tk, nk=L // tk),
        scratch_shapes=[pltpu.VMEM((2, 2, tk, tq), F32)],
        grid=(b, N_HEADS // 2, L // tq),
        in_specs=[pl.BlockSpec((1, tq, 2 * LANE), lambda bi, p, i: (bi, i, p)),
                  pl.BlockSpec((1, L, 2 * LANE), lambda bi, p, i: (bi, 0, p)),
                  pl.BlockSpec((1, LANE, L), lambda bi, p, i: (bi, p, 0))],
        out_specs=pl.BlockSpec((1, LANE, tq), lambda bi, p, i: (bi, p, i)),
        out_shape=jax.ShapeDtypeStruct((b, MIX, L), BF16),
        compiler_params=_params("parallel", "parallel", "arbitrary"), name="flash",
    )(q, k, vt)


def _s5_kernel(u_ref, t_ref, f_ref, e_ref, a_ref, y_ref, sloc_ref, st_ref, *, nb, nct):
    m = nb * nct
    ucat = jnp.concatenate([u_ref[:, 0, hi].reshape(m, S5_CHUNK) for hi in range(S5_GROUP)], axis=1)
    y = _dot(ucat, t_ref[0])
    sloc_ref[...] = _dot(ucat, f_ref[0])
    a = a_ref[0]
    arf, aif, arb, aib = a[0:1], a[1:2], a[2:3], a[3:4]

    sb = 8 if nct % 8 == 0 else nct
    nblk = nct // sb

    def body(blk, carry):
        new = []
        for bi in range(nb):
            xrf, xif, xrb, xib = carry[bi]
            rf = pl.multiple_of(bi * nct + blk * sb, sb)
            rb = pl.multiple_of(bi * nct + (nblk - 1 - blk) * sb, sb)
            lf = sloc_ref[pl.ds(rf, sb), 0:2 * LANE]
            lb = sloc_ref[pl.ds(rb, sb), 2 * LANE:4 * LANE]
            frows, brows = [], []
            for r in range(sb):
                frows.append((xrf, xif))
                xrf, xif = (arf * xrf - aif * xif + lf[r:r + 1, 0:LANE],
                            arf * xif + aif * xrf + lf[r:r + 1, LANE:2 * LANE])
            for r in range(sb - 1, -1, -1):
                brows.append((xrb, xib))
                xrb, xib = (arb * xrb - aib * xib + lb[r:r + 1, 0:LANE],
                            arb * xib + aib * xrb + lb[r:r + 1, LANE:2 * LANE])
            brows = brows[::-1]
            st_ref[pl.ds(rf, sb), 0:LANE] = jnp.concatenate([t[0] for t in frows], axis=0)
            st_ref[pl.ds(rf, sb), LANE:2 * LANE] = jnp.concatenate([t[1] for t in frows], axis=0)
            st_ref[pl.ds(rb, sb), 2 * LANE:3 * LANE] = jnp.concatenate([t[0] for t in brows], axis=0)
            st_ref[pl.ds(rb, sb), 3 * LANE:4 * LANE] = jnp.concatenate([t[1] for t in brows], axis=0)
            new.append((xrf, xif, xrb, xib))
        return tuple(new)

    z = jnp.zeros((1, LANE), F32)
    lax.fori_loop(0, nblk, body, tuple((z, z, z, z) for _ in range(nb)))
    y = y + _dot(st_ref[...].astype(BF16), e_ref[0])
    for ho in range(S5_GROUP):
        y_ref[:, 0, ho] = y[:, ho * S5_CHUNK:(ho + 1) * S5_CHUNK].reshape(nb, nct, S5_CHUNK)


def _s5_call(ut, ops):
    b, _, L = ut.shape
    nct = L // S5_CHUNK
    u5 = ut.reshape(b, S5_GROUPS, S5_GROUP, nct, S5_CHUNK)
    gw = S5_GROUP * S5_CHUNK
    blk = pl.BlockSpec((b, 1, S5_GROUP, nct, S5_CHUNK), lambda g: (0, g, 0, 0, 0))
    y = pl.pallas_call(
        functools.partial(_s5_kernel, nb=b, nct=nct),
        grid=(S5_GROUPS,),
        in_specs=[blk,
                  pl.BlockSpec((1, gw, gw), lambda g: (g, 0, 0)),
                  pl.BlockSpec((1, gw, 4 * LANE), lambda g: (g, 0, 0)),
                  pl.BlockSpec((1, 4 * LANE, gw), lambda g: (g, 0, 0)),
                  pl.BlockSpec((1, 8, LANE), lambda g: (g, 0, 0))],
        out_specs=blk,
        out_shape=jax.ShapeDtypeStruct(u5.shape, F32),
        scratch_shapes=[pltpu.VMEM((b * nct, 4 * LANE), F32), pltpu.VMEM((b * nct, 4 * LANE), F32)],
        compiler_params=_params("parallel"), name="s5",
    )(u5, ops["t"], ops["f"], ops["e"], ops["a"])
    return y.reshape(b, MIX, L)


def _toeplitz_kernel(w_ref, t_ref):
    c = S5_CHUNK

    def body(hi, _):
        r0 = pl.multiple_of(hi * c, c)
        for ho in range(S5_GROUP):
            row = w_ref[0, hi, ho:ho + 1, :]
            skew = pltpu.roll(jnp.broadcast_to(row, (c, 2 * c)), 0, 1, stride=1, stride_axis=0)
            t_ref[0, pl.ds(r0, c), ho * c:(ho + 1) * c] = skew[:, :c].astype(BF16)
        return 0

    lax.fori_loop(0, S5_GROUP, body, 0)


def _toeplitz_call(w):
    g = w.shape[0]
    gw = S5_GROUP * S5_CHUNK
    return pl.pallas_call(
        _toeplitz_kernel, grid=(g,),
        in_specs=[pl.BlockSpec((1, S5_GROUP, S5_GROUP, 2 * S5_CHUNK), lambda i: (i, 0, 0, 0))],
        out_specs=pl.BlockSpec((1, gw, gw), lambda i: (i, 0, 0)),
        out_shape=jax.ShapeDtypeStruct((g, gw, gw), BF16),
        compiler_params=_params("parallel"), name="s5_toeplitz",
    )(w)


def _s5_operators(lam_re, lam_im, log_dt, b_re, b_im, c_re, c_im, d):
    hp = lax.Precision.HIGHEST
    G, P, GS, C = S5_GROUPS, S5_STATE, S5_GROUP, S5_CHUNK
    lam_re, lam_im = lam_re.astype(F32), lam_im.astype(F32)
    dt = jnp.exp(log_dt.astype(F32))[..., None]
    zr, zi = lam_re * dt, lam_im * dt
    kk = jnp.arange(C + 1, dtype=F32)[None, None, :, None]
    mag = jnp.exp(zr[:, :, None, :] * kk)
    ang = zi[:, :, None, :] * kk
    pw_re, pw_im = mag * jnp.cos(ang), mag * jnp.sin(ang)
    lb_re, lb_im = pw_re[:, :, 1], pw_im[:, :, 1]
    den = lam_re * lam_re + lam_im * lam_im
    nr, ni = lb_re - 1.0, lb_im
    cr = (nr * lam_re + ni * lam_im) / den
    ci = (ni * lam_re - nr * lam_im) / den
    bb_re = cr[..., None] * b_re - ci[..., None] * b_im
    bb_im = cr[..., None] * b_im + ci[..., None] * b_re
    c_re, c_im = c_re.astype(F32), c_im.astype(F32)

    def kern(dr):
        pr, pi_ = pw_re[dr, :, :C], pw_im[dr, :, :C]
        cp_re = c_re[dr][:, None] * pr[:, :, None, :] - c_im[dr][:, None] * pi_[:, :, None, :]
        cp_im = c_re[dr][:, None] * pi_[:, :, None, :] + c_im[dr][:, None] * pr[:, :, None, :]
        return (jnp.einsum('gkhp,gpi->gkhi', cp_re, bb_re[dr], precision=hp)
                - jnp.einsum('gkhp,gpi->gkhi', cp_im, bb_im[dr], precision=hp))

    kf, kb = kern(0), kern(1)
    k0 = kf[:, 0] + kb[:, 0] + jnp.eye(GS, dtype=F32)[None] * d.astype(F32).reshape(G, GS)[:, :, None]
    kfull = jnp.concatenate([kb[:, 1:][:, ::-1], k0[:, None], kf[:, 1:], jnp.zeros((G, 1, GS, GS), F32)], axis=1)
    t_op = _toeplitz_call(jnp.roll(kfull, -(C - 1), axis=1).transpose(0, 3, 2, 1))

    def f_part(dr, idx):
        pr, pi_ = pw_re[dr][:, idx], pw_im[dr][:, idx]
        br, bi = bb_re[dr].transpose(0, 2, 1), bb_im[dr].transpose(0, 2, 1)
        re = pr[:, None] * br[:, :, None] - pi_[:, None] * bi[:, :, None]
        im = pr[:, None] * bi[:, :, None] + pi_[:, None] * br[:, :, None]
        return re.reshape(G, GS * C, P), im.reshape(G, GS * C, P)

    tau = jnp.arange(C)
    ffr, ffi = f_part(0, C - 1 - tau)
    fbr, fbi = f_part(1, tau)
    padl = lambda a: jnp.pad(a, ((0, 0), (0, 0), (0, LANE - P)))
    f_op = jnp.concatenate([padl(ffr), padl(ffi), padl(fbr), padl(fbi)], axis=-1).astype(BF16)

    def e_part(dr, idx):
        pr, pi_ = pw_re[dr][:, idx], pw_im[dr][:, idx]
        cr_, ci_ = c_re[dr].transpose(0, 2, 1), c_im[dr].transpose(0, 2, 1)
        prt, pit = pr.transpose(0, 2, 1), pi_.transpose(0, 2, 1)
        re = cr_[:, :, :, None] * prt[:, :, None, :] - ci_[:, :, :, None] * pit[:, :, None, :]
        im = cr_[:, :, :, None] * pit[:, :, None, :] + ci_[:, :, :, None] * prt[:, :, None, :]
        return re.reshape(G, P, GS * C), -im.reshape(G, P, GS * C)

    efr, efi = e_part(0, tau + 1)
    ebr, ebi = e_part(1, C - tau)
    padr = lambda a: jnp.pad(a, ((0, 0), (0, LANE - P), (0, 0)))
    e_op = jnp.concatenate([padr(efr), padr(efi), padr(ebr), padr(ebi)], axis=1).astype(BF16)

    arows = [pw_re[0, :, C], pw_im[0, :, C], pw_re[1, :, C], pw_im[1, :, C]]
    a_op = jnp.stack([jnp.pad(r, ((0, 0), (0, LANE - P))) for r in arows]
                     + [jnp.zeros((G, LANE), F32)] * 4, axis=1)
    return dict(t=t_op, f=f_op, e=e_op, a=a_op)


def _split3(x):
    hi = x.astype(BF16)
    r1 = x - hi.astype(F32)
    mid = r1.astype(BF16)
    lo = (r1 - mid.astype(F32)).astype(BF16)
    return hi, mid, lo


HG_HALVES = (32, 16, 8, 4, 2, 1)
HG_SEL_HALVES = (8, 4, 2)


def _hg_later(idx, h, fwd):
    return (idx % (2 * h) >= h) if fwd else (idx % (2 * h) < h)


def _hg_anchor(idx, h, fwd):
    base = idx // (2 * h) * (2 * h)
    return base + h if fwd else base + h - 1


def _hg_consts(fwd):
    c = HG_CHUNK
    t = lax.broadcasted_iota(jnp.int32, (c, c), 0)
    s = lax.broadcasted_iota(jnp.int32, (c, c), 1)
    tri = jnp.where((s <= t) if fwd else (s >= t), 1.0, 0.0).astype(BF16)
    masks = {h: (t // (2 * h) == s // (2 * h)) & _hg_later(t, h, fwd) & ~_hg_later(s, h, fwd) for h in HG_HALVES}
    rows = lax.broadcasted_iota(jnp.int32, (c, HG_D), 0)
    later = {h: _hg_later(rows, h, fwd) for h in HG_HALVES}
    sign = {h: jnp.where(later[h], 1.0, -1.0) for h in HG_HALVES}
    sr = lax.broadcasted_iota(jnp.int32, (len(HG_SEL_HALVES) * c, c), 0)
    sc = lax.broadcasted_iota(jnp.int32, (len(HG_SEL_HALVES) * c, c), 1)
    hit = jnp.zeros(sr.shape, jnp.bool_)
    for i, h in enumerate(HG_SEL_HALVES):
        hit = hit | ((sr // c == i) & (sc == _hg_anchor(sr % c, h, fwd)))
    sel = jnp.where(hit, 1.0, 0.0).astype(BF16)
    return dict(tri=tri, masks=masks, diag=(t == s), later=later, sign=sign, sel=sel)


def _hg_chunks(streams):
    c = HG_CHUNK

    gates = []
    for hq, hv, hf, lb, st, fwd, consts in streams:
        q = hq * _gate(hq)
        f = lb + (1.0 - lb) * _sigmoid(hf)
        logf = jnp.log(f)
        p1, p2, p3 = _split3(logf)
        tri = consts["tri"]
        cum = _dot(tri, p1) + _dot(tri, p2) + _dot(tri, p3)
        gates.append((q, 1.0 - f, f, logf, cum))

    refs = []
    for (hq, hv, hf, lb, st, fwd, consts), (q, k, f, logf, cum) in zip(streams, gates):
        cexc = cum - logf
        ref = {}
        for h in HG_HALVES:
            if 2 * h >= 16 and h not in HG_SEL_HALVES:
                ref[h] = jnp.concatenate(
                    [jnp.broadcast_to(cexc[a:a + 1, :], (2 * h, HG_D))
                     for a in (_hg_anchor(b0, h, fwd) for b0 in range(0, c, 2 * h))], axis=0)
        c_hi = cexc.astype(BF16)
        c_mid = (cexc - c_hi.astype(F32)).astype(BF16)
        gathered = _dot(consts["sel"], jnp.concatenate([c_hi, c_mid], axis=1))
        for i, h in enumerate(HG_SEL_HALVES):
            ref[h] = gathered[i * c:(i + 1) * c, :HG_D] + gathered[i * c:(i + 1) * c, HG_D:]
        refs.append(ref)

    scores = []
    for (hq, hv, hf, lb, st, fwd, consts), (q, k, f, logf, cum), ref in zip(streams, gates, refs):
        a = jnp.where(consts["diag"], _dot_nt(q.astype(BF16), k.astype(BF16)), 0.0)
        qf = q * f
        for h in HG_HALVES:
            late = consts["later"][h]
            if h == 1:
                y = jnp.where(late, qf, k)
            else:
                y = jnp.where(late, q, k) * jnp.exp((cum - ref[h]) * consts["sign"][h])
            y = y.astype(BF16)
            a = jnp.where(consts["masks"][h], _dot_nt(y, y), a)
        scores.append(a)

    outs = []
    for (hq, hv, hf, lb, st, fwd, consts), (q, k, f, logf, cum), a in zip(streams, gates, scores):
        if isinstance(st, int):
            st = outs[st][1]
        vb = hv.astype(BF16)
        o = _dot(a.astype(BF16), vb) + _dot_nt((q * jnp.exp(cum)).astype(BF16), st.astype(BF16))
        last = cum[c - 1:c] if fwd else cum[0:1]
        kst = (k * jnp.exp(last - cum)).astype(BF16)
        outs.append((o, st * jnp.exp(last) + _dot_tn(vb, kst)))
    return outs


def _hg_kernel(qvf_ref, ff_ref, qvb_ref, fb_ref, lbf_ref, lbb_ref, of_ref, ob_ref, sf_ref, sb_ref, *, ncc):
    @pl.when(pl.program_id(1) == 0)
    def _():
        sf_ref[...] = jnp.zeros_like(sf_ref)
        sb_ref[...] = jnp.zeros_like(sb_ref)

    cf = _hg_consts(True)
    cb = _hg_consts(False)
    unroll = HG_UNROLL if ncc % HG_UNROLL == 0 else 1
    per = 2 * HG_HEADS

    def body(it, _):
        streams, rows = [], []
        for u in range(unroll):
            cc = it * unroll + u
            rf = pl.multiple_of(cc * HG_CHUNK, HG_CHUNK)
            rb = pl.multiple_of((ncc - 1 - cc) * HG_CHUNK, HG_CHUNK)
            rows.append((rf, rb))
            for hd in range(HG_HEADS):
                sl = slice(hd * HG_D, (hd + 1) * HG_D)
                sv = slice(MIX + hd * HG_D, MIX + (hd + 1) * HG_D)
                prev = len(streams) - per
                streams.append((qvf_ref[0, pl.ds(rf, HG_CHUNK), sl].astype(F32),
                                qvf_ref[0, pl.ds(rf, HG_CHUNK), sv].astype(F32),
                                ff_ref[0, pl.ds(rf, HG_CHUNK), sl], lbf_ref[:, sl],
                                sf_ref[hd] if u == 0 else prev, True, cf))
                prev = len(streams) - per
                streams.append((qvb_ref[0, pl.ds(rb, HG_CHUNK), sl].astype(F32),
                                qvb_ref[0, pl.ds(rb, HG_CHUNK), sv].astype(F32),
                                fb_ref[0, pl.ds(rb, HG_CHUNK), sl], lbb_ref[:, sl],
                                sb_ref[hd] if u == 0 else prev, False, cb))
        outs = _hg_chunks(streams)
        for u, (rf, rb) in enumerate(rows):
            for hd in range(HG_HEADS):
                sl = slice(hd * HG_D, (hd + 1) * HG_D)
                of_ref[0, pl.ds(rf, HG_CHUNK), sl] = outs[u * per + 2 * hd][0]
                ob_ref[0, pl.ds(rb, HG_CHUNK), sl] = outs[u * per + 2 * hd + 1][0]
        for hd in range(HG_HEADS):
            sf_ref[hd] = outs[(unroll - 1) * per + 2 * hd][1]
            sb_ref[hd] = outs[(unroll - 1) * per + 2 * hd + 1][1]
        return 0

    lax.fori_loop(0, ncc // unroll, body, 0)


def _hg_call(hqv, hff, lbf, lbb):
    b, L, _ = hqv.shape
    tl = min(TL_HG, L)
    n = L // tl
    fwd = lambda w: pl.BlockSpec((1, tl, w), lambda bi, i: (bi, i, 0))
    bwd = lambda w, j: pl.BlockSpec((1, tl, w), lambda bi, i: (bi, n - 1 - i, j))
    return pl.pallas_call(
        functools.partial(_hg_kernel, ncc=tl // HG_CHUNK),
        grid=(b, n),
        in_specs=[fwd(2 * MIX), fwd(MIX), bwd(2 * MIX, 0), bwd(MIX, 1),
                  pl.BlockSpec((1, MIX), lambda bi, i: (0, 0)), pl.BlockSpec((1, MIX), lambda bi, i: (0, 0))],
        out_specs=[fwd(MIX), bwd(MIX, 0)],
        out_shape=[jax.ShapeDtypeStruct((b, L, MIX), F32)] * 2,
        scratch_shapes=[pltpu.VMEM((HG_HEADS, HG_D, HG_D), F32), pltpu.VMEM((HG_HEADS, HG_D, HG_D), F32)],
        compiler_params=_params("parallel", "arbitrary"), name="hgrn2",
    )(hqv, hff, hqv, hff, lbf, lbb)


def _merge_kernel(x_ref, mod_ref, zat_ref, yt_ref, of_ref, ob_ref, og_ref, gp_ref,
                  wglut_ref, bglu_ref, ghg_ref, wbr_ref, wout_ref, o_ref):
    g = _gelu_tanh(yt_ref[0])
    glu = _dot(wglut_ref[...], g.astype(BF16)) + bglu_ref[...]
    zbt = (g * _gate(glu)).astype(BF16)
    o = of_ref[0] + ob_ref[0]
    on = jnp.concatenate([_rms(o[:, hd * HG_D:(hd + 1) * HG_D]) for hd in range(HG_HEADS)], axis=1)
    og = og_ref[0].astype(F32)
    zc = (on * ghg_ref[...] * (og * _gate(og))).astype(BF16)
    bra = _dot_tn(zat_ref[0], wbr_ref[0])
    brb = _dot_tn(zbt, wbr_ref[1])
    brc = _dot(zc, wbr_ref[2])
    gp = gp_ref[0].astype(F32)
    mix = (_gate(gp[:, :D_MODEL]) * bra + _gate(gp[:, D_MODEL:2 * D_MODEL]) * brb
           + _gate(gp[:, 2 * D_MODEL:]) * brc)
    out = _dot(mix.astype(BF16), wout_ref[...])
    o_ref[0] = x_ref[0] + mod_ref[0][2:3] * out


def _merge_call(x, mod, zat, yt, of, ob, hog, gp, w):
    b, L, _ = x.shape
    tl = TL_MG
    tok = lambda n: pl.BlockSpec((1, tl, n), lambda bi, i: (bi, i, 0))
    tr = pl.BlockSpec((1, MIX, tl), lambda bi, i: (bi, 0, i))
    consts = [w["wglut"], w["bglu"], w["ghg"], w["wbr"], w["wout"]]
    return pl.pallas_call(
        _merge_kernel, grid=(b, L // tl),
        in_specs=[tok(D_MODEL), pl.BlockSpec((1, 8, D_MODEL), lambda bi, i: (bi, 0, 0)), tr, tr,
                  tok(MIX), tok(MIX), tok(MIX), tok(3 * D_MODEL)] + [_const_spec(a.shape) for a in consts],
        out_specs=tok(D_MODEL),
        out_shape=jax.ShapeDtypeStruct(x.shape, F32),
        compiler_params=_params("parallel", "parallel"), name="merge",
    )(x, mod, zat, yt, of, ob, hog, gp, *consts)


def _ff_chunks():
    out, off = [], 0
    while off < D_FF:
        w = min(FF_CW, D_FF - off)
        out.append((off, w))
        off += w
    return out


def _ffn_kernel(x_ref, xp_ref, xn_ref, mod_ref, gffn_ref, wup_ref, wconv_ref, bconv_ref, wdn_ref, o_ref,
                up_ref, h_ref, acc_ref, *, tl):
    i = pl.program_id(1)
    mod = mod_ref[0]

    def modulate(xv):
        return _rms(xv) * gffn_ref[...] * (1.0 + mod[4:5]) + mod[3:4]

    keep_prev = jnp.where(i > 0, 1.0, 0.0)
    keep_next = jnp.where(i < pl.num_programs(1) - 1, 1.0, 0.0)
    h_ref[...] = jnp.concatenate([modulate(xp_ref[0]) * keep_prev, modulate(x_ref[0]),
                                  modulate(xn_ref[0]) * keep_next], axis=0).astype(BF16)
    chunks = _ff_chunks()

    def up_proj(ci):
        off, cw = chunks[ci]
        for part in range(2):
            col = off + part * D_FF
            up_ref[ci % 2, part, :, 0:cw] = _dot(h_ref[...], wup_ref[:, col:col + cw])

    up_proj(0)
    for ci, (off, cw) in enumerate(chunks):
        if ci + 1 < len(chunks):
            up_proj(ci + 1)
        act = None
        for part in range(2):
            col = off + part * D_FF
            wc = wconv_ref[:, col:col + cw]
            buf = up_ref.at[ci % 2, part]
            y = (buf[pl.ds(HALO - 1, tl), 0:cw] * wc[0:1] + buf[pl.ds(HALO, tl), 0:cw] * wc[1:2]
                 + buf[pl.ds(HALO + 1, tl), 0:cw] * wc[2:3] + bconv_ref[:, col:col + cw])
            act = y * _gate(y) if part == 0 else act * y
        dn = _dot(act.astype(BF16), wdn_ref[off:off + cw, :])
        if ci == 0:
            acc_ref[...] = dn
        else:
            acc_ref[...] += dn
    o_ref[0] = x_ref[0] + mod[5:6] * acc_ref[...]


def _ffn_call(x, mod, w):
    b, L, _ = x.shape
    tl = TL_FF
    hb = tl // HALO
    nh = L // HALO
    consts = [w["gffn"], w["wup"], w["wconv"], w["bconv"], w["wdn"]]
    return pl.pallas_call(
        functools.partial(_ffn_kernel, tl=tl), grid=(b, L // tl),
        in_specs=[pl.BlockSpec((1, tl, D_MODEL), lambda bi, i: (bi, i, 0)),
                  pl.BlockSpec((1, HALO, D_MODEL), lambda bi, i: (bi, jnp.maximum(i * hb - 1, 0), 0)),
                  pl.BlockSpec((1, HALO, D_MODEL), lambda bi, i: (bi, jnp.minimum((i + 1) * hb, nh - 1), 0)),
                  pl.BlockSpec((1, 8, D_MODEL), lambda bi, i: (bi, 0, 0))] + [_const_spec(a.shape) for a in consts],
        out_specs=pl.BlockSpec((1, tl, D_MODEL), lambda bi, i: (bi, i, 0)),
        out_shape=jax.ShapeDtypeStruct(x.shape, F32),
        scratch_shapes=[pltpu.VMEM((2, 2, tl + 2 * HALO, FF_CW), F32), pltpu.VMEM((tl + 2 * HALO, D_MODEL), BF16),
                        pltpu.VMEM((tl, D_MODEL), F32)],
        compiler_params=_params("parallel", "parallel"), name="ffn",
    )(x, x, x, mod, *consts)


def _final_kernel(x_ref, g_ref, o_ref):
    o_ref[0] = _rms(x_ref[0]) * g_ref[...]


def _final_call(x, g):
    b, L, _ = x.shape
    tl = 512
    spec = pl.BlockSpec((1, tl, D_MODEL), lambda bi, i: (bi, i, 0))
    return pl.pallas_call(
        _final_kernel, grid=(b, L // tl), in_specs=[spec, pl.BlockSpec((1, D_MODEL), lambda bi, i: (0, 0))],
        out_specs=spec, out_shape=jax.ShapeDtypeStruct(x.shape, F32),
        compiler_params=_params("parallel", "parallel"), name="final_norm",
    )(x, g)


def _layer_weights(p, l):
    w_in = p["w_in"][l]
    col = lambda i: w_in[:, IN_OFFS[i]:IN_OFFS[i + 1]]
    half = ROPE // 2
    kr = col(2)
    z = lambda n: jnp.zeros((D_MODEL, n), F32)
    wkr = jnp.concatenate([z(NOPE), kr[:, :half], kr[:, half:], z(LANE - NOPE - ROPE),
                           z(NOPE), kr[:, half:], kr[:, :half], z(LANE - NOPE - ROPE)], axis=1)
    wq = p["w_q_up"][l].reshape(Q_LORA, N_HEADS, NOPE + ROPE)
    zq = lambda n: jnp.zeros((Q_LORA, N_HEADS, n), F32)
    x1, x2 = wq[:, :, NOPE:NOPE + half], wq[:, :, NOPE + half:]
    wqa = jnp.concatenate([wq[:, :, :NOPE], x1, x2, zq(LANE - NOPE - ROPE)], axis=2).reshape(Q_LORA, N_HEADS * LANE)
    wqb = jnp.concatenate([zq(NOPE), x2, x1, zq(LANE - NOPE - ROPE)], axis=2).reshape(Q_LORA, N_HEADS * LANE)
    wkv = p["w_kv_up"][l].reshape(KV_LORA, N_HEADS, 2 * NOPE)
    wk = jnp.concatenate([wkv[:, :, :NOPE], jnp.zeros((KV_LORA, N_HEADS, LANE - NOPE), F32)],
                         axis=2).reshape(KV_LORA, N_HEADS * LANE)
    wvt = wkv[:, :, NOPE:].reshape(KV_LORA, MIX).T
    bf = lambda a: a.astype(BF16)
    return dict(
        gmix=p["g_mix"][l][None], gq=p["g_q_lat"][l][None], gkv=p["g_kv_lat"][l][None],
        wlat=bf(jnp.concatenate([col(0), col(1)], axis=1)), wkr=bf(wkr), wut=bf(col(3).T),
        whqv=bf(jnp.concatenate([col(4), col(7)], axis=1)), whog=bf(col(8)),
        whff=bf(jnp.concatenate([col(5), col(6)], axis=1)), wgate=bf(col(9)),
        wqa=bf(wqa), wqb=bf(wqb), wk=bf(wk), wvt=bf(wvt),
        wglut=bf(p["w_glu"][l].T), bglu=p["b_glu"][l][:, None], ghg=p["g_hg_out"][l][None],
        wbr=bf(p["w_branch"][l]), wout=bf(p["w_out"][l]),
        gffn=p["g_ffn"][l][None], wup=bf(p["w_ffn_up"][l]), wconv=p["w_ffn_conv"][l],
        bconv=p["b_ffn_conv"][l][None], wdn=bf(p["w_ffn_down"][l]),
    )


def _rope_tables(L):
    half = ROPE // 2
    inv_freq = 1.0 / (ROPE_BASE ** (jnp.arange(0, ROPE, 2, dtype=F32) / ROPE))
    ang = jnp.arange(L, dtype=F32)[:, None] * inv_freq[None, :]
    cos, sin = jnp.cos(ang), jnp.sin(ang)
    one, zero = jnp.ones((L, NOPE), F32), jnp.zeros((L, NOPE), F32)
    pad = jnp.zeros((L, LANE - NOPE - ROPE), F32)
    return (jnp.concatenate([one, cos, cos, pad], axis=1), jnp.concatenate([zero, -sin, sin, pad], axis=1))


def _trunk(x, mods, weights, s5ops, lb, g_final):
    b, L, _ = x.shape
    cos_t, sin_t = _rope_tables(L)
    for l in range(DEPTH):
        w = weights[l]
        q, k, vt, ut, hqv, hog, hff, gp = _inproj_call(x, mods[l], w, cos_t, sin_t)
        zat = _flash_call(q, k, vt)
        yt = _s5_call(ut, s5ops[l])
        of, ob = _hg_call(hqv, hff, lb[0, l][None], lb[1, l][None])
        x = _merge_call(x, mods[l], zat, yt, of, ob, hog, gp, w)
        x = _ffn_call(x, mods[l], w)
    return _final_call(x, g_final[None])


def kernel(x_prompt, x_sample, c_prompt, c_sample, w_ada, b_ada, g_mix, w_in, g_q_lat, w_q_up, g_kv_lat, w_kv_up,
           s5_lam_re, s5_lam_im, s5_log_dt, s5_b_re, s5_b_im, s5_c_re, s5_c_im, s5_d, w_glu, b_glu, hg_lb_logits,
           g_hg_out, w_branch, w_out, g_ffn, w_ffn_up, w_ffn_conv, b_ffn_conv, w_ffn_down, g_final):
    p = dict(g_mix=g_mix, w_in=w_in, g_q_lat=g_q_lat, w_q_up=w_q_up, g_kv_lat=g_kv_lat, w_kv_up=w_kv_up,
             w_glu=w_glu, b_glu=b_glu, g_hg_out=g_hg_out, w_branch=w_branch, w_out=w_out, g_ffn=g_ffn,
             w_ffn_up=w_ffn_up, w_ffn_conv=w_ffn_conv, b_ffn_conv=b_ffn_conv, w_ffn_down=w_ffn_down)
    depth = w_in.shape[0]
    assert depth == DEPTH
    bp, bs = c_prompt.shape[0], c_sample.shape[0]
    rows = -(-(bp + bs) // 8) * 8
    c_all = jnp.concatenate([c_prompt, c_sample, jnp.zeros((rows - bp - bs, D_MODEL), F32)], axis=0)
    mod_all = _ada_call(c_all, w_ada, b_ada)

    def mods_for(lo, n):
        m = mod_all[:, lo:lo + n].reshape(DEPTH, n, 6, D_MODEL)
        return jnp.pad(m, ((0, 0), (0, 0), (0, 2), (0, 0)))

    weights = [_layer_weights(p, l) for l in range(DEPTH)]
    s5ops = [_s5_operators(s5_lam_re[l], s5_lam_im[l], s5_log_dt[l], s5_b_re[l], s5_b_im[l],
                           s5_c_re[l], s5_c_im[l], s5_d[l]) for l in range(DEPTH)]
    gam = jax.nn.softmax(hg_lb_logits.astype(F32), axis=1)
    lb = jnp.cumsum(gam, axis=1) - gam[:, :1]
    y_prompt = _trunk(x_prompt, mods_for(0, bp), weights, s5ops, lb, g_final)
    y_sample = _trunk(x_sample, mods_for(bp, bs), weights, s5ops, lb, g_final)
    return (y_prompt, y_sample)
```

```python
import functools
import math

import jax
import jax.numpy as jnp
from jax import lax
from jax.experimental import pallas as pl
from jax.experimental.pallas import tpu as pltpu

F32 = jnp.float32
BF16 = jnp.bfloat16

D_MODEL = 1024
DEPTH = 4
MIX = 512
N_HEADS = 8
NOPE = 64
ROPE = 32
Q_LORA = 384
KV_LORA = 256
ROPE_BASE = 10000.0
S5_GROUPS = 32
S5_GROUP = 16
S5_STATE = 64
S5_CHUNK = 128
HG_HEADS = 4
HG_D = 128
HG_CHUNK = 64
HG_UNROLL = 4
D_FF = 2816
EPS = 1e-6
ATTN_SCALE = 1.0 / math.sqrt(NOPE + ROPE)
LOG2E = 1.4426950408889634
IN_OFFS = (0, 384, 640, 672, 1184, 1696, 2208, 2720, 3232, 3744, 6816)

LANE = 128
VMEM_LIMIT = 56 * 1024 * 1024

TL_IN = 512
TQ = 512
TK = 512
FLASH_UNROLL = 8
TL_HG = 256
TL_MG = 512
TL_FF = 512
FF_CW = 768
HALO = 8


def _dot(a, b):
    return jnp.dot(a, b, preferred_element_type=F32)


def _dot_nt(a, b):
    return lax.dot_general(a, b, (((1,), (1,)), ((), ())), preferred_element_type=F32)


def _dot_tn(a, b):
    return lax.dot_general(a, b, (((0,), (0,)), ((), ())), preferred_element_type=F32)


def _sigmoid(x):
    return 1.0 / (1.0 + jnp.exp(-x))


def _gate(x):
    return 0.5 * jnp.tanh(0.5 * x) + 0.5


def _rms(x):
    return x * lax.rsqrt(jnp.mean(x * x, axis=-1, keepdims=True) + EPS)


def _gelu_tanh(x):
    return 0.5 * x * (1.0 + jnp.tanh(math.sqrt(2.0 / math.pi) * (x + 0.044715 * (x * x * x))))


def _params(*sem):
    return pltpu.CompilerParams(dimension_semantics=sem, vmem_limit_bytes=VMEM_LIMIT)


def _const_spec(shape):
    nd = len(shape)
    return pl.BlockSpec(shape, lambda *_: (0,) * nd, pipeline_mode=pl.Buffered(1))


def _ada_kernel(c_ref, w_ref, b_ref, o_ref):
    c = c_ref[...]
    a = (c * _sigmoid(c)).astype(BF16)
    o_ref[0] = _dot(a, w_ref[0].astype(BF16)) + b_ref[0]


def _ada_call(c_all, w_ada, b_ada):
    nb = 1536
    rows = c_all.shape[0]
    return pl.pallas_call(
        _ada_kernel,
        grid=(DEPTH, 6 * D_MODEL // nb),
        in_specs=[pl.BlockSpec((rows, D_MODEL), lambda l, j: (0, 0)),
                  pl.BlockSpec((1, D_MODEL, nb), lambda l, j: (l, 0, j)),
                  pl.BlockSpec((1, 1, nb), lambda l, j: (l, 0, j))],
        out_specs=pl.BlockSpec((1, rows, nb), lambda l, j: (l, 0, j)),
        out_shape=jax.ShapeDtypeStruct((DEPTH, rows, 6 * D_MODEL), F32),
        compiler_params=_params("arbitrary", "arbitrary"),
        name="ada_mod",
    )(c_all, w_ada, b_ada.reshape(DEPTH, 1, 6 * D_MODEL))


def _inproj_kernel(x_ref, mod_ref, gmix_ref, cos_ref, sin_ref,
                   wlat_ref, wkr_ref, wut_ref, whqv_ref, whog_ref, whff_ref, wgate_ref,
                   gq_ref, gkv_ref, wqa_ref, wqb_ref, wk_ref, wvt_ref,
                   q_ref, k_ref, vt_ref, ut_ref, hqv_ref, hog_ref, hff_ref, gp_ref):
    mod = mod_ref[0]
    h = _rms(x_ref[0]) * gmix_ref[...] * (1.0 + mod[1:2]) + mod[0:1]
    hb = h.astype(BF16)
    lat = _dot(hb, wlat_ref[...])
    qn = (_rms(lat[:, :Q_LORA]) * gq_ref[...]).astype(BF16)
    kvn = (_rms(lat[:, Q_LORA:]) * gkv_ref[...]).astype(BF16)
    cos = cos_ref[...]
    sin = sin_ref[...]
    qa = _dot(qn, wqa_ref[...])
    qb = _dot(qn, wqb_ref[...])
    krp = _dot(hb, wkr_ref[...])
    kr = krp[:, :LANE] * cos + krp[:, LANE:] * sin
    kn = _dot(kvn, wk_ref[...])
    for hd in range(N_HEADS):
        sl = slice(hd * LANE, (hd + 1) * LANE)
        q_ref[0, :, sl] = ((qa[:, sl] * cos + qb[:, sl] * sin) * (ATTN_SCALE * LOG2E)).astype(BF16)
        k_ref[0, :, sl] = (kn[:, sl] + kr).astype(BF16)
    vt_ref[0] = _dot_nt(wvt_ref[...], kvn).astype(BF16)
    ut = _dot_nt(wut_ref[...], hb)
    for ch in range(ut.shape[1] // S5_CHUNK):
        ut_ref[0, :, ch * S5_GROUP:(ch + 1) * S5_GROUP, :] = (
            ut[:, ch * S5_CHUNK:(ch + 1) * S5_CHUNK].reshape(S5_GROUPS, S5_GROUP, S5_CHUNK))
    hqv_ref[0] = _dot(hb, whqv_ref[...]).astype(BF16)
    hog_ref[0] = _dot(hb, whog_ref[...]).astype(BF16)
    hff_ref[0] = _dot(hb, whff_ref[...])
    gp_ref[0] = _dot(hb, wgate_ref[...]).astype(BF16)


def _inproj_call(x, mod, w, cos_t, sin_t):
    b, L, _ = x.shape
    tl = TL_IN
    tok = lambda n: pl.BlockSpec((1, tl, n), lambda bi, i: (bi, i, 0))
    tr = lambda n: pl.BlockSpec((1, n, tl), lambda bi, i: (bi, 0, i))
    weights = [w["gmix"], None, None, w["wlat"], w["wkr"], w["wut"], w["whqv"], w["whog"], w["whff"], w["wgate"],
               w["gq"], w["gkv"], w["wqa"], w["wqb"], w["wk"], w["wvt"]]
    in_specs = [tok(D_MODEL), pl.BlockSpec((1, 8, D_MODEL), lambda bi, i: (bi, 0, 0))]
    args = [x, mod]
    for a in weights:
        if a is None:
            continue
        in_specs.append(_const_spec(a.shape))
        args.append(a)
    in_specs[3:3] = [pl.BlockSpec((tl, LANE), lambda bi, i: (i, 0))] * 2
    args[3:3] = [cos_t, sin_t]
    out_shape = [jax.ShapeDtypeStruct((b, L, 1024), BF16), jax.ShapeDtypeStruct((b, L, 1024), BF16),
                 jax.ShapeDtypeStruct((b, MIX, L), BF16),
                 jax.ShapeDtypeStruct((b, S5_GROUPS, L // S5_CHUNK * S5_GROUP, S5_CHUNK), F32),
                 jax.ShapeDtypeStruct((b, L, 1024), BF16), jax.ShapeDtypeStruct((b, L, MIX), BF16),
                 jax.ShapeDtypeStruct((b, L, 1024), F32), jax.ShapeDtypeStruct((b, L, 3 * D_MODEL), BF16)]
    grp = pl.BlockSpec((1, S5_GROUPS, tl // S5_CHUNK * S5_GROUP, S5_CHUNK), lambda bi, i: (bi, 0, i, 0))
    out_specs = [tok(1024), tok(1024), tr(MIX), grp, tok(1024), tok(MIX), tok(1024), tok(3 * D_MODEL)]
    return pl.pallas_call(
        _inproj_kernel, grid=(b, L // tl), in_specs=in_specs, out_specs=out_specs, out_shape=out_shape,
        compiler_params=_params("parallel", "parallel"), name="inproj",
    )(*args)


def _flash_kernel(q_ref, k_ref, vt_ref, o_ref, s_ref, *, tk, nk):
    tq = q_ref.shape[1]
    half = NOPE
    ones = jnp.ones((16, tk), BF16)

    def scores(buf, j):
        off = pl.multiple_of(j * tk, tk)
        for hh in range(2):
            s_ref[buf, hh] = _dot_nt(k_ref[0, pl.ds(off, tk), hh * LANE:(hh + 1) * LANE],
                                     q_ref[0, :, hh * LANE:(hh + 1) * LANE])

    def process(buf, j, carry):
        off = pl.multiple_of(j * tk, tk)
        new = []
        for hh in range(2):
            m, acc = carry[hh]
            s = s_ref[buf, hh]
            mn = jnp.maximum(m, jnp.max(s, axis=0, keepdims=True))
            p = jnp.exp2(s - mn).astype(BF16)
            a = jnp.exp2(m - mn)
            v = jnp.concatenate([vt_ref[0, hh * half:(hh + 1) * half, pl.ds(off, tk)], ones], axis=0)
            new.append((mn, a * acc + _dot(v, p)))
        return tuple(new)

    def body(i, carry):
        for u in range(FLASH_UNROLL):
            j = i * FLASH_UNROLL + u
            scores(1 - u % 2, j + 1)
            carry = process(u % 2, j, carry)
        return carry

    scores(0, 0)
    init = tuple((jnp.full((1, tq), -1e30, F32), jnp.zeros((half + 16, tq), F32)) for _ in range(2))
    n_loop = (nk - 1) // FLASH_UNROLL
    res = lax.fori_loop(0, n_loop, body, init)
    for j in range(n_loop * FLASH_UNROLL, nk):
        if j + 1 < nk:
            scores(1 - j % 2, j + 1)
        res = process(j % 2, j, res)
    for hh in range(2):
        acc = res[hh][1]
        o_ref[0, hh * half:(hh + 1) * half, :] = (acc[:half] / acc[half:half + 1]).astype(BF16)


def _flash_call(q, k, vt):
    b, L, _ = q.shape
    tq, tk = min(TQ, L), min(TK, L)
    return pl.pallas_call(
        functools.partial(_flash_kernel, tk=tk, nk=L // tk),
        scratch_shapes=[pltpu.VMEM((2, 2, tk, tq), F32)],
        grid=(b, N_HEADS // 2, L // tq),
        in_specs=[pl.BlockSpec((1, tq, 2 * LANE), lambda bi, p, i: (bi, i, p)),
                  pl.BlockSpec((1, L, 2 * LANE), lambda bi, p, i: (bi, 0, p)),
                  pl.BlockSpec((1, LANE, L), lambda bi, p, i: (bi, p, 0))],
        out_specs=pl.BlockSpec((1, LANE, tq), lambda bi, p, i: (bi, p, i)),
        out_shape=jax.ShapeDtypeStruct((b, MIX, L), BF16),
        compiler_params=_params("parallel", "parallel", "arbitrary"), name="flash",
    )(q, k, vt)


def _s5_kernel(u_ref, t_ref, f_ref, e_ref, a_ref, y_ref, sloc_ref, st_ref, *, nb, nct):
    m = nb * nct
    ucat = jnp.concatenate(
        [jnp.concatenate([u_ref[bi, 0, pl.ds(hi, nct, stride=S5_GROUP), :].astype(BF16) for hi in range(S5_GROUP)],
                         axis=1) for bi in range(nb)], axis=0)
    y = _dot(ucat, t_ref[0])
    sloc_ref[...] = _dot(ucat, f_ref[0])
    a = a_ref[0]
    arf, aif, arb, aib = a[0:1], a[1:2], a[2:3], a[3:4]

    sb = 8 if nct % 8 == 0 else nct
    nblk = nct // sb

    def body(blk, carry):
        new = []
        for bi in range(nb):
            xrf, xif, xrb, xib = carry[bi]
            rf = pl.multiple_of(bi * nct + blk * sb, sb)
            rb = pl.multiple_of(bi * nct + (nblk - 1 - blk) * sb, sb)
            lf = sloc_ref[pl.ds(rf, sb), 0:2 * LANE]
            lb = sloc_ref[pl.ds(rb, sb), 2 * LANE:4 * LANE]
            frows, brows = [], []
            for r in range(sb):
                frows.append((xrf, xif))
                xrf, xif = (arf * xrf - aif * xif + lf[r:r + 1, 0:LANE],
                            arf * xif + aif * xrf + lf[r:r + 1, LANE:2 * LANE])
            for r in range(sb - 1, -1, -1):
                brows.append((xrb, xib))
                xrb, xib = (arb * xrb - aib * xib + lb[r:r + 1, 0:LANE],
                            arb * xib + aib * xrb + lb[r:r + 1, LANE:2 * LANE])
            brows = brows[::-1]
            st_ref[pl.ds(rf, sb), 0:LANE] = jnp.concatenate([t[0] for t in frows], axis=0)
            st_ref[pl.ds(rf, sb), LANE:2 * LANE] = jnp.concatenate([t[1] for t in frows], axis=0)
            st_ref[pl.ds(rb, sb), 2 * LANE:3 * LANE] = jnp.concatenate([t[0] for t in brows], axis=0)
            st_ref[pl.ds(rb, sb), 3 * LANE:4 * LANE] = jnp.concatenate([t[1] for t in brows], axis=0)
            new.append((xrf, xif, xrb, xib))
        return tuple(new)

    z = jnp.zeros((1, LANE), F32)
    lax.fori_loop(0, nblk, body, tuple((z, z, z, z) for _ in range(nb)))
    y = y + _dot(st_ref[...].astype(BF16), e_ref[0])
    for ho in range(S5_GROUP):
        for bi in range(nb):
            y_ref[bi, 0, pl.ds(ho, nct, stride=S5_GROUP), :] = y[bi * nct:(bi + 1) * nct,
                                                                 ho * S5_CHUNK:(ho + 1) * S5_CHUNK]


def _s5_call(u4, ops):
    b, _, rows, _ = u4.shape
    nct = rows // S5_GROUP
    gw = S5_GROUP * S5_CHUNK
    blk = pl.BlockSpec((b, 1, rows, S5_CHUNK), lambda g: (0, g, 0, 0))
    return pl.pallas_call(
        functools.partial(_s5_kernel, nb=b, nct=nct),
        grid=(S5_GROUPS,),
        in_specs=[blk,
                  pl.BlockSpec((1, gw, gw), lambda g: (g, 0, 0)),
                  pl.BlockSpec((1, gw, 4 * LANE), lambda g: (g, 0, 0)),
                  pl.BlockSpec((1, 4 * LANE, gw), lambda g: (g, 0, 0)),
                  pl.BlockSpec((1, 8, LANE), lambda g: (g, 0, 0))],
        out_specs=blk,
        out_shape=jax.ShapeDtypeStruct(u4.shape, F32),
        scratch_shapes=[pltpu.VMEM((b * nct, 4 * LANE), F32), pltpu.VMEM((b * nct, 4 * LANE), F32)],
        compiler_params=_params("parallel"), name="s5",
    )(u4, ops["t"], ops["f"], ops["e"], ops["a"])


def _toeplitz_kernel(w_ref, t_ref):
    c = S5_CHUNK

    def body(hi, _):
        r0 = pl.multiple_of(hi * c, c)
        for ho in range(S5_GROUP):
            row = w_ref[0, hi, ho:ho + 1, :]
            skew = pltpu.roll(jnp.broadcast_to(row, (c, 2 * c)), 0, 1, stride=1, stride_axis=0)
            t_ref[0, pl.ds(r0, c), ho * c:(ho + 1) * c] = skew[:, :c].astype(BF16)
        return 0

    lax.fori_loop(0, S5_GROUP, body, 0)


def _toeplitz_call(w):
    g = w.shape[0]
    gw = S5_GROUP * S5_CHUNK
    return pl.pallas_call(
        _toeplitz_kernel, grid=(g,),
        in_specs=[pl.BlockSpec((1, S5_GROUP, S5_GROUP, 2 * S5_CHUNK), lambda i: (i, 0, 0, 0))],
        out_specs=pl.BlockSpec((1, gw, gw), lambda i: (i, 0, 0)),
        out_shape=jax.ShapeDtypeStruct((g, gw, gw), BF16),
        compiler_params=_params("parallel"), name="s5_toeplitz",
    )(w)


def _s5_operators(lam_re, lam_im, log_dt, b_re, b_im, c_re, c_im, d):
    hp = lax.Precision.HIGHEST
    G, P, GS, C = S5_GROUPS, S5_STATE, S5_GROUP, S5_CHUNK
    lam_re, lam_im = lam_re.astype(F32), lam_im.astype(F32)
    dt = jnp.exp(log_dt.astype(F32))[..., None]
    zr, zi = lam_re * dt, lam_im * dt
    kk = jnp.arange(C + 1, dtype=F32)[None, None, :, None]
    mag = jnp.exp(zr[:, :, None, :] * kk)
    ang = zi[:, :, None, :] * kk
    pw_re, pw_im = mag * jnp.cos(ang), mag * jnp.sin(ang)
    lb_re, lb_im = pw_re[:, :, 1], pw_im[:, :, 1]
    den = lam_re * lam_re + lam_im * lam_im
    nr, ni = lb_re - 1.0, lb_im
    cr = (nr * lam_re + ni * lam_im) / den
    ci = (ni * lam_re - nr * lam_im) / den
    bb_re = cr[..., None] * b_re - ci[..., None] * b_im
    bb_im = cr[..., None] * b_im + ci[..., None] * b_re
    c_re, c_im = c_re.astype(F32), c_im.astype(F32)

    def kern(dr):
        pr, pi_ = pw_re[dr, :, :C], pw_im[dr, :, :C]
        cp_re = c_re[dr][:, None] * pr[:, :, None, :] - c_im[dr][:, None] * pi_[:, :, None, :]
        cp_im = c_re[dr][:, None] * pi_[:, :, None, :] + c_im[dr][:, None] * pr[:, :, None, :]
        return (jnp.einsum('gkhp,gpi->gkhi', cp_re, bb_re[dr], precision=hp)
                - jnp.einsum('gkhp,gpi->gkhi', cp_im, bb_im[dr], precision=hp))

    kf, kb = kern(0), kern(1)
    k0 = kf[:, 0] + kb[:, 0] + jnp.eye(GS, dtype=F32)[None] * d.astype(F32).reshape(G, GS)[:, :, None]
    kfull = jnp.concatenate([kb[:, 1:][:, ::-1], k0[:, None], kf[:, 1:], jnp.zeros((G, 1, GS, GS), F32)], axis=1)
    t_op = _toeplitz_call(jnp.roll(kfull, -(C - 1), axis=1).transpose(0, 3, 2, 1))

    def f_part(dr, idx):
        pr, pi_ = pw_re[dr][:, idx], pw_im[dr][:, idx]
        br, bi = bb_re[dr].transpose(0, 2, 1), bb_im[dr].transpose(0, 2, 1)
        re = pr[:, None] * br[:, :, None] - pi_[:, None] * bi[:, :, None]
        im = pr[:, None] * bi[:, :, None] + pi_[:, None] * br[:, :, None]
        return re.reshape(G, GS * C, P), im.reshape(G, GS * C, P)

    tau = jnp.arange(C)
    ffr, ffi = f_part(0, C - 1 - tau)
    fbr, fbi = f_part(1, tau)
    padl = lambda a: jnp.pad(a, ((0, 0), (0, 0), (0, LANE - P)))
    f_op = jnp.concatenate([padl(ffr), padl(ffi), padl(fbr), padl(fbi)], axis=-1).astype(BF16)

    def e_part(dr, idx):
        pr, pi_ = pw_re[dr][:, idx], pw_im[dr][:, idx]
        cr_, ci_ = c_re[dr].transpose(0, 2, 1), c_im[dr].transpose(0, 2, 1)
        prt, pit = pr.transpose(0, 2, 1), pi_.transpose(0, 2, 1)
        re = cr_[:, :, :, None] * prt[:, :, None, :] - ci_[:, :, :, None] * pit[:, :, None, :]
        im = cr_[:, :, :, None] * pit[:, :, None, :] + ci_[:, :, :, None] * prt[:, :, None, :]
        return re.reshape(G, P, GS * C), -im.reshape(G, P, GS * C)

    efr, efi = e_part(0, tau + 1)
    ebr, ebi = e_part(1, C - tau)
    padr = lambda a: jnp.pad(a, ((0, 0), (0, LANE - P), (0, 0)))
    e_op = jnp.concatenate([padr(efr), padr(efi), padr(ebr), padr(ebi)], axis=1).astype(BF16)

    arows = [pw_re[0, :, C], pw_im[0, :, C], pw_re[1, :, C], pw_im[1, :, C]]
    a_op = jnp.stack([jnp.pad(r, ((0, 0), (0, LANE - P))) for r in arows]
                     + [jnp.zeros((G, LANE), F32)] * 4, axis=1)
    return dict(t=t_op, f=f_op, e=e_op, a=a_op)


def _split3(x):
    hi = x.astype(BF16)
    r1 = x - hi.astype(F32)
    mid = r1.astype(BF16)
    lo = (r1 - mid.astype(F32)).astype(BF16)
    return hi, mid, lo


HG_HALVES = (32, 16, 8, 4, 2, 1)
HG_SEL_HALVES = (8, 4, 2)


def _hg_later(idx, h, fwd):
    return (idx % (2 * h) >= h) if fwd else (idx % (2 * h) < h)


def _hg_anchor(idx, h, fwd):
    base = idx // (2 * h) * (2 * h)
    return base + h if fwd else base + h - 1


def _hg_consts(fwd):
    c = HG_CHUNK
    t = lax.broadcasted_iota(jnp.int32, (c, c), 0)
    s = lax.broadcasted_iota(jnp.int32, (c, c), 1)
    tri = jnp.where((s <= t) if fwd else (s >= t), 1.0, 0.0).astype(BF16)
    masks = {h: (t // (2 * h) == s // (2 * h)) & _hg_later(t, h, fwd) & ~_hg_later(s, h, fwd) for h in HG_HALVES}
    rows = lax.broadcasted_iota(jnp.int32, (c, HG_D), 0)
    later = {h: _hg_later(rows, h, fwd) for h in HG_HALVES}
    sign = {h: jnp.where(later[h], 1.0, -1.0) for h in HG_HALVES}
    sr = lax.broadcasted_iota(jnp.int32, (len(HG_SEL_HALVES) * c, c), 0)
    sc = lax.broadcasted_iota(jnp.int32, (len(HG_SEL_HALVES) * c, c), 1)
    hit = jnp.zeros(sr.shape, jnp.bool_)
    for i, h in enumerate(HG_SEL_HALVES):
        hit = hit | ((sr // c == i) & (sc == _hg_anchor(sr % c, h, fwd)))
    sel = jnp.where(hit, 1.0, 0.0).astype(BF16)
    return dict(tri=tri, masks=masks, diag=(t == s), later=later, sign=sign, sel=sel)


def _hg_chunks(streams):
    c = HG_CHUNK

    gates = []
    for hq, hv, hf, lb, st, fwd, consts in streams:
        q = hq * _gate(hq)
        f = lb + (1.0 - lb) * _sigmoid(hf)
        logf = jnp.log(f)
        p1, p2, p3 = _split3(logf)
        tri = consts["tri"]
        cum = _dot(tri, p1) + _dot(tri, p2) + _dot(tri, p3)
        gates.append((q, 1.0 - f, f, logf, cum))

    refs = []
    for (hq, hv, hf, lb, st, fwd, consts), (q, k, f, logf, cum) in zip(streams, gates):
        cexc = cum - logf
        ref = {}
        for h in HG_HALVES:
            if 2 * h >= 16 and h not in HG_SEL_HALVES:
                ref[h] = jnp.concatenate(
                    [jnp.broadcast_to(cexc[a:a + 1, :], (2 * h, HG_D))
                     for a in (_hg_anchor(b0, h, fwd) for b0 in range(0, c, 2 * h))], axis=0)
        c_hi = cexc.astype(BF16)
        c_mid = (cexc - c_hi.astype(F32)).astype(BF16)
        gathered = _dot(consts["sel"], jnp.concatenate([c_hi, c_mid], axis=1))
        for i, h in enumerate(HG_SEL_HALVES):
            ref[h] = gathered[i * c:(i + 1) * c, :HG_D] + gathered[i * c:(i + 1) * c, HG_D:]
        refs.append(ref)

    scores = []
    for (hq, hv, hf, lb, st, fwd, consts), (q, k, f, logf, cum), ref in zip(streams, gates, refs):
        a = jnp.where(consts["diag"], _dot_nt(q.astype(BF16), k.astype(BF16)), 0.0)
        qf = q * f
        for h in HG_HALVES:
            late = consts["later"][h]
            if h == 1:
                y = jnp.where(late, qf, k)
            else:
                y = jnp.where(late, q, k) * jnp.exp((cum - ref[h]) * consts["sign"][h])
            y = y.astype(BF16)
            a = jnp.where(consts["masks"][h], _dot_nt(y, y), a)
        scores.append(a)

    outs = []
    for (hq, hv, hf, lb, st, fwd, consts), (q, k, f, logf, cum), a in zip(streams, gates, scores):
        if isinstance(st, int):
            st = outs[st][1]
        vb = hv.astype(BF16)
        o = _dot(a.astype(BF16), vb) + _dot_nt((q * jnp.exp(cum)).astype(BF16), st.astype(BF16))
        last = cum[c - 1:c] if fwd else cum[0:1]
        kst = (k * jnp.exp(last - cum)).astype(BF16)
        outs.append((o, st * jnp.exp(last) + _dot_tn(vb, kst)))
    return outs


def _hg_kernel(qvf_ref, ff_ref, qvb_ref, fb_ref, lbf_ref, lbb_ref, of_ref, ob_ref, sf_ref, sb_ref, *, ncc):
    @pl.when(pl.program_id(1) == 0)
    def _():
        sf_ref[...] = jnp.zeros_like(sf_ref)
        sb_ref[...] = jnp.zeros_like(sb_ref)

    cf = _hg_consts(True)
    cb = _hg_consts(False)
    unroll = HG_UNROLL if ncc % HG_UNROLL == 0 else 1
    per = 2 * HG_HEADS

    def body(it, _):
        streams, rows = [], []
        for u in range(unroll):
            cc = it * unroll + u
            rf = pl.multiple_of(cc * HG_CHUNK, HG_CHUNK)
            rb = pl.multiple_of((ncc - 1 - cc) * HG_CHUNK, HG_CHUNK)
            rows.append((rf, rb))
            for hd in range(HG_HEADS):
                sl = slice(hd * HG_D, (hd + 1) * HG_D)
                sv = slice(MIX + hd * HG_D, MIX + (hd + 1) * HG_D)
                prev = len(streams) - per
                streams.append((qvf_ref[0, pl.ds(rf, HG_CHUNK), sl].astype(F32),
                                qvf_ref[0, pl.ds(rf, HG_CHUNK), sv].astype(F32),
                                ff_ref[0, pl.ds(rf, HG_CHUNK), sl], lbf_ref[:, sl],
                                sf_ref[hd] if u == 0 else prev, True, cf))
                prev = len(streams) - per
                streams.append((qvb_ref[0, pl.ds(rb, HG_CHUNK), sl].astype(F32),
                                qvb_ref[0, pl.ds(rb, HG_CHUNK), sv].astype(F32),
                                fb_ref[0, pl.ds(rb, HG_CHUNK), sl], lbb_ref[:, sl],
                                sb_ref[hd] if u == 0 else prev, False, cb))
        outs = _hg_chunks(streams)
        for u, (rf, rb) in enumerate(rows):
            for hd in range(HG_HEADS):
                sl = slice(hd * HG_D, (hd + 1) * HG_D)
                of_ref[0, pl.ds(rf, HG_CHUNK), sl] = outs[u * per + 2 * hd][0].astype(BF16)
                ob_ref[0, pl.ds(rb, HG_CHUNK), sl] = outs[u * per + 2 * hd + 1][0].astype(BF16)
        for hd in range(HG_HEADS):
            sf_ref[hd] = outs[(unroll - 1) * per + 2 * hd][1]
            sb_ref[hd] = outs[(unroll - 1) * per + 2 * hd + 1][1]
        return 0

    lax.fori_loop(0, ncc // unroll, body, 0)


def _hg_call(hqv, hff, lbf, lbb):
    b, L, _ = hqv.shape
    tl = min(TL_HG, L)
    n = L // tl
    fwd = lambda w: pl.BlockSpec((1, tl, w), lambda bi, i: (bi, i, 0))
    bwd = lambda w, j: pl.BlockSpec((1, tl, w), lambda bi, i: (bi, n - 1 - i, j))
    return pl.pallas_call(
        functools.partial(_hg_kernel, ncc=tl // HG_CHUNK),
        grid=(b, n),
        in_specs=[fwd(2 * MIX), fwd(MIX), bwd(2 * MIX, 0), bwd(MIX, 1),
                  pl.BlockSpec((1, MIX), lambda bi, i: (0, 0)), pl.BlockSpec((1, MIX), lambda bi, i: (0, 0))],
        out_specs=[fwd(MIX), bwd(MIX, 0)],
        out_shape=[jax.ShapeDtypeStruct((b, L, MIX), BF16)] * 2,
        scratch_shapes=[pltpu.VMEM((HG_HEADS, HG_D, HG_D), F32), pltpu.VMEM((HG_HEADS, HG_D, HG_D), F32)],
        compiler_params=_params("parallel", "arbitrary"), name="hgrn2",
    )(hqv, hff, hqv, hff, lbf, lbb)


def _merge_kernel(x_ref, mod_ref, zat_ref, yt_ref, of_ref, ob_ref, og_ref, gp_ref,
                  wglut_ref, bglu_ref, ghg_ref, wbr_ref, wout_ref, o_ref):
    nch = yt_ref.shape[2] // S5_GROUP
    yt = jnp.concatenate([yt_ref[0, :, ch * S5_GROUP:(ch + 1) * S5_GROUP, :].reshape(MIX, S5_CHUNK)
                          for ch in range(nch)], axis=1)
    g = _gelu_tanh(yt)
    glu = _dot(wglut_ref[...], g.astype(BF16)) + bglu_ref[...]
    zbt = (g * _gate(glu)).astype(BF16)
    o = of_ref[0].astype(F32) + ob_ref[0].astype(F32)
    on = jnp.concatenate([_rms(o[:, hd * HG_D:(hd + 1) * HG_D]) for hd in range(HG_HEADS)], axis=1)
    og = og_ref[0].astype(F32)
    zc = (on * ghg_ref[...] * (og * _gate(og))).astype(BF16)
    bra = _dot_tn(zat_ref[0], wbr_ref[0])
    brb = _dot_tn(zbt, wbr_ref[1])
    brc = _dot(zc, wbr_ref[2])
    gp = gp_ref[0].astype(F32)
    mix = (_gate(gp[:, :D_MODEL]) * bra + _gate(gp[:, D_MODEL:2 * D_MODEL]) * brb
           + _gate(gp[:, 2 * D_MODEL:]) * brc)
    out = _dot(mix.astype(BF16), wout_ref[...])
    o_ref[0] = x_ref[0] + mod_ref[0][2:3] * out


def _merge_call(x, mod, zat, yt, of, ob, hog, gp, w):
    b, L, _ = x.shape
    tl = TL_MG
    tok = lambda n: pl.BlockSpec((1, tl, n), lambda bi, i: (bi, i, 0))
    tr = pl.BlockSpec((1, MIX, tl), lambda bi, i: (bi, 0, i))
    consts = [w["wglut"], w["bglu"], w["ghg"], w["wbr"], w["wout"]]
    return pl.pallas_call(
        _merge_kernel, grid=(b, L // tl),
        in_specs=[tok(D_MODEL), pl.BlockSpec((1, 8, D_MODEL), lambda bi, i: (bi, 0, 0)), tr,
                  pl.BlockSpec((1, S5_GROUPS, tl // S5_CHUNK * S5_GROUP, S5_CHUNK), lambda bi, i: (bi, 0, i, 0)),
                  tok(MIX), tok(MIX), tok(MIX), tok(3 * D_MODEL)] + [_const_spec(a.shape) for a in consts],
        out_specs=tok(D_MODEL),
        out_shape=jax.ShapeDtypeStruct(x.shape, F32),
        compiler_params=_params("parallel", "parallel"), name="merge",
    )(x, mod, zat, yt, of, ob, hog, gp, *consts)


def _ff_chunks():
    out, off = [], 0
    while off < D_FF:
        w = min(FF_CW, D_FF - off)
        out.append((off, w))
        off += w
    return out


def _ffn_kernel(x_ref, xp_ref, xn_ref, mod_ref, gffn_ref, wup_ref, wconv_ref, bconv_ref, wdn_ref, o_ref,
                up_ref, h_ref, acc_ref, *, tl):
    i = pl.program_id(1)
    mod = mod_ref[0]

    def modulate(xv):
        return _rms(xv) * gffn_ref[...] * (1.0 + mod[4:5]) + mod[3:4]

    keep_prev = jnp.where(i > 0, 1.0, 0.0)
    keep_next = jnp.where(i < pl.num_programs(1) - 1, 1.0, 0.0)
    h_ref[...] = jnp.concatenate([modulate(xp_ref[0]) * keep_prev, modulate(x_ref[0]),
                                  modulate(xn_ref[0]) * keep_next], axis=0).astype(BF16)
    chunks = _ff_chunks()

    def up_proj(ci):
        off, cw = chunks[ci]
        for part in range(2):
            col = off + part * D_FF
            up_ref[ci % 2, part, :, 0:cw] = _dot(h_ref[...], wup_ref[:, col:col + cw])

    up_proj(0)
    for ci, (off, cw) in enumerate(chunks):
        if ci + 1 < len(chunks):
            up_proj(ci + 1)
        act = None
        for part in range(2):
            col = off + part * D_FF
            wc = wconv_ref[:, col:col + cw]
            buf = up_ref.at[ci % 2, part]
            y = (buf[pl.ds(HALO - 1, tl), 0:cw] * wc[0:1] + buf[pl.ds(HALO, tl), 0:cw] * wc[1:2]
                 + buf[pl.ds(HALO + 1, tl), 0:cw] * wc[2:3] + bconv_ref[:, col:col + cw])
            act = y * _gate(y) if part == 0 else act * y
        dn = _dot(act.astype(BF16), wdn_ref[off:off + cw, :])
        if ci == 0:
            acc_ref[...] = dn
        else:
            acc_ref[...] += dn
    o_ref[0] = x_ref[0] + mod[5:6] * acc_ref[...]


def _ffn_call(x, mod, w):
    b, L, _ = x.shape
    tl = TL_FF
    hb = tl // HALO
    nh = L // HALO
    consts = [w["gffn"], w["wup"], w["wconv"], w["bconv"], w["wdn"]]
    return pl.pallas_call(
        functools.partial(_ffn_kernel, tl=tl), grid=(b, L // tl),
        in_specs=[pl.BlockSpec((1, tl, D_MODEL), lambda bi, i: (bi, i, 0)),
                  pl.BlockSpec((1, HALO, D_MODEL), lambda bi, i: (bi, jnp.maximum(i * hb - 1, 0), 0)),
                  pl.BlockSpec((1, HALO, D_MODEL), lambda bi, i: (bi, jnp.minimum((i + 1) * hb, nh - 1), 0)),
                  pl.BlockSpec((1, 8, D_MODEL), lambda bi, i: (bi, 0, 0))] + [_const_spec(a.shape) for a in consts],
        out_specs=pl.BlockSpec((1, tl, D_MODEL), lambda bi, i: (bi, i, 0)),
        out_shape=jax.ShapeDtypeStruct(x.shape, F32),
        scratch_shapes=[pltpu.VMEM((2, 2, tl + 2 * HALO, FF_CW), F32), pltpu.VMEM((tl + 2 * HALO, D_MODEL), BF16),
                        pltpu.VMEM((tl, D_MODEL), F32)],
        compiler_params=_params("parallel", "parallel"), name="ffn",
    )(x, x, x, mod, *consts)


def _final_kernel(x_ref, g_ref, o_ref):
    o_ref[0] = _rms(x_ref[0]) * g_ref[...]


def _final_call(x, g):
    b, L, _ = x.shape
    tl = 512
    spec = pl.BlockSpec((1, tl, D_MODEL), lambda bi, i: (bi, i, 0))
    return pl.pallas_call(
        _final_kernel, grid=(b, L // tl), in_specs=[spec, pl.BlockSpec((1, D_MODEL), lambda bi, i: (0, 0))],
        out_specs=spec, out_shape=jax.ShapeDtypeStruct(x.shape, F32),
        compiler_params=_params("parallel", "parallel"), name="final_norm",
    )(x, g)


def _layer_weights(p, l):
    w_in = p["w_in"][l]
    col = lambda i: w_in[:, IN_OFFS[i]:IN_OFFS[i + 1]]
    half = ROPE // 2
    kr = col(2)
    z = lambda n: jnp.zeros((D_MODEL, n), F32)
    wkr = jnp.concatenate([z(NOPE), kr[:, :half], kr[:, half:], z(LANE - NOPE - ROPE),
                           z(NOPE), kr[:, half:], kr[:, :half], z(LANE - NOPE - ROPE)], axis=1)
    wq = p["w_q_up"][l].reshape(Q_LORA, N_HEADS, NOPE + ROPE)
    zq = lambda n: jnp.zeros((Q_LORA, N_HEADS, n), F32)
    x1, x2 = wq[:, :, NOPE:NOPE + half], wq[:, :, NOPE + half:]
    wqa = jnp.concatenate([wq[:, :, :NOPE], x1, x2, zq(LANE - NOPE - ROPE)], axis=2).reshape(Q_LORA, N_HEADS * LANE)
    wqb = jnp.concatenate([zq(NOPE), x2, x1, zq(LANE - NOPE - ROPE)], axis=2).reshape(Q_LORA, N_HEADS * LANE)
    wkv = p["w_kv_up"][l].reshape(KV_LORA, N_HEADS, 2 * NOPE)
    wk = jnp.concatenate([wkv[:, :, :NOPE], jnp.zeros((KV_LORA, N_HEADS, LANE - NOPE), F32)],
                         axis=2).reshape(KV_LORA, N_HEADS * LANE)
    wvt = wkv[:, :, NOPE:].reshape(KV_LORA, MIX).T
    bf = lambda a: a.astype(BF16)
    return dict(
        gmix=p["g_mix"][l][None], gq=p["g_q_lat"][l][None], gkv=p["g_kv_lat"][l][None],
        wlat=bf(jnp.concatenate([col(0), col(1)], axis=1)), wkr=bf(wkr), wut=bf(col(3).T),
        whqv=bf(jnp.concatenate([col(4), col(7)], axis=1)), whog=bf(col(8)),
        whff=bf(jnp.concatenate([col(5), col(6)], axis=1)), wgate=bf(col(9)),
        wqa=bf(wqa), wqb=bf(wqb), wk=bf(wk), wvt=bf(wvt),
        wglut=bf(p["w_glu"][l].T), bglu=p["b_glu"][l][:, None], ghg=p["g_hg_out"][l][None],
        wbr=bf(p["w_branch"][l]), wout=bf(p["w_out"][l]),
        gffn=p["g_ffn"][l][None], wup=bf(p["w_ffn_up"][l]), wconv=p["w_ffn_conv"][l],
        bconv=p["b_ffn_conv"][l][None], wdn=bf(p["w_ffn_down"][l]),
    )


def _rope_tables(L):
    half = ROPE // 2
    inv_freq = 1.0 / (ROPE_BASE ** (jnp.arange(0, ROPE, 2, dtype=F32) / ROPE))
    ang = jnp.arange(L, dtype=F32)[:, None] * inv_freq[None, :]
    cos, sin = jnp.cos(ang), jnp.sin(ang)
    one, zero = jnp.ones((L, NOPE), F32), jnp.zeros((L, NOPE), F32)
    pad = jnp.zeros((L, LANE - NOPE - ROPE), F32)
    return (jnp.concatenate([one, cos, cos, pad], axis=1), jnp.concatenate([zero, -sin, sin, pad], axis=1))


def _trunk(x, mods, weights, s5ops, lb, g_final):
    b, L, _ = x.shape
    cos_t, sin_t = _rope_tables(L)
    for l in range(DEPTH):
        w = weights[l]
        q, k, vt, ut, hqv, hog, hff, gp = _inproj_call(x, mods[l], w, cos_t, sin_t)
        zat = _flash_call(q, k, vt)
        yt = _s5_call(ut, s5ops[l])
        of, ob = _hg_call(hqv, hff, lb[0, l][None], lb[1, l][None])
        x = _merge_call(x, mods[l], zat, yt, of, ob, hog, gp, w)
        x = _ffn_call(x, mods[l], w)
    return _final_call(x, g_final[None])


def kernel(x_prompt, x_sample, c_prompt, c_sample, w_ada, b_ada, g_mix, w_in, g_q_lat, w_q_up, g_kv_lat, w_kv_up,
           s5_lam_re, s5_lam_im, s5_log_dt, s5_b_re, s5_b_im, s5_c_re, s5_c_im, s5_d, w_glu, b_glu, hg_lb_logits,
           g_hg_out, w_branch, w_out, g_ffn, w_ffn_up, w_ffn_conv, b_ffn_conv, w_ffn_down, g_final):
    p = dict(g_mix=g_mix, w_in=w_in, g_q_lat=g_q_lat, w_q_up=w_q_up, g_kv_lat=g_kv_lat, w_kv_up=w_kv_up,
             w_glu=w_glu, b_glu=b_glu, g_hg_out=g_hg_out, w_branch=w_branch, w_out=w_out, g_ffn=g_ffn,
             w_ffn_up=w_ffn_up, w_ffn_conv=w_ffn_conv, b_ffn_conv=b_ffn_conv, w_ffn_down=w_ffn_down)
    depth = w_in.shape[0]
    assert depth == DEPTH
    bp, bs = c_prompt.shape[0], c_sample.shape[0]
    rows = -(-(bp + bs) // 8) * 8
    c_all = jnp.concatenate([c_prompt, c_sample, jnp.zeros((rows - bp - bs, D_MODEL), F32)], axis=0)
    mod_all = _ada_call(c_all, w_ada, b_ada)

    def mods_for(lo, n):
        m = mod_all[:, lo:lo + n].reshape(DEPTH, n, 6, D_MODEL)
        return jnp.pad(m, ((0, 0), (0, 0), (0, 2), (0, 0)))

    weights = [_layer_weights(p, l) for l in range(DEPTH)]
    s5ops = [_s5_operators(s5_lam_re[l], s5_lam_im[l], s5_log_dt[l], s5_b_re[l], s5_b_im[l],
                           s5_c_re[l], s5_c_im[l], s5_d[l]) for l in range(DEPTH)]
    gam = jax.nn.softmax(hg_lb_logits.astype(F32), axis=1)
    lb = jnp.cumsum(gam, axis=1) - gam[:, :1]
    y_prompt = _trunk(x_prompt, mods_for(0, bp), weights, s5ops, lb, g_final)
    y_sample = _trunk(x_sample, mods_for(bp, bs), weights, s5ops, lb, g_final)
    return (y_prompt, y_sample)
```

```python
import functools
import math

import jax
import jax.numpy as jnp
from jax import lax
from jax.experimental import pallas as pl
from jax.experimental.pallas import tpu as pltpu

F32 = jnp.float32
BF16 = jnp.bfloat16

D_MODEL = 1024
DEPTH = 4
MIX = 512
N_HEADS = 8
NOPE = 64
ROPE = 32
Q_LORA = 384
KV_LORA = 256
ROPE_BASE = 10000.0
S5_GROUPS = 32
S5_GROUP = 16
S5_STATE = 64
S5_CHUNK = 128
HG_HEADS = 4
HG_D = 128
HG_CHUNK = 64
HG_UNROLL = 4
D_FF = 2816
EPS = 1e-6
ATTN_SCALE = 1.0 / math.sqrt(NOPE + ROPE)
LOG2E = 1.4426950408889634
IN_OFFS = (0, 384, 640, 672, 1184, 1696, 2208, 2720, 3232, 3744, 6816)

LANE = 128
VMEM_LIMIT = 56 * 1024 * 1024

TL_IN = 512
TQ = 512
TK = 512
FLASH_UNROLL = 8
TL_HG = 512
TL_MG = 512
TL_FF = 512
FF_CW = 768
HALO = 8


def _dot(a, b):
    return jnp.dot(a, b, preferred_element_type=F32)


def _dot_nt(a, b):
    return lax.dot_general(a, b, (((1,), (1,)), ((), ())), preferred_element_type=F32)


def _dot_tn(a, b):
    return lax.dot_general(a, b, (((0,), (0,)), ((), ())), preferred_element_type=F32)


def _sigmoid(x):
    return 1.0 / (1.0 + jnp.exp(-x))


def _gate(x):
    return 0.5 * jnp.tanh(0.5 * x) + 0.5


def _rms(x):
    return x * lax.rsqrt(jnp.mean(x * x, axis=-1, keepdims=True) + EPS)


def _gelu_tanh(x):
    return 0.5 * x * (1.0 + jnp.tanh(math.sqrt(2.0 / math.pi) * (x + 0.044715 * (x * x * x))))


def _params(*sem):
    return pltpu.CompilerParams(dimension_semantics=sem, vmem_limit_bytes=VMEM_LIMIT)


def _const_spec(shape):
    nd = len(shape)
    return pl.BlockSpec(shape, lambda *_: (0,) * nd, pipeline_mode=pl.Buffered(1))


def _ada_kernel(c_ref, w_ref, b_ref, o_ref):
    c = c_ref[...]
    a = (c * _sigmoid(c)).astype(BF16)
    o_ref[0] = _dot(a, w_ref[0].astype(BF16)) + b_ref[0]


def _ada_call(c_all, w_ada, b_ada):
    nb = 1536
    rows = c_all.shape[0]
    return pl.pallas_call(
        _ada_kernel,
        grid=(DEPTH, 6 * D_MODEL // nb),
        in_specs=[pl.BlockSpec((rows, D_MODEL), lambda l, j: (0, 0)),
                  pl.BlockSpec((1, D_MODEL, nb), lambda l, j: (l, 0, j)),
                  pl.BlockSpec((1, 1, nb), lambda l, j: (l, 0, j))],
        out_specs=pl.BlockSpec((1, rows, nb), lambda l, j: (l, 0, j)),
        out_shape=jax.ShapeDtypeStruct((DEPTH, rows, 6 * D_MODEL), F32),
        compiler_params=_params("arbitrary", "arbitrary"),
        name="ada_mod",
    )(c_all, w_ada, b_ada.reshape(DEPTH, 1, 6 * D_MODEL))


def _inproj_kernel(x_ref, mod_ref, gmix_ref, cos_ref, sin_ref,
                   wlat_ref, wkr_ref, wut_ref, whqv_ref, whog_ref, whff_ref, wgate_ref,
                   gq_ref, gkv_ref, wqa_ref, wk_ref, wvt_ref,
                   q_ref, k_ref, vt_ref, ut_ref, hqv_ref, hog_ref, hff_ref, gp_ref):
    mod = mod_ref[0]
    h = _rms(x_ref[0]) * gmix_ref[...] * (1.0 + mod[1:2]) + mod[0:1]
    hb = h.astype(BF16)
    lat = _dot(hb, wlat_ref[...])
    qn = (_rms(lat[:, :Q_LORA]) * gq_ref[...]).astype(BF16)
    kvn = (_rms(lat[:, Q_LORA:]) * gkv_ref[...]).astype(BF16)
    cos = cos_ref[...]
    sin = sin_ref[...]
    lane = lax.broadcasted_iota(jnp.int32, sin.shape, 1)
    sin1 = jnp.where(lane < NOPE + ROPE // 2, sin, 0.0)
    sin2 = sin - sin1
    half = ROPE // 2

    def rotate(t, sl):
        down = pltpu.roll(t, t.shape[1] - half, 1)
        up = pltpu.roll(t, half, 1)
        return t[:, sl] * cos + down[:, sl] * sin1 + up[:, sl] * sin2

    qa = _dot(qn, wqa_ref[...])
    krp = _dot(hb, wkr_ref[...])
    kr = rotate(krp, slice(0, LANE))
    kn = _dot(kvn, wk_ref[...])
    qdn = pltpu.roll(qa, qa.shape[1] - half, 1)
    qup = pltpu.roll(qa, half, 1)
    for hd in range(N_HEADS):
        sl = slice(hd * LANE, (hd + 1) * LANE)
        q_ref[0, :, sl] = ((qa[:, sl] * cos + qdn[:, sl] * sin1 + qup[:, sl] * sin2)
                           * (ATTN_SCALE * LOG2E)).astype(BF16)
        k_ref[0, :, sl] = (kn[:, sl] + kr).astype(BF16)
    vt_ref[0] = _dot_nt(wvt_ref[...], kvn).astype(BF16)
    ut = _dot_nt(wut_ref[...], hb)
    for ch in range(ut.shape[1] // S5_CHUNK):
        ut_ref[0, :, ch * S5_GROUP:(ch + 1) * S5_GROUP, :] = (
            ut[:, ch * S5_CHUNK:(ch + 1) * S5_CHUNK].reshape(S5_GROUPS, S5_GROUP, S5_CHUNK))
    hqv_ref[0] = _dot(hb, whqv_ref[...]).astype(BF16)
    hog_ref[0] = _dot(hb, whog_ref[...]).astype(BF16)
    hff_ref[0] = _dot(hb, whff_ref[...])
    gp_ref[0] = _dot(hb, wgate_ref[...]).astype(BF16)


def _inproj_call(x, mod, w, cos_t, sin_t):
    b, L, _ = x.shape
    tl = TL_IN
    tok = lambda n: pl.BlockSpec((1, tl, n), lambda bi, i: (bi, i, 0))
    tr = lambda n: pl.BlockSpec((1, n, tl), lambda bi, i: (bi, 0, i))
    weights = [w["gmix"], None, None, w["wlat"], w["wkr"], w["wut"], w["whqv"], w["whog"], w["whff"], w["wgate"],
               w["gq"], w["gkv"], w["wqa"], w["wk"], w["wvt"]]
    in_specs = [tok(D_MODEL), pl.BlockSpec((1, 8, D_MODEL), lambda bi, i: (bi, 0, 0))]
    args = [x, mod]
    for a in weights:
        if a is None:
            continue
        in_specs.append(_const_spec(a.shape))
        args.append(a)
    in_specs[3:3] = [pl.BlockSpec((tl, LANE), lambda bi, i: (i, 0))] * 2
    args[3:3] = [cos_t, sin_t]
    out_shape = [jax.ShapeDtypeStruct((b, L, 1024), BF16), jax.ShapeDtypeStruct((b, L, 1024), BF16),
                 jax.ShapeDtypeStruct((b, MIX, L), BF16),
                 jax.ShapeDtypeStruct((b, S5_GROUPS, L // S5_CHUNK * S5_GROUP, S5_CHUNK), F32),
                 jax.ShapeDtypeStruct((b, L, 1024), BF16), jax.ShapeDtypeStruct((b, L, MIX), BF16),
                 jax.ShapeDtypeStruct((b, L, 1024), F32), jax.ShapeDtypeStruct((b, L, 3 * D_MODEL), BF16)]
    grp = pl.BlockSpec((1, S5_GROUPS, tl // S5_CHUNK * S5_GROUP, S5_CHUNK), lambda bi, i: (bi, 0, i, 0))
    out_specs = [tok(1024), tok(1024), tr(MIX), grp, tok(1024), tok(MIX), tok(1024), tok(3 * D_MODEL)]
    return pl.pallas_call(
        _inproj_kernel, grid=(b, L // tl), in_specs=in_specs, out_specs=out_specs, out_shape=out_shape,
        compiler_params=_params("parallel", "parallel"), name="inproj",
    )(*args)


def _flash_kernel(q_ref, k_ref, vt_ref, o_ref, s_ref, *, tk, nk):
    tq = q_ref.shape[1]
    half = NOPE
    ones = jnp.ones((16, tk), BF16)

    def scores(buf, j):
        off = pl.multiple_of(j * tk, tk)
        for hh in range(2):
            s_ref[buf, hh] = _dot_nt(k_ref[0, pl.ds(off, tk), hh * LANE:(hh + 1) * LANE],
                                     q_ref[0, :, hh * LANE:(hh + 1) * LANE])

    def process(buf, j, carry):
        off = pl.multiple_of(j * tk, tk)
        new = []
        for hh in range(2):
            m, acc = carry[hh]
            s = s_ref[buf, hh]
            mn = jnp.maximum(m, jnp.max(s, axis=0, keepdims=True))
            p = jnp.exp2(s - mn).astype(BF16)
            a = jnp.exp2(m - mn)
            v = jnp.concatenate([vt_ref[0, hh * half:(hh + 1) * half, pl.ds(off, tk)], ones], axis=0)
            new.append((mn, a * acc + _dot(v, p)))
        return tuple(new)

    def body(i, carry):
        for u in range(FLASH_UNROLL):
            j = i * FLASH_UNROLL + u
            scores(1 - u % 2, j + 1)
            carry = process(u % 2, j, carry)
        return carry

    scores(0, 0)
    init = tuple((jnp.full((1, tq), -1e30, F32), jnp.zeros((half + 16, tq), F32)) for _ in range(2))
    n_loop = (nk - 1) // FLASH_UNROLL
    res = lax.fori_loop(0, n_loop, body, init)
    for j in range(n_loop * FLASH_UNROLL, nk):
        if j + 1 < nk:
            scores(1 - j % 2, j + 1)
        res = process(j % 2, j, res)
    for hh in range(2):
        acc = res[hh][1]
        o_ref[0, hh * half:(hh + 1) * half, :] = (acc[:half] / acc[half:half + 1]).astype(BF16)


def _flash_call(q, k, vt):
    b, L, _ = q.shape
    tq, tk = min(TQ, L), min(TK, L)
    return pl.pallas_call(
        functools.partial(_flash_kernel, tk=tk, nk=L // tk),
        scratch_shapes=[pltpu.VMEM((2, 2, tk, tq), F32)],
        grid=(b, N_HEADS // 2, L // tq),
        in_specs=[pl.BlockSpec((1, tq, 2 * LANE), lambda bi, p, i: (bi, i, p)),
                  pl.BlockSpec((1, L, 2 * LANE), lambda bi, p, i: (bi, 0, p)),
                  pl.BlockSpec((1, LANE, L), lambda bi, p, i: (bi, p, 0))],
        out_specs=pl.BlockSpec((1, LANE, tq), lambda bi, p, i: (bi, p, i)),
        out_shape=jax.ShapeDtypeStruct((b, MIX, L), BF16),
        compiler_params=_params("parallel", "parallel", "arbitrary"), name="flash",
    )(q, k, vt)


def _s5_kernel(u_ref, t_ref, f_ref, e_ref, a_ref, y_ref, sloc_ref, st_ref, *, nb, nct):
    m = nb * nct
    ucat = jnp.concatenate(
        [jnp.concatenate([u_ref[bi, 0, pl.ds(hi, nct, stride=S5_GROUP), :].astype(BF16) for hi in range(S5_GROUP)],
                         axis=1) for bi in range(nb)], axis=0)
    y = _dot(ucat, t_ref[0])
    sloc_ref[...] = _dot(ucat, f_ref[0])
    a = a_ref[0]
    arf, aif, arb, aib = a[0:1], a[1:2], a[2:3], a[3:4]

    sb = 8 if nct % 8 == 0 else nct
    nblk = nct // sb

    def body(blk, carry):
        new = []
        for bi in range(nb):
            xrf, xif, xrb, xib = carry[bi]
            rf = pl.multiple_of(bi * nct + blk * sb, sb)
            rb = pl.multiple_of(bi * nct + (nblk - 1 - blk) * sb, sb)
            lf = sloc_ref[pl.ds(rf, sb), 0:2 * LANE]
            lb = sloc_ref[pl.ds(rb, sb), 2 * LANE:4 * LANE]
            frows, brows = [], []
            for r in range(sb):
                frows.append((xrf, xif))
                xrf, xif = (arf * xrf - aif * xif + lf[r:r + 1, 0:LANE],
                            arf * xif + aif * xrf + lf[r:r + 1, LANE:2 * LANE])
            for r in range(sb - 1, -1, -1):
                brows.append((xrb, xib))
                xrb, xib = (arb * xrb - aib * xib + lb[r:r + 1, 0:LANE],
                            arb * xib + aib * xrb + lb[r:r + 1, LANE:2 * LANE])
            brows = brows[::-1]
            st_ref[pl.ds(rf, sb), 0:LANE] = jnp.concatenate([t[0] for t in frows], axis=0)
            st_ref[pl.ds(rf, sb), LANE:2 * LANE] = jnp.concatenate([t[1] for t in frows], axis=0)
            st_ref[pl.ds(rb, sb), 2 * LANE:3 * LANE] = jnp.concatenate([t[0] for t in brows], axis=0)
            st_ref[pl.ds(rb, sb), 3 * LANE:4 * LANE] = jnp.concatenate([t[1] for t in brows], axis=0)
            new.append((xrf, xif, xrb, xib))
        return tuple(new)

    z = jnp.zeros((1, LANE), F32)
    lax.fori_loop(0, nblk, body, tuple((z, z, z, z) for _ in range(nb)))
    y = y + _dot(st_ref[...].astype(BF16), e_ref[0])
    for ho in range(S5_GROUP):
        for bi in range(nb):
            y_ref[bi, 0, pl.ds(ho, nct, stride=S5_GROUP), :] = y[bi * nct:(bi + 1) * nct,
                                                                 ho * S5_CHUNK:(ho + 1) * S5_CHUNK]


def _s5_call(u4, ops):
    b, _, rows, _ = u4.shape
    nct = rows // S5_GROUP
    gw = S5_GROUP * S5_CHUNK
    blk = pl.BlockSpec((b, 1, rows, S5_CHUNK), lambda g: (0, g, 0, 0))
    return pl.pallas_call(
        functools.partial(_s5_kernel, nb=b, nct=nct),
        grid=(S5_GROUPS,),
        in_specs=[blk,
                  pl.BlockSpec((1, gw, gw), lambda g: (g, 0, 0)),
                  pl.BlockSpec((1, gw, 4 * LANE), lambda g: (g, 0, 0)),
                  pl.BlockSpec((1, 4 * LANE, gw), lambda g: (g, 0, 0)),
                  pl.BlockSpec((1, 8, LANE), lambda g: (g, 0, 0))],
        out_specs=blk,
        out_shape=jax.ShapeDtypeStruct(u4.shape, F32),
        scratch_shapes=[pltpu.VMEM((b * nct, 4 * LANE), F32), pltpu.VMEM((b * nct, 4 * LANE), F32)],
        compiler_params=_params("parallel"), name="s5",
    )(u4, ops["t"], ops["f"], ops["e"], ops["a"])


def _toeplitz_kernel(w_ref, t_ref):
    c = S5_CHUNK

    def body(hi, _):
        r0 = pl.multiple_of(hi * c, c)
        for ho in range(S5_GROUP):
            row = w_ref[0, hi, ho:ho + 1, :]
            skew = pltpu.roll(jnp.broadcast_to(row, (c, 2 * c)), 0, 1, stride=1, stride_axis=0)
            t_ref[0, pl.ds(r0, c), ho * c:(ho + 1) * c] = skew[:, :c].astype(BF16)
        return 0

    lax.fori_loop(0, S5_GROUP, body, 0)


def _toeplitz_call(w):
    g = w.shape[0]
    gw = S5_GROUP * S5_CHUNK
    return pl.pallas_call(
        _toeplitz_kernel, grid=(g,),
        in_specs=[pl.BlockSpec((1, S5_GROUP, S5_GROUP, 2 * S5_CHUNK), lambda i: (i, 0, 0, 0))],
        out_specs=pl.BlockSpec((1, gw, gw), lambda i: (i, 0, 0)),
        out_shape=jax.ShapeDtypeStruct((g, gw, gw), BF16),
        compiler_params=_params("parallel"), name="s5_toeplitz",
    )(w)


def _s5_operators(lam_re, lam_im, log_dt, b_re, b_im, c_re, c_im, d):
    hp = lax.Precision.HIGHEST
    G, P, GS, C = S5_GROUPS, S5_STATE, S5_GROUP, S5_CHUNK
    lam_re, lam_im = lam_re.astype(F32), lam_im.astype(F32)
    dt = jnp.exp(log_dt.astype(F32))[..., None]
    zr, zi = lam_re * dt, lam_im * dt
    kk = jnp.arange(C + 1, dtype=F32)[None, None, :, None]
    mag = jnp.exp(zr[:, :, None, :] * kk)
    ang = zi[:, :, None, :] * kk
    pw_re, pw_im = mag * jnp.cos(ang), mag * jnp.sin(ang)
    lb_re, lb_im = pw_re[:, :, 1], pw_im[:, :, 1]
    den = lam_re * lam_re + lam_im * lam_im
    nr, ni = lb_re - 1.0, lb_im
    cr = (nr * lam_re + ni * lam_im) / den
    ci = (ni * lam_re - nr * lam_im) / den
    bb_re = cr[..., None] * b_re - ci[..., None] * b_im
    bb_im = cr[..., None] * b_im + ci[..., None] * b_re
    c_re, c_im = c_re.astype(F32), c_im.astype(F32)

    def kern(dr):
        pr, pi_ = pw_re[dr, :, :C], pw_im[dr, :, :C]
        cp_re = c_re[dr][:, None] * pr[:, :, None, :] - c_im[dr][:, None] * pi_[:, :, None, :]
        cp_im = c_re[dr][:, None] * pi_[:, :, None, :] + c_im[dr][:, None] * pr[:, :, None, :]
        return (jnp.einsum('gkhp,gpi->gkhi', cp_re, bb_re[dr], precision=hp)
                - jnp.einsum('gkhp,gpi->gkhi', cp_im, bb_im[dr], precision=hp))

    kf, kb = kern(0), kern(1)
    k0 = kf[:, 0] + kb[:, 0] + jnp.eye(GS, dtype=F32)[None] * d.astype(F32).reshape(G, GS)[:, :, None]
    kfull = jnp.concatenate([kb[:, 1:][:, ::-1], k0[:, None], kf[:, 1:], jnp.zeros((G, 1, GS, GS), F32)], axis=1)
    t_op = _toeplitz_call(jnp.roll(kfull, -(C - 1), axis=1).transpose(0, 3, 2, 1))

    def f_part(dr, idx):
        pr, pi_ = pw_re[dr][:, idx], pw_im[dr][:, idx]
        br, bi = bb_re[dr].transpose(0, 2, 1), bb_im[dr].transpose(0, 2, 1)
        re = pr[:, None] * br[:, :, None] - pi_[:, None] * bi[:, :, None]
        im = pr[:, None] * bi[:, :, None] + pi_[:, None] * br[:, :, None]
        return re.reshape(G, GS * C, P), im.reshape(G, GS * C, P)

    tau = jnp.arange(C)
    ffr, ffi = f_part(0, C - 1 - tau)
    fbr, fbi = f_part(1, tau)
    padl = lambda a: jnp.pad(a, ((0, 0), (0, 0), (0, LANE - P)))
    f_op = jnp.concatenate([padl(ffr), padl(ffi), padl(fbr), padl(fbi)], axis=-1).astype(BF16)

    def e_part(dr, idx):
        pr, pi_ = pw_re[dr][:, idx], pw_im[dr][:, idx]
        cr_, ci_ = c_re[dr].transpose(0, 2, 1), c_im[dr].transpose(0, 2, 1)
        prt, pit = pr.transpose(0, 2, 1), pi_.transpose(0, 2, 1)
        re = cr_[:, :, :, None] * prt[:, :, None, :] - ci_[:, :, :, None] * pit[:, :, None, :]
        im = cr_[:, :, :, None] * pit[:, :, None, :] + ci_[:, :, :, None] * prt[:, :, None, :]
        return re.reshape(G, P, GS * C), -im.reshape(G, P, GS * C)

    efr, efi = e_part(0, tau + 1)
    ebr, ebi = e_part(1, C - tau)
    padr = lambda a: jnp.pad(a, ((0, 0), (0, LANE - P), (0, 0)))
    e_op = jnp.concatenate([padr(efr), padr(efi), padr(ebr), padr(ebi)], axis=1).astype(BF16)

    arows = [pw_re[0, :, C], pw_im[0, :, C], pw_re[1, :, C], pw_im[1, :, C]]
    a_op = jnp.stack([jnp.pad(r, ((0, 0), (0, LANE - P))) for r in arows]
                     + [jnp.zeros((G, LANE), F32)] * 4, axis=1)
    return dict(t=t_op, f=f_op, e=e_op, a=a_op)


def _split3(x):
    hi = x.astype(BF16)
    r1 = x - hi.astype(F32)
    mid = r1.astype(BF16)
    lo = (r1 - mid.astype(F32)).astype(BF16)
    return hi, mid, lo


HG_HALVES = (32, 16, 8, 4, 2, 1)
HG_SEL_HALVES = (8, 4, 2)


def _hg_later(idx, h, fwd):
    return (idx % (2 * h) >= h) if fwd else (idx % (2 * h) < h)


def _hg_anchor(idx, h, fwd):
    base = idx // (2 * h) * (2 * h)
    return base + h if fwd else base + h - 1


def _hg_consts(fwd):
    c = HG_CHUNK
    t = lax.broadcasted_iota(jnp.int32, (c, c), 0)
    s = lax.broadcasted_iota(jnp.int32, (c, c), 1)
    tri = jnp.where((s <= t) if fwd else (s >= t), 1.0, 0.0).astype(BF16)
    masks = {h: (t // (2 * h) == s // (2 * h)) & _hg_later(t, h, fwd) & ~_hg_later(s, h, fwd) for h in HG_HALVES}
    rows = lax.broadcasted_iota(jnp.int32, (c, HG_D), 0)
    later = {h: _hg_later(rows, h, fwd) for h in HG_HALVES}
    sign = {h: jnp.where(later[h], 1.0, -1.0) for h in HG_HALVES}
    sr = lax.broadcasted_iota(jnp.int32, (len(HG_SEL_HALVES) * c, c), 0)
    sc = lax.broadcasted_iota(jnp.int32, (len(HG_SEL_HALVES) * c, c), 1)
    hit = jnp.zeros(sr.shape, jnp.bool_)
    for i, h in enumerate(HG_SEL_HALVES):
        hit = hit | ((sr // c == i) & (sc == _hg_anchor(sr % c, h, fwd)))
    sel = jnp.where(hit, 1.0, 0.0).astype(BF16)
    return dict(tri=tri, masks=masks, diag=(t == s), later=later, sign=sign, sel=sel)


def _hg_chunks(streams):
    c = HG_CHUNK

    gates = []
    for hq, hv, hf, lb, st, fwd, consts in streams:
        q = hq * _gate(hq)
        f = lb + (1.0 - lb) * _sigmoid(hf)
        logf = jnp.log(f)
        p1, p2, p3 = _split3(logf)
        tri = consts["tri"]
        cum = _dot(tri, p1) + _dot(tri, p2) + _dot(tri, p3)
        gates.append((q, 1.0 - f, f, logf, cum))

    refs = []
    for (hq, hv, hf, lb, st, fwd, consts), (q, k, f, logf, cum) in zip(streams, gates):
        cexc = cum - logf
        ref = {}
        for h in HG_HALVES:
            if 2 * h >= 16 and h not in HG_SEL_HALVES:
                ref[h] = jnp.concatenate(
                    [jnp.broadcast_to(cexc[a:a + 1, :], (2 * h, HG_D))
                     for a in (_hg_anchor(b0, h, fwd) for b0 in range(0, c, 2 * h))], axis=0)
        c_hi = cexc.astype(BF16)
        c_mid = (cexc - c_hi.astype(F32)).astype(BF16)
        gathered = _dot(consts["sel"], jnp.concatenate([c_hi, c_mid], axis=1))
        for i, h in enumerate(HG_SEL_HALVES):
            ref[h] = gathered[i * c:(i + 1) * c, :HG_D] + gathered[i * c:(i + 1) * c, HG_D:]
        refs.append(ref)

    scores = []
    for (hq, hv, hf, lb, st, fwd, consts), (q, k, f, logf, cum), ref in zip(streams, gates, refs):
        a = jnp.where(consts["diag"], _dot_nt(q.astype(BF16), k.astype(BF16)), 0.0)
        qf = q * f
        for h in HG_HALVES:
            late = consts["later"][h]
            if h == 1:
                y = jnp.where(late, qf, k)
            else:
                y = jnp.where(late, q, k) * jnp.exp((cum - ref[h]) * consts["sign"][h])
            y = y.astype(BF16)
            a = jnp.where(consts["masks"][h], _dot_nt(y, y), a)
        scores.append(a)

    outs = []
    for (hq, hv, hf, lb, st, fwd, consts), (q, k, f, logf, cum), a in zip(streams, gates, scores):
        if isinstance(st, int):
            st = outs[st][1]
        vb = hv.astype(BF16)
        o = _dot(a.astype(BF16), vb) + _dot_nt((q * jnp.exp(cum)).astype(BF16), st.astype(BF16))
        last = cum[c - 1:c] if fwd else cum[0:1]
        kst = (k * jnp.exp(last - cum)).astype(BF16)
        outs.append((o, st * jnp.exp(last) + _dot_tn(vb, kst)))
    return outs


def _hg_kernel(qvf_ref, ff_ref, qvb_ref, fb_ref, lbf_ref, lbb_ref, of_ref, ob_ref, sf_ref, sb_ref, *, ncc):
    @pl.when(pl.program_id(1) == 0)
    def _():
        sf_ref[...] = jnp.zeros_like(sf_ref)
        sb_ref[...] = jnp.zeros_like(sb_ref)

    cf = _hg_consts(True)
    cb = _hg_consts(False)
    unroll = HG_UNROLL if ncc % HG_UNROLL == 0 else 1
    per = 2 * HG_HEADS

    def body(it, _):
        streams, rows = [], []
        for u in range(unroll):
            cc = it * unroll + u
            rf = pl.multiple_of(cc * HG_CHUNK, HG_CHUNK)
            rb = pl.multiple_of((ncc - 1 - cc) * HG_CHUNK, HG_CHUNK)
            rows.append((rf, rb))
            for hd in range(HG_HEADS):
                sl = slice(hd * HG_D, (hd + 1) * HG_D)
                sv = slice(MIX + hd * HG_D, MIX + (hd + 1) * HG_D)
                prev = len(streams) - per
                streams.append((qvf_ref[0, pl.ds(rf, HG_CHUNK), sl].astype(F32),
                                qvf_ref[0, pl.ds(rf, HG_CHUNK), sv].astype(F32),
                                ff_ref[0, pl.ds(rf, HG_CHUNK), sl], lbf_ref[:, sl],
                                sf_ref[hd] if u == 0 else prev, True, cf))
                prev = len(streams) - per
                streams.append((qvb_ref[0, pl.ds(rb, HG_CHUNK), sl].astype(F32),
                                qvb_ref[0, pl.ds(rb, HG_CHUNK), sv].astype(F32),
                                fb_ref[0, pl.ds(rb, HG_CHUNK), sl], lbb_ref[:, sl],
                                sb_ref[hd] if u == 0 else prev, False, cb))
        outs = _hg_chunks(streams)
        for u, (rf, rb) in enumerate(rows):
            for hd in range(HG_HEADS):
                sl = slice(hd * HG_D, (hd + 1) * HG_D)
                of_ref[0, pl.ds(rf, HG_CHUNK), sl] = outs[u * per + 2 * hd][0].astype(BF16)
                ob_ref[0, pl.ds(rb, HG_CHUNK), sl] = outs[u * per + 2 * hd + 1][0].astype(BF16)
        for hd in range(HG_HEADS):
            sf_ref[hd] = outs[(unroll - 1) * per + 2 * hd][1]
            sb_ref[hd] = outs[(unroll - 1) * per + 2 * hd + 1][1]
        return 0

    lax.fori_loop(0, ncc // unroll, body, 0)


def _hg_call(hqv, hff, lbf, lbb):
    b, L, _ = hqv.shape
    tl = min(TL_HG, L)
    n = L // tl
    fwd = lambda w: pl.BlockSpec((1, tl, w), lambda bi, i: (bi, i, 0))
    bwd = lambda w, j: pl.BlockSpec((1, tl, w), lambda bi, i: (bi, n - 1 - i, j))
    return pl.pallas_call(
        functools.partial(_hg_kernel, ncc=tl // HG_CHUNK),
        grid=(b, n),
        in_specs=[fwd(2 * MIX), fwd(MIX), bwd(2 * MIX, 0), bwd(MIX, 1),
                  pl.BlockSpec((1, MIX), lambda bi, i: (0, 0)), pl.BlockSpec((1, MIX), lambda bi, i: (0, 0))],
        out_specs=[fwd(MIX), bwd(MIX, 0)],
        out_shape=[jax.ShapeDtypeStruct((b, L, MIX), BF16)] * 2,
        scratch_shapes=[pltpu.VMEM((HG_HEADS, HG_D, HG_D), F32), pltpu.VMEM((HG_HEADS, HG_D, HG_D), F32)],
        compiler_params=_params("parallel", "arbitrary"), name="hgrn2",
    )(hqv, hff, hqv, hff, lbf, lbb)


def _merge_kernel(x_ref, mod_ref, zat_ref, yt_ref, of_ref, ob_ref, og_ref, gp_ref,
                  wglut_ref, bglu_ref, ghg_ref, wbr_ref, wout_ref, o_ref):
    nch = yt_ref.shape[2] // S5_GROUP
    yt = jnp.concatenate([yt_ref[0, :, ch * S5_GROUP:(ch + 1) * S5_GROUP, :].reshape(MIX, S5_CHUNK)
                          for ch in range(nch)], axis=1)
    g = _gelu_tanh(yt)
    glu = _dot(wglut_ref[...], g.astype(BF16)) + bglu_ref[...]
    zbt = (g * _gate(glu)).astype(BF16)
    o = of_ref[0].astype(F32) + ob_ref[0].astype(F32)
    on = jnp.concatenate([_rms(o[:, hd * HG_D:(hd + 1) * HG_D]) for hd in range(HG_HEADS)], axis=1)
    og = og_ref[0].astype(F32)
    zc = (on * ghg_ref[...] * (og * _gate(og))).astype(BF16)
    bra = _dot_tn(zat_ref[0], wbr_ref[0])
    brb = _dot_tn(zbt, wbr_ref[1])
    brc = _dot(zc, wbr_ref[2])
    gp = gp_ref[0].astype(F32)
    mix = (_gate(gp[:, :D_MODEL]) * bra + _gate(gp[:, D_MODEL:2 * D_MODEL]) * brb
           + _gate(gp[:, 2 * D_MODEL:]) * brc)
    out = _dot(mix.astype(BF16), wout_ref[...])
    o_ref[0] = x_ref[0] + mod_ref[0][2:3] * out


def _merge_call(x, mod, zat, yt, of, ob, hog, gp, w):
    b, L, _ = x.shape
    tl = TL_MG
    tok = lambda n: pl.BlockSpec((1, tl, n), lambda bi, i: (bi, i, 0))
    tr = pl.BlockSpec((1, MIX, tl), lambda bi, i: (bi, 0, i))
    consts = [w["wglut"], w["bglu"], w["ghg"], w["wbr"], w["wout"]]
    return pl.pallas_call(
        _merge_kernel, grid=(b, L // tl),
        in_specs=[tok(D_MODEL), pl.BlockSpec((1, 8, D_MODEL), lambda bi, i: (bi, 0, 0)), tr,
                  pl.BlockSpec((1, S5_GROUPS, tl // S5_CHUNK * S5_GROUP, S5_CHUNK), lambda bi, i: (bi, 0, i, 0)),
                  tok(MIX), tok(MIX), tok(MIX), tok(3 * D_MODEL)] + [_const_spec(a.shape) for a in consts],
        out_specs=tok(D_MODEL),
        out_shape=jax.ShapeDtypeStruct(x.shape, F32),
        compiler_params=_params("parallel", "parallel"), name="merge",
    )(x, mod, zat, yt, of, ob, hog, gp, *consts)


def _ff_chunks():
    out, off = [], 0
    while off < D_FF:
        w = min(FF_CW, D_FF - off)
        out.append((off, w))
        off += w
    return out


def _ffn_kernel(x_ref, xp_ref, xn_ref, mod_ref, gffn_ref, wup_ref, wconv_ref, bconv_ref, wdn_ref, o_ref,
                up_ref, h_ref, acc_ref, *, tl):
    i = pl.program_id(1)
    mod = mod_ref[0]

    def modulate(xv):
        return _rms(xv) * gffn_ref[...] * (1.0 + mod[4:5]) + mod[3:4]

    keep_prev = jnp.where(i > 0, 1.0, 0.0)
    keep_next = jnp.where(i < pl.num_programs(1) - 1, 1.0, 0.0)
    h_ref[...] = jnp.concatenate([modulate(xp_ref[0]) * keep_prev, modulate(x_ref[0]),
                                  modulate(xn_ref[0]) * keep_next], axis=0).astype(BF16)
    chunks = _ff_chunks()

    def up_proj(ci):
        off, cw = chunks[ci]
        for part in range(2):
            col = off + part * D_FF
            up_ref[ci % 2, part, :, 0:cw] = _dot(h_ref[...], wup_ref[:, col:col + cw])

    up_proj(0)
    for ci, (off, cw) in enumerate(chunks):
        if ci + 1 < len(chunks):
            up_proj(ci + 1)
        act = None
        for part in range(2):
            col = off + part * D_FF
            wc = wconv_ref[:, col:col + cw]
            buf = up_ref.at[ci % 2, part]
            y = (buf[pl.ds(HALO - 1, tl), 0:cw] * wc[0:1] + buf[pl.ds(HALO, tl), 0:cw] * wc[1:2]
                 + buf[pl.ds(HALO + 1, tl), 0:cw] * wc[2:3] + bconv_ref[:, col:col + cw])
            act = y * _gate(y) if part == 0 else act * y
        dn = _dot(act.astype(BF16), wdn_ref[off:off + cw, :])
        if ci == 0:
            acc_ref[...] = dn
        else:
            acc_ref[...] += dn
    o_ref[0] = x_ref[0] + mod[5:6] * acc_ref[...]


def _ffn_call(x, mod, w):
    b, L, _ = x.shape
    tl = TL_FF
    hb = tl // HALO
    nh = L // HALO
    consts = [w["gffn"], w["wup"], w["wconv"], w["bconv"], w["wdn"]]
    return pl.pallas_call(
        functools.partial(_ffn_kernel, tl=tl), grid=(b, L // tl),
        in_specs=[pl.BlockSpec((1, tl, D_MODEL), lambda bi, i: (bi, i, 0)),
                  pl.BlockSpec((1, HALO, D_MODEL), lambda bi, i: (bi, jnp.maximum(i * hb - 1, 0), 0)),
                  pl.BlockSpec((1, HALO, D_MODEL), lambda bi, i: (bi, jnp.minimum((i + 1) * hb, nh - 1), 0)),
                  pl.BlockSpec((1, 8, D_MODEL), lambda bi, i: (bi, 0, 0))] + [_const_spec(a.shape) for a in consts],
        out_specs=pl.BlockSpec((1, tl, D_MODEL), lambda bi, i: (bi, i, 0)),
        out_shape=jax.ShapeDtypeStruct(x.shape, F32),
        scratch_shapes=[pltpu.VMEM((2, 2, tl + 2 * HALO, FF_CW), F32), pltpu.VMEM((tl + 2 * HALO, D_MODEL), BF16),
                        pltpu.VMEM((tl, D_MODEL), F32)],
        compiler_params=_params("parallel", "parallel"), name="ffn",
    )(x, x, x, mod, *consts)


def _final_kernel(x_ref, g_ref, o_ref):
    o_ref[0] = _rms(x_ref[0]) * g_ref[...]


def _final_call(x, g):
    b, L, _ = x.shape
    tl = 512
    spec = pl.BlockSpec((1, tl, D_MODEL), lambda bi, i: (bi, i, 0))
    return pl.pallas_call(
        _final_kernel, grid=(b, L // tl), in_specs=[spec, pl.BlockSpec((1, D_MODEL), lambda bi, i: (0, 0))],
        out_specs=spec, out_shape=jax.ShapeDtypeStruct(x.shape, F32),
        compiler_params=_params("parallel", "parallel"), name="final_norm",
    )(x, g)


def _layer_weights(p, l):
    w_in = p["w_in"][l]
    col = lambda i: w_in[:, IN_OFFS[i]:IN_OFFS[i + 1]]
    half = ROPE // 2
    kr = col(2)
    z = lambda n: jnp.zeros((D_MODEL, n), F32)
    wkr = jnp.concatenate([z(NOPE), kr, z(LANE - NOPE - ROPE)], axis=1)
    wq = p["w_q_up"][l].reshape(Q_LORA, N_HEADS, NOPE + ROPE)
    zq = lambda n: jnp.zeros((Q_LORA, N_HEADS, n), F32)
    x1, x2 = wq[:, :, NOPE:NOPE + half], wq[:, :, NOPE + half:]
    wqa = jnp.concatenate([wq[:, :, :NOPE], x1, x2, zq(LANE - NOPE - ROPE)], axis=2).reshape(Q_LORA, N_HEADS * LANE)
    wkv = p["w_kv_up"][l].reshape(KV_LORA, N_HEADS, 2 * NOPE)
    wk = jnp.concatenate([wkv[:, :, :NOPE], jnp.zeros((KV_LORA, N_HEADS, LANE - NOPE), F32)],
                         axis=2).reshape(KV_LORA, N_HEADS * LANE)
    wvt = wkv[:, :, NOPE:].reshape(KV_LORA, MIX).T
    bf = lambda a: a.astype(BF16)
    return dict(
        gmix=p["g_mix"][l][None], gq=p["g_q_lat"][l][None], gkv=p["g_kv_lat"][l][None],
        wlat=bf(jnp.concatenate([col(0), col(1)], axis=1)), wkr=bf(wkr), wut=bf(col(3).T),
        whqv=bf(jnp.concatenate([col(4), col(7)], axis=1)), whog=bf(col(8)),
        whff=bf(jnp.concatenate([col(5), col(6)], axis=1)), wgate=bf(col(9)),
        wqa=bf(wqa), wk=bf(wk), wvt=bf(wvt),
        wglut=bf(p["w_glu"][l].T), bglu=p["b_glu"][l][:, None], ghg=p["g_hg_out"][l][None],
        wbr=bf(p["w_branch"][l]), wout=bf(p["w_out"][l]),
        gffn=p["g_ffn"][l][None], wup=bf(p["w_ffn_up"][l]), wconv=p["w_ffn_conv"][l],
        bconv=p["b_ffn_conv"][l][None], wdn=bf(p["w_ffn_down"][l]),
    )


def _rope_tables(L):
    half = ROPE // 2
    inv_freq = 1.0 / (ROPE_BASE ** (jnp.arange(0, ROPE, 2, dtype=F32) / ROPE))
    ang = jnp.arange(L, dtype=F32)[:, None] * inv_freq[None, :]
    cos, sin = jnp.cos(ang), jnp.sin(ang)
    one, zero = jnp.ones((L, NOPE), F32), jnp.zeros((L, NOPE), F32)
    pad = jnp.zeros((L, LANE - NOPE - ROPE), F32)
    return (jnp.concatenate([one, cos, cos, pad], axis=1), jnp.concatenate([zero, -sin, sin, pad], axis=1))


def _trunk(x, mods, weights, s5ops, lb, g_final):
    b, L, _ = x.shape
    cos_t, sin_t = _rope_tables(L)
    for l in range(DEPTH):
        w = weights[l]
        q, k, vt, ut, hqv, hog, hff, gp = _inproj_call(x, mods[l], w, cos_t, sin_t)
        zat = _flash_call(q, k, vt)
        yt = _s5_call(ut, s5ops[l])
        of, ob = _hg_call(hqv, hff, lb[0, l][None], lb[1, l][None])
        x = _merge_call(x, mods[l], zat, yt, of, ob, hog, gp, w)
        x = _ffn_call(x, mods[l], w)
    return _final_call(x, g_final[None])


def kernel(x_prompt, x_sample, c_prompt, c_sample, w_ada, b_ada, g_mix, w_in, g_q_lat, w_q_up, g_kv_lat, w_kv_up,
           s5_lam_re, s5_lam_im, s5_log_dt, s5_b_re, s5_b_im, s5_c_re, s5_c_im, s5_d, w_glu, b_glu, hg_lb_logits,
           g_hg_out, w_branch, w_out, g_ffn, w_ffn_up, w_ffn_conv, b_ffn_conv, w_ffn_down, g_final):
    p = dict(g_mix=g_mix, w_in=w_in, g_q_lat=g_q_lat, w_q_up=w_q_up, g_kv_lat=g_kv_lat, w_kv_up=w_kv_up,
             w_glu=w_glu, b_glu=b_glu, g_hg_out=g_hg_out, w_branch=w_branch, w_out=w_out, g_ffn=g_ffn,
             w_ffn_up=w_ffn_up, w_ffn_conv=w_ffn_conv, b_ffn_conv=b_ffn_conv, w_ffn_down=w_ffn_down)
    depth = w_in.shape[0]
    assert depth == DEPTH
    bp, bs = c_prompt.shape[0], c_sample.shape[0]
    rows = -(-(bp + bs) // 8) * 8
    c_all = jnp.concatenate([c_prompt, c_sample, jnp.zeros((rows - bp - bs, D_MODEL), F32)], axis=0)
    mod_all = _ada_call(c_all, w_ada, b_ada)

    def mods_for(lo, n):
        m = mod_all[:, lo:lo + n].reshape(DEPTH, n, 6, D_MODEL)
        return jnp.pad(m, ((0, 0), (0, 0), (0, 2), (0, 0)))

    weights = [_layer_weights(p, l) for l in range(DEPTH)]
    s5ops = [_s5_operators(s5_lam_re[l], s5_lam_im[l], s5_log_dt[l], s5_b_re[l], s5_b_im[l],
                           s5_c_re[l], s5_c_im[l], s5_d[l]) for l in range(DEPTH)]
    gam = jax.nn.softmax(hg_lb_logits.astype(F32), axis=1)
    lb = jnp.cumsum(gam, axis=1) - gam[:, :1]
    y_prompt = _trunk(x_prompt, mods_for(0, bp), weights, s5ops, lb, g_final)
    y_sample = _trunk(x_sample, mods_for(bp, bs), weights, s5ops, lb, g_final)
    return (y_prompt, y_sample)
```

```python
import functools
import math

import jax
import jax.numpy as jnp
from jax import lax
from jax.experimental import pallas as pl
from jax.experimental.pallas import tpu as pltpu

F32 = jnp.float32
BF16 = jnp.bfloat16

D_MODEL = 1024
DEPTH = 4
MIX = 512
N_HEADS = 8
NOPE = 64
ROPE = 32
Q_LORA = 384
KV_LORA = 256
ROPE_BASE = 10000.0
S5_GROUPS = 32
S5_GROUP = 16
S5_STATE = 64
S5_CHUNK = 128
HG_HEADS = 4
HG_D = 128
HG_CHUNK = 64
HG_UNROLL = 4
D_FF = 2816
EPS = 1e-6
ATTN_SCALE = 1.0 / math.sqrt(NOPE + ROPE)
LOG2E = 1.4426950408889634
IN_OFFS = (0, 384, 640, 672, 1184, 1696, 2208, 2720, 3232, 3744, 6816)

LANE = 128
VMEM_LIMIT = 56 * 1024 * 1024

TL_IN = 512
TQ = 512
TK = 512
FLASH_UNROLL = 8
TL_HG = 512
TL_MG = 512
TL_FF = 512
FF_CW = 768
HALO = 8


def _dot(a, b):
    return jnp.dot(a, b, preferred_element_type=F32)


def _dot_nt(a, b):
    return lax.dot_general(a, b, (((1,), (1,)), ((), ())), preferred_element_type=F32)


def _dot_tn(a, b):
    return lax.dot_general(a, b, (((0,), (0,)), ((), ())), preferred_element_type=F32)


def _sigmoid(x):
    return 1.0 / (1.0 + jnp.exp(-x))


def _gate(x):
    return 0.5 * jnp.tanh(0.5 * x) + 0.5


def _rms(x):
    return x * lax.rsqrt(jnp.mean(x * x, axis=-1, keepdims=True) + EPS)


def _gelu_tanh(x):
    return 0.5 * x * (1.0 + jnp.tanh(math.sqrt(2.0 / math.pi) * (x + 0.044715 * (x * x * x))))


def _params(*sem):
    return pltpu.CompilerParams(dimension_semantics=sem, vmem_limit_bytes=VMEM_LIMIT)


def _const_spec(shape):
    nd = len(shape)
    return pl.BlockSpec(shape, lambda *_: (0,) * nd, pipeline_mode=pl.Buffered(1))


def _ada_kernel(c_ref, w_ref, b_ref, o_ref):
    c = c_ref[...]
    a = (c * _sigmoid(c)).astype(BF16)
    o_ref[0] = _dot(a, w_ref[0].astype(BF16)) + b_ref[0]


def _ada_call(c_all, w_ada, b_ada):
    nb = 1536
    rows = c_all.shape[0]
    return pl.pallas_call(
        _ada_kernel,
        grid=(DEPTH, 6 * D_MODEL // nb),
        in_specs=[pl.BlockSpec((rows, D_MODEL), lambda l, j: (0, 0)),
                  pl.BlockSpec((1, D_MODEL, nb), lambda l, j: (l, 0, j)),
                  pl.BlockSpec((1, 1, nb), lambda l, j: (l, 0, j))],
        out_specs=pl.BlockSpec((1, rows, nb), lambda l, j: (l, 0, j)),
        out_shape=jax.ShapeDtypeStruct((DEPTH, rows, 6 * D_MODEL), F32),
        compiler_params=_params("arbitrary", "arbitrary"),
        name="ada_mod",
    )(c_all, w_ada, b_ada.reshape(DEPTH, 1, 6 * D_MODEL))


def _inproj_kernel(x_ref, mod_ref, gmix_ref, cos_ref, sin_ref,
                   wlat_ref, wkr_ref, wut_ref, whqv_ref, whog_ref, whff_ref, wgate_ref,
                   gq_ref, gkv_ref, wqa_ref, wk_ref, wvt_ref,
                   q_ref, k_ref, vt_ref, ut_ref, hqv_ref, hog_ref, hff_ref, gp_ref):
    mod = mod_ref[0]
    h = _rms(x_ref[0]) * gmix_ref[...] * (1.0 + mod[1:2]) + mod[0:1]
    hb = h.astype(BF16)
    lat = _dot(hb, wlat_ref[...])
    qn = (_rms(lat[:, :Q_LORA]) * gq_ref[...]).astype(BF16)
    kvn = (_rms(lat[:, Q_LORA:]) * gkv_ref[...]).astype(BF16)
    cos = cos_ref[...]
    sin = sin_ref[...]
    lane = lax.broadcasted_iota(jnp.int32, sin.shape, 1)
    sin1 = jnp.where(lane < NOPE + ROPE // 2, sin, 0.0)
    sin2 = sin - sin1
    half = ROPE // 2

    def rotate(t, sl):
        down = pltpu.roll(t, t.shape[1] - half, 1)
        up = pltpu.roll(t, half, 1)
        return t[:, sl] * cos + down[:, sl] * sin1 + up[:, sl] * sin2

    qa = _dot(qn, wqa_ref[...])
    krp = _dot(hb, wkr_ref[...])
    kr = rotate(krp, slice(0, LANE))
    kn = _dot(kvn, wk_ref[...])
    qdn = pltpu.roll(qa, qa.shape[1] - half, 1)
    qup = pltpu.roll(qa, half, 1)
    for hd in range(N_HEADS):
        sl = slice(hd * LANE, (hd + 1) * LANE)
        q_ref[0, :, sl] = ((qa[:, sl] * cos + qdn[:, sl] * sin1 + qup[:, sl] * sin2)
                           * (ATTN_SCALE * LOG2E)).astype(BF16)
        k_ref[0, :, sl] = (kn[:, sl] + kr).astype(BF16)
    vt_ref[0] = _dot_nt(wvt_ref[...], kvn).astype(BF16)
    ut = _dot_nt(wut_ref[...], hb)
    for ch in range(ut.shape[1] // S5_CHUNK):
        ut_ref[0, :, ch * S5_GROUP:(ch + 1) * S5_GROUP, :] = (
            ut[:, ch * S5_CHUNK:(ch + 1) * S5_CHUNK].reshape(S5_GROUPS, S5_GROUP, S5_CHUNK))
    hqv_ref[0] = _dot(hb, whqv_ref[...]).astype(BF16)
    hog_ref[0] = _dot(hb, whog_ref[...]).astype(BF16)
    hff_ref[0] = _dot(hb, whff_ref[...])
    gp_ref[0] = _dot(hb, wgate_ref[...]).astype(BF16)


def _inproj_call(x, mod, w, cos_t, sin_t):
    b, L, _ = x.shape
    tl = TL_IN
    tok = lambda n: pl.BlockSpec((1, tl, n), lambda bi, i: (bi, i, 0))
    tr = lambda n: pl.BlockSpec((1, n, tl), lambda bi, i: (bi, 0, i))
    weights = [w["gmix"], None, None, w["wlat"], w["wkr"], w["wut"], w["whqv"], w["whog"], w["whff"], w["wgate"],
               w["gq"], w["gkv"], w["wqa"], w["wk"], w["wvt"]]
    in_specs = [tok(D_MODEL), pl.BlockSpec((1, 8, D_MODEL), lambda bi, i: (bi, 0, 0))]
    args = [x, mod]
    for a in weights:
        if a is None:
            continue
        in_specs.append(_const_spec(a.shape))
        args.append(a)
    in_specs[3:3] = [pl.BlockSpec((tl, LANE), lambda bi, i: (i, 0))] * 2
    args[3:3] = [cos_t, sin_t]
    out_shape = [jax.ShapeDtypeStruct((b, L, 1024), BF16), jax.ShapeDtypeStruct((b, L, 1024), BF16),
                 jax.ShapeDtypeStruct((b, MIX, L), BF16),
                 jax.ShapeDtypeStruct((b, S5_GROUPS, L // S5_CHUNK * S5_GROUP, S5_CHUNK), F32),
                 jax.ShapeDtypeStruct((b, L, 1024), BF16), jax.ShapeDtypeStruct((b, L, MIX), BF16),
                 jax.ShapeDtypeStruct((b, L, 1024), F32), jax.ShapeDtypeStruct((b, L, 3 * D_MODEL), BF16)]
    grp = pl.BlockSpec((1, S5_GROUPS, tl // S5_CHUNK * S5_GROUP, S5_CHUNK), lambda bi, i: (bi, 0, i, 0))
    out_specs = [tok(1024), tok(1024), tr(MIX), grp, tok(1024), tok(MIX), tok(1024), tok(3 * D_MODEL)]
    return pl.pallas_call(
        _inproj_kernel, grid=(b, L // tl), in_specs=in_specs, out_specs=out_specs, out_shape=out_shape,
        compiler_params=_params("parallel", "parallel"), name="inproj",
    )(*args)


def _flash_kernel(q_ref, k_ref, vt_ref, o_ref, s_ref, *, tk, nk):
    tq = q_ref.shape[1]
    half = NOPE
    ones = jnp.ones((16, tk), BF16)

    def scores(buf, j):
        off = pl.multiple_of(j * tk, tk)
        for hh in range(2):
            s_ref[buf, hh] = _dot_nt(k_ref[0, pl.ds(off, tk), hh * LANE:(hh + 1) * LANE],
                                     q_ref[0, :, hh * LANE:(hh + 1) * LANE])

    def process(buf, j, carry):
        off = pl.multiple_of(j * tk, tk)
        new = []
        for hh in range(2):
            m, acc = carry[hh]
            s = s_ref[buf, hh]
            mn = jnp.maximum(m, jnp.max(s, axis=0, keepdims=True))
            p = jnp.exp2(s - mn).astype(BF16)
            a = jnp.exp2(m - mn)
            v = jnp.concatenate([vt_ref[0, hh * half:(hh + 1) * half, pl.ds(off, tk)], ones], axis=0)
            new.append((mn, a * acc + _dot(v, p)))
        return tuple(new)

    unroll = FLASH_UNROLL if (nk - 1) // FLASH_UNROLL >= 2 else max(FLASH_UNROLL // 2, 2)

    def body(i, carry):
        for u in range(unroll):
            j = i * unroll + u
            scores(1 - u % 2, j + 1)
            carry = process(u % 2, j, carry)
        return carry

    scores(0, 0)
    init = tuple((jnp.full((1, tq), -1e30, F32), jnp.zeros((half + 16, tq), F32)) for _ in range(2))
    n_loop = (nk - 1) // unroll
    res = lax.fori_loop(0, n_loop, body, init)
    for j in range(n_loop * unroll, nk):
        if j + 1 < nk:
            scores(1 - j % 2, j + 1)
        res = process(j % 2, j, res)
    for hh in range(2):
        acc = res[hh][1]
        o_ref[0, hh * half:(hh + 1) * half, :] = (acc[:half] / acc[half:half + 1]).astype(BF16)


def _flash_call(q, k, vt):
    b, L, _ = q.shape
    tq, tk = min(TQ, L), min(TK, L)
    return pl.pallas_call(
        functools.partial(_flash_kernel, tk=tk, nk=L // tk),
        scratch_shapes=[pltpu.VMEM((2, 2, tk, tq), F32)],
        grid=(b, N_HEADS // 2, L // tq),
        in_specs=[pl.BlockSpec((1, tq, 2 * LANE), lambda bi, p, i: (bi, i, p)),
                  pl.BlockSpec((1, L, 2 * LANE), lambda bi, p, i: (bi, 0, p)),
                  pl.BlockSpec((1, LANE, L), lambda bi, p, i: (bi, p, 0))],
        out_specs=pl.BlockSpec((1, LANE, tq), lambda bi, p, i: (bi, p, i)),
        out_shape=jax.ShapeDtypeStruct((b, MIX, L), BF16),
        compiler_params=_params("parallel", "parallel", "arbitrary"), name="flash",
    )(q, k, vt)


def _s5_kernel(u_ref, t_ref, f_ref, e_ref, a_ref, y_ref, sloc_ref, st_ref, *, nb, nct):
    m = nb * nct
    ucat = jnp.concatenate(
        [jnp.concatenate([u_ref[bi, 0, pl.ds(hi, nct, stride=S5_GROUP), :].astype(BF16) for hi in range(S5_GROUP)],
                         axis=1) for bi in range(nb)], axis=0)
    y = _dot(ucat, t_ref[0])
    sloc_ref[...] = _dot(ucat, f_ref[0])
    a = a_ref[0]
    arf, aif, arb, aib = a[0:1], a[1:2], a[2:3], a[3:4]

    sb = 8 if nct % 8 == 0 else nct
    nblk = nct // sb

    def body(blk, carry):
        new = []
        for bi in range(nb):
            xrf, xif, xrb, xib = carry[bi]
            rf = pl.multiple_of(bi * nct + blk * sb, sb)
            rb = pl.multiple_of(bi * nct + (nblk - 1 - blk) * sb, sb)
            lf = sloc_ref[pl.ds(rf, sb), 0:2 * LANE]
            lb = sloc_ref[pl.ds(rb, sb), 2 * LANE:4 * LANE]
            frows, brows = [], []
            for r in range(sb):
                frows.append((xrf, xif))
                xrf, xif = (arf * xrf - aif * xif + lf[r:r + 1, 0:LANE],
                            arf * xif + aif * xrf + lf[r:r + 1, LANE:2 * LANE])
            for r in range(sb - 1, -1, -1):
                brows.append((xrb, xib))
                xrb, xib = (arb * xrb - aib * xib + lb[r:r + 1, 0:LANE],
                            arb * xib + aib * xrb + lb[r:r + 1, LANE:2 * LANE])
            brows = brows[::-1]
            st_ref[pl.ds(rf, sb), 0:LANE] = jnp.concatenate([t[0] for t in frows], axis=0)
            st_ref[pl.ds(rf, sb), LANE:2 * LANE] = jnp.concatenate([t[1] for t in frows], axis=0)
            st_ref[pl.ds(rb, sb), 2 * LANE:3 * LANE] = jnp.concatenate([t[0] for t in brows], axis=0)
            st_ref[pl.ds(rb, sb), 3 * LANE:4 * LANE] = jnp.concatenate([t[1] for t in brows], axis=0)
            new.append((xrf, xif, xrb, xib))
        return tuple(new)

    z = jnp.zeros((1, LANE), F32)
    lax.fori_loop(0, nblk, body, tuple((z, z, z, z) for _ in range(nb)))
    y = y + _dot(st_ref[...].astype(BF16), e_ref[0])
    for ho in range(S5_GROUP):
        for bi in range(nb):
            y_ref[bi, 0, pl.ds(ho, nct, stride=S5_GROUP), :] = y[bi * nct:(bi + 1) * nct,
                                                                 ho * S5_CHUNK:(ho + 1) * S5_CHUNK]


def _s5_call(u4, ops):
    b, _, rows, _ = u4.shape
    nct = rows // S5_GROUP
    gw = S5_GROUP * S5_CHUNK
    blk = pl.BlockSpec((b, 1, rows, S5_CHUNK), lambda g: (0, g, 0, 0))
    return pl.pallas_call(
        functools.partial(_s5_kernel, nb=b, nct=nct),
        grid=(S5_GROUPS,),
        in_specs=[blk,
                  pl.BlockSpec((1, gw, gw), lambda g: (g, 0, 0)),
                  pl.BlockSpec((1, gw, 4 * LANE), lambda g: (g, 0, 0)),
                  pl.BlockSpec((1, 4 * LANE, gw), lambda g: (g, 0, 0)),
                  pl.BlockSpec((1, 8, LANE), lambda g: (g, 0, 0))],
        out_specs=blk,
        out_shape=jax.ShapeDtypeStruct(u4.shape, F32),
        scratch_shapes=[pltpu.VMEM((b * nct, 4 * LANE), F32), pltpu.VMEM((b * nct, 4 * LANE), F32)],
        compiler_params=_params("parallel"), name="s5",
    )(u4, ops["t"], ops["f"], ops["e"], ops["a"])


def _toeplitz_kernel(w_ref, t_ref):
    c = S5_CHUNK

    def body(hi, _):
        r0 = pl.multiple_of(hi * c, c)
        for ho in range(S5_GROUP):
            row = w_ref[0, hi, ho:ho + 1, :]
            skew = pltpu.roll(jnp.broadcast_to(row, (c, 2 * c)), 0, 1, stride=1, stride_axis=0)
            t_ref[0, pl.ds(r0, c), ho * c:(ho + 1) * c] = skew[:, :c].astype(BF16)
        return 0

    lax.fori_loop(0, S5_GROUP, body, 0)


def _toeplitz_call(w):
    g = w.shape[0]
    gw = S5_GROUP * S5_CHUNK
    return pl.pallas_call(
        _toeplitz_kernel, grid=(g,),
        in_specs=[pl.BlockSpec((1, S5_GROUP, S5_GROUP, 2 * S5_CHUNK), lambda i: (i, 0, 0, 0))],
        out_specs=pl.BlockSpec((1, gw, gw), lambda i: (i, 0, 0)),
        out_shape=jax.ShapeDtypeStruct((g, gw, gw), BF16),
        compiler_params=_params("parallel"), name="s5_toeplitz",
    )(w)


def _s5_operators(lam_re, lam_im, log_dt, b_re, b_im, c_re, c_im, d):
    hp = lax.Precision.HIGHEST
    G, P, GS, C = S5_GROUPS, S5_STATE, S5_GROUP, S5_CHUNK
    lam_re, lam_im = lam_re.astype(F32), lam_im.astype(F32)
    dt = jnp.exp(log_dt.astype(F32))[..., None]
    zr, zi = lam_re * dt, lam_im * dt
    kk = jnp.arange(C + 1, dtype=F32)[None, None, :, None]
    mag = jnp.exp(zr[:, :, None, :] * kk)
    ang = zi[:, :, None, :] * kk
    pw_re, pw_im = mag * jnp.cos(ang), mag * jnp.sin(ang)
    lb_re, lb_im = pw_re[:, :, 1], pw_im[:, :, 1]
    den = lam_re * lam_re + lam_im * lam_im
    nr, ni = lb_re - 1.0, lb_im
    cr = (nr * lam_re + ni * lam_im) / den
    ci = (ni * lam_re - nr * lam_im) / den
    bb_re = cr[..., None] * b_re - ci[..., None] * b_im
    bb_im = cr[..., None] * b_im + ci[..., None] * b_re
    c_re, c_im = c_re.astype(F32), c_im.astype(F32)

    def kern(dr):
        pr, pi_ = pw_re[dr, :, :C], pw_im[dr, :, :C]
        cp_re = c_re[dr][:, None] * pr[:, :, None, :] - c_im[dr][:, None] * pi_[:, :, None, :]
        cp_im = c_re[dr][:, None] * pi_[:, :, None, :] + c_im[dr][:, None] * pr[:, :, None, :]
        return (jnp.einsum('gkhp,gpi->gkhi', cp_re, bb_re[dr], precision=hp)
                - jnp.einsum('gkhp,gpi->gkhi', cp_im, bb_im[dr], precision=hp))

    kf, kb = kern(0), kern(1)
    k0 = kf[:, 0] + kb[:, 0] + jnp.eye(GS, dtype=F32)[None] * d.astype(F32).reshape(G, GS)[:, :, None]
    kfull = jnp.concatenate([kb[:, 1:][:, ::-1], k0[:, None], kf[:, 1:], jnp.zeros((G, 1, GS, GS), F32)], axis=1)
    t_op = _toeplitz_call(jnp.roll(kfull, -(C - 1), axis=1).transpose(0, 3, 2, 1))

    def f_part(dr, idx):
        pr, pi_ = pw_re[dr][:, idx], pw_im[dr][:, idx]
        br, bi = bb_re[dr].transpose(0, 2, 1), bb_im[dr].transpose(0, 2, 1)
        re = pr[:, None] * br[:, :, None] - pi_[:, None] * bi[:, :, None]
        im = pr[:, None] * bi[:, :, None] + pi_[:, None] * br[:, :, None]
        return re.reshape(G, GS * C, P), im.reshape(G, GS * C, P)

    tau = jnp.arange(C)
    ffr, ffi = f_part(0, C - 1 - tau)
    fbr, fbi = f_part(1, tau)
    padl = lambda a: jnp.pad(a, ((0, 0), (0, 0), (0, LANE - P)))
    f_op = jnp.concatenate([padl(ffr), padl(ffi), padl(fbr), padl(fbi)], axis=-1).astype(BF16)

    def e_part(dr, idx):
        pr, pi_ = pw_re[dr][:, idx], pw_im[dr][:, idx]
        cr_, ci_ = c_re[dr].transpose(0, 2, 1), c_im[dr].transpose(0, 2, 1)
        prt, pit = pr.transpose(0, 2, 1), pi_.transpose(0, 2, 1)
        re = cr_[:, :, :, None] * prt[:, :, None, :] - ci_[:, :, :, None] * pit[:, :, None, :]
        im = cr_[:, :, :, None] * pit[:, :, None, :] + ci_[:, :, :, None] * prt[:, :, None, :]
        return re.reshape(G, P, GS * C), -im.reshape(G, P, GS * C)

    efr, efi = e_part(0, tau + 1)
    ebr, ebi = e_part(1, C - tau)
    padr = lambda a: jnp.pad(a, ((0, 0), (0, LANE - P), (0, 0)))
    e_op = jnp.concatenate([padr(efr), padr(efi), padr(ebr), padr(ebi)], axis=1).astype(BF16)

    arows = [pw_re[0, :, C], pw_im[0, :, C], pw_re[1, :, C], pw_im[1, :, C]]
    a_op = jnp.stack([jnp.pad(r, ((0, 0), (0, LANE - P))) for r in arows]
                     + [jnp.zeros((G, LANE), F32)] * 4, axis=1)
    return dict(t=t_op, f=f_op, e=e_op, a=a_op)


def _split3(x):
    hi = x.astype(BF16)
    r1 = x - hi.astype(F32)
    mid = r1.astype(BF16)
    lo = (r1 - mid.astype(F32)).astype(BF16)
    return hi, mid, lo


HG_HALVES = (32, 16, 8, 4, 2, 1)
HG_SEL_HALVES = (8, 4, 2)


def _hg_later(idx, h, fwd):
    return (idx % (2 * h) >= h) if fwd else (idx % (2 * h) < h)


def _hg_anchor(idx, h, fwd):
    base = idx // (2 * h) * (2 * h)
    return base + h if fwd else base + h - 1


def _hg_consts(fwd):
    c = HG_CHUNK
    t = lax.broadcasted_iota(jnp.int32, (c, c), 0)
    s = lax.broadcasted_iota(jnp.int32, (c, c), 1)
    tri = jnp.where((s <= t) if fwd else (s >= t), 1.0, 0.0).astype(BF16)
    masks = {h: (t // (2 * h) == s // (2 * h)) & _hg_later(t, h, fwd) & ~_hg_later(s, h, fwd) for h in HG_HALVES}
    rows = lax.broadcasted_iota(jnp.int32, (c, HG_D), 0)
    later = {h: _hg_later(rows, h, fwd) for h in HG_HALVES}
    sign = {h: jnp.where(later[h], 1.0, -1.0) for h in HG_HALVES}
    sr = lax.broadcasted_iota(jnp.int32, (len(HG_SEL_HALVES) * c, c), 0)
    sc = lax.broadcasted_iota(jnp.int32, (len(HG_SEL_HALVES) * c, c), 1)
    hit = jnp.zeros(sr.shape, jnp.bool_)
    for i, h in enumerate(HG_SEL_HALVES):
        hit = hit | ((sr // c == i) & (sc == _hg_anchor(sr % c, h, fwd)))
    sel = jnp.where(hit, 1.0, 0.0).astype(BF16)
    return dict(tri=tri, masks=masks, diag=(t == s), later=later, sign=sign, sel=sel)


def _hg_chunks(streams):
    c = HG_CHUNK

    gates = []
    for hq, hv, hf, lb, st, fwd, consts in streams:
        q = hq * _gate(hq)
        f = lb + (1.0 - lb) * _sigmoid(hf)
        logf = jnp.log(f)
        p1, p2, p3 = _split3(logf)
        tri = consts["tri"]
        cum = _dot(tri, p1) + _dot(tri, p2) + _dot(tri, p3)
        gates.append((q, 1.0 - f, f, logf, cum))

    refs = []
    for (hq, hv, hf, lb, st, fwd, consts), (q, k, f, logf, cum) in zip(streams, gates):
        cexc = cum - logf
        ref = {}
        for h in HG_HALVES:
            if 2 * h >= 16 and h not in HG_SEL_HALVES:
                ref[h] = jnp.concatenate(
                    [jnp.broadcast_to(cexc[a:a + 1, :], (2 * h, HG_D))
                     for a in (_hg_anchor(b0, h, fwd) for b0 in range(0, c, 2 * h))], axis=0)
        c_hi = cexc.astype(BF16)
        c_mid = (cexc - c_hi.astype(F32)).astype(BF16)
        gathered = _dot(consts["sel"], jnp.concatenate([c_hi, c_mid], axis=1))
        for i, h in enumerate(HG_SEL_HALVES):
            ref[h] = gathered[i * c:(i + 1) * c, :HG_D] + gathered[i * c:(i + 1) * c, HG_D:]
        refs.append(ref)

    scores = []
    for (hq, hv, hf, lb, st, fwd, consts), (q, k, f, logf, cum), ref in zip(streams, gates, refs):
        a = jnp.where(consts["diag"], _dot_nt(q.astype(BF16), k.astype(BF16)), 0.0)
        qf = q * f
        for h in HG_HALVES:
            late = consts["later"][h]
            if h == 1:
                y = jnp.where(late, qf, k)
            else:
                y = jnp.where(late, q, k) * jnp.exp((cum - ref[h]) * consts["sign"][h])
            y = y.astype(BF16)
            a = jnp.where(consts["masks"][h], _dot_nt(y, y), a)
        scores.append(a)

    outs = []
    for (hq, hv, hf, lb, st, fwd, consts), (q, k, f, logf, cum), a in zip(streams, gates, scores):
        if isinstance(st, int):
            st = outs[st][1]
        vb = hv.astype(BF16)
        o = _dot(a.astype(BF16), vb) + _dot_nt((q * jnp.exp(cum)).astype(BF16), st.astype(BF16))
        last = cum[c - 1:c] if fwd else cum[0:1]
        kst = (k * jnp.exp(last - cum)).astype(BF16)
        outs.append((o, st * jnp.exp(last) + _dot_tn(vb, kst)))
    return outs


def _hg_kernel(qvf_ref, ff_ref, qvb_ref, fb_ref, lbf_ref, lbb_ref, of_ref, ob_ref, sf_ref, sb_ref, *, ncc):
    @pl.when(pl.program_id(1) == 0)
    def _():
        sf_ref[...] = jnp.zeros_like(sf_ref)
        sb_ref[...] = jnp.zeros_like(sb_ref)

    cf = _hg_consts(True)
    cb = _hg_consts(False)
    unroll = HG_UNROLL if ncc % HG_UNROLL == 0 else 1
    per = 2 * HG_HEADS

    def body(it, _):
        streams, rows = [], []
        for u in range(unroll):
            cc = it * unroll + u
            rf = pl.multiple_of(cc * HG_CHUNK, HG_CHUNK)
            rb = pl.multiple_of((ncc - 1 - cc) * HG_CHUNK, HG_CHUNK)
            rows.append((rf, rb))
            for hd in range(HG_HEADS):
                sl = slice(hd * HG_D, (hd + 1) * HG_D)
                sv = slice(MIX + hd * HG_D, MIX + (hd + 1) * HG_D)
                prev = len(streams) - per
                streams.append((qvf_ref[0, pl.ds(rf, HG_CHUNK), sl].astype(F32),
                                qvf_ref[0, pl.ds(rf, HG_CHUNK), sv].astype(F32),
                                ff_ref[0, pl.ds(rf, HG_CHUNK), sl], lbf_ref[:, sl],
                                sf_ref[hd] if u == 0 else prev, True, cf))
                prev = len(streams) - per
                streams.append((qvb_ref[0, pl.ds(rb, HG_CHUNK), sl].astype(F32),
                                qvb_ref[0, pl.ds(rb, HG_CHUNK), sv].astype(F32),
                                fb_ref[0, pl.ds(rb, HG_CHUNK), sl], lbb_ref[:, sl],
                                sb_ref[hd] if u == 0 else prev, False, cb))
        outs = _hg_chunks(streams)
        for u, (rf, rb) in enumerate(rows):
            for hd in range(HG_HEADS):
                sl = slice(hd * HG_D, (hd + 1) * HG_D)
                of_ref[0, pl.ds(rf, HG_CHUNK), sl] = outs[u * per + 2 * hd][0].astype(BF16)
                ob_ref[0, pl.ds(rb, HG_CHUNK), sl] = outs[u * per + 2 * hd + 1][0].astype(BF16)
        for hd in range(HG_HEADS):
            sf_ref[hd] = outs[(unroll - 1) * per + 2 * hd][1]
            sb_ref[hd] = outs[(unroll - 1) * per + 2 * hd + 1][1]
        return 0

    lax.fori_loop(0, ncc // unroll, body, 0)


def _hg_call(hqv, hff, lbf, lbb):
    b, L, _ = hqv.shape
    tl = min(TL_HG, L)
    n = L // tl
    fwd = lambda w: pl.BlockSpec((1, tl, w), lambda bi, i: (bi, i, 0))
    bwd = lambda w, j: pl.BlockSpec((1, tl, w), lambda bi, i: (bi, n - 1 - i, j))
    return pl.pallas_call(
        functools.partial(_hg_kernel, ncc=tl // HG_CHUNK),
        grid=(b, n),
        in_specs=[fwd(2 * MIX), fwd(MIX), bwd(2 * MIX, 0), bwd(MIX, 1),
                  pl.BlockSpec((1, MIX), lambda bi, i: (0, 0)), pl.BlockSpec((1, MIX), lambda bi, i: (0, 0))],
        out_specs=[fwd(MIX), bwd(MIX, 0)],
        out_shape=[jax.ShapeDtypeStruct((b, L, MIX), BF16)] * 2,
        scratch_shapes=[pltpu.VMEM((HG_HEADS, HG_D, HG_D), F32), pltpu.VMEM((HG_HEADS, HG_D, HG_D), F32)],
        compiler_params=_params("parallel", "arbitrary"), name="hgrn2",
    )(hqv, hff, hqv, hff, lbf, lbb)


def _merge_kernel(x_ref, mod_ref, zat_ref, yt_ref, of_ref, ob_ref, og_ref, gp_ref,
                  wglut_ref, bglu_ref, ghg_ref, wbr_ref, wout_ref, o_ref):
    nch = yt_ref.shape[2] // S5_GROUP
    yt = jnp.concatenate([yt_ref[0, :, ch * S5_GROUP:(ch + 1) * S5_GROUP, :].reshape(MIX, S5_CHUNK)
                          for ch in range(nch)], axis=1)
    g = _gelu_tanh(yt)
    glu = _dot(wglut_ref[...], g.astype(BF16)) + bglu_ref[...]
    zbt = (g * _gate(glu)).astype(BF16)
    o = of_ref[0].astype(F32) + ob_ref[0].astype(F32)
    on = jnp.concatenate([_rms(o[:, hd * HG_D:(hd + 1) * HG_D]) for hd in range(HG_HEADS)], axis=1)
    og = og_ref[0].astype(F32)
    zc = (on * ghg_ref[...] * (og * _gate(og))).astype(BF16)
    bra = _dot_tn(zat_ref[0], wbr_ref[0])
    brb = _dot_tn(zbt, wbr_ref[1])
    brc = _dot(zc, wbr_ref[2])
    gp = gp_ref[0].astype(F32)
    mix = (_gate(gp[:, :D_MODEL]) * bra + _gate(gp[:, D_MODEL:2 * D_MODEL]) * brb
           + _gate(gp[:, 2 * D_MODEL:]) * brc)
    out = _dot(mix.astype(BF16), wout_ref[...])
    o_ref[0] = x_ref[0] + mod_ref[0][2:3] * out


def _merge_call(x, mod, zat, yt, of, ob, hog, gp, w):
    b, L, _ = x.shape
    tl = TL_MG
    tok = lambda n: pl.BlockSpec((1, tl, n), lambda bi, i: (bi, i, 0))
    tr = pl.BlockSpec((1, MIX, tl), lambda bi, i: (bi, 0, i))
    consts = [w["wglut"], w["bglu"], w["ghg"], w["wbr"], w["wout"]]
    return pl.pallas_call(
        _merge_kernel, grid=(b, L // tl),
        in_specs=[tok(D_MODEL), pl.BlockSpec((1, 8, D_MODEL), lambda bi, i: (bi, 0, 0)), tr,
                  pl.BlockSpec((1, S5_GROUPS, tl // S5_CHUNK * S5_GROUP, S5_CHUNK), lambda bi, i: (bi, 0, i, 0)),
                  tok(MIX), tok(MIX), tok(MIX), tok(3 * D_MODEL)] + [_const_spec(a.shape) for a in consts],
        out_specs=tok(D_MODEL),
        out_shape=jax.ShapeDtypeStruct(x.shape, F32),
        compiler_params=_params("parallel", "parallel"), name="merge",
    )(x, mod, zat, yt, of, ob, hog, gp, *consts)


def _ff_chunks():
    out, off = [], 0
    while off < D_FF:
        w = min(FF_CW, D_FF - off)
        out.append((off, w))
        off += w
    return out


def _ffn_kernel(x_ref, xp_ref, xn_ref, mod_ref, gffn_ref, wup_ref, wconv_ref, bconv_ref, wdn_ref, o_ref,
                up_ref, h_ref, acc_ref, *, tl):
    i = pl.program_id(1)
    mod = mod_ref[0]

    def modulate(xv):
        return _rms(xv) * gffn_ref[...] * (1.0 + mod[4:5]) + mod[3:4]

    keep_prev = jnp.where(i > 0, 1.0, 0.0)
    keep_next = jnp.where(i < pl.num_programs(1) - 1, 1.0, 0.0)
    h_ref[...] = jnp.concatenate([modulate(xp_ref[0]) * keep_prev, modulate(x_ref[0]),
                                  modulate(xn_ref[0]) * keep_next], axis=0).astype(BF16)
    chunks = _ff_chunks()

    def up_proj(ci):
        off, cw = chunks[ci]
        for part in range(2):
            col = off + part * D_FF
            up_ref[ci % 2, part, :, 0:cw] = _dot(h_ref[...], wup_ref[:, col:col + cw])

    up_proj(0)
    for ci, (off, cw) in enumerate(chunks):
        if ci + 1 < len(chunks):
            up_proj(ci + 1)
        act = None
        for part in range(2):
            col = off + part * D_FF
            wc = wconv_ref[:, col:col + cw]
            buf = up_ref.at[ci % 2, part]
            y = (buf[pl.ds(HALO - 1, tl), 0:cw] * wc[0:1] + buf[pl.ds(HALO, tl), 0:cw] * wc[1:2]
                 + buf[pl.ds(HALO + 1, tl), 0:cw] * wc[2:3] + bconv_ref[:, col:col + cw])
            act = y * _gate(y) if part == 0 else act * y
        dn = _dot(act.astype(BF16), wdn_ref[off:off + cw, :])
        if ci == 0:
            acc_ref[...] = dn
        else:
            acc_ref[...] += dn
    o_ref[0] = x_ref[0] + mod[5:6] * acc_ref[...]


def _ffn_call(x, mod, w):
    b, L, _ = x.shape
    tl = TL_FF
    hb = tl // HALO
    nh = L // HALO
    consts = [w["gffn"], w["wup"], w["wconv"], w["bconv"], w["wdn"]]
    return pl.pallas_call(
        functools.partial(_ffn_kernel, tl=tl), grid=(b, L // tl),
        in_specs=[pl.BlockSpec((1, tl, D_MODEL), lambda bi, i: (bi, i, 0)),
                  pl.BlockSpec((1, HALO, D_MODEL), lambda bi, i: (bi, jnp.maximum(i * hb - 1, 0), 0)),
                  pl.BlockSpec((1, HALO, D_MODEL), lambda bi, i: (bi, jnp.minimum((i + 1) * hb, nh - 1), 0)),
                  pl.BlockSpec((1, 8, D_MODEL), lambda bi, i: (bi, 0, 0))] + [_const_spec(a.shape) for a in consts],
        out_specs=pl.BlockSpec((1, tl, D_MODEL), lambda bi, i: (bi, i, 0)),
        out_shape=jax.ShapeDtypeStruct(x.shape, F32),
        scratch_shapes=[pltpu.VMEM((2, 2, tl + 2 * HALO, FF_CW), F32), pltpu.VMEM((tl + 2 * HALO, D_MODEL), BF16),
                        pltpu.VMEM((tl, D_MODEL), F32)],
        compiler_params=_params("parallel", "parallel"), name="ffn",
    )(x, x, x, mod, *consts)


def _final_kernel(x_ref, g_ref, o_ref):
    o_ref[0] = _rms(x_ref[0]) * g_ref[...]


def _final_call(x, g):
    b, L, _ = x.shape
    tl = 512
    spec = pl.BlockSpec((1, tl, D_MODEL), lambda bi, i: (bi, i, 0))
    return pl.pallas_call(
        _final_kernel, grid=(b, L // tl), in_specs=[spec, pl.BlockSpec((1, D_MODEL), lambda bi, i: (0, 0))],
        out_specs=spec, out_shape=jax.ShapeDtypeStruct(x.shape, F32),
        compiler_params=_params("parallel", "parallel"), name="final_norm",
    )(x, g)


def _layer_weights(p, l):
    w_in = p["w_in"][l]
    col = lambda i: w_in[:, IN_OFFS[i]:IN_OFFS[i + 1]]
    half = ROPE // 2
    kr = col(2)
    z = lambda n: jnp.zeros((D_MODEL, n), F32)
    wkr = jnp.concatenate([z(NOPE), kr, z(LANE - NOPE - ROPE)], axis=1)
    wq = p["w_q_up"][l].reshape(Q_LORA, N_HEADS, NOPE + ROPE)
    zq = lambda n: jnp.zeros((Q_LORA, N_HEADS, n), F32)
    x1, x2 = wq[:, :, NOPE:NOPE + half], wq[:, :, NOPE + half:]
    wqa = jnp.concatenate([wq[:, :, :NOPE], x1, x2, zq(LANE - NOPE - ROPE)], axis=2).reshape(Q_LORA, N_HEADS * LANE)
    wkv = p["w_kv_up"][l].reshape(KV_LORA, N_HEADS, 2 * NOPE)
    wk = jnp.concatenate([wkv[:, :, :NOPE], jnp.zeros((KV_LORA, N_HEADS, LANE - NOPE), F32)],
                         axis=2).reshape(KV_LORA, N_HEADS * LANE)
    wvt = wkv[:, :, NOPE:].reshape(KV_LORA, MIX).T
    bf = lambda a: a.astype(BF16)
    return dict(
        gmix=p["g_mix"][l][None], gq=p["g_q_lat"][l][None], gkv=p["g_kv_lat"][l][None],
        wlat=bf(jnp.concatenate([col(0), col(1)], axis=1)), wkr=bf(wkr), wut=bf(col(3).T),
        whqv=bf(jnp.concatenate([col(4), col(7)], axis=1)), whog=bf(col(8)),
        whff=bf(jnp.concatenate([col(5), col(6)], axis=1)), wgate=bf(col(9)),
        wqa=bf(wqa), wk=bf(wk), wvt=bf(wvt),
        wglut=bf(p["w_glu"][l].T), bglu=p["b_glu"][l][:, None], ghg=p["g_hg_out"][l][None],
        wbr=bf(p["w_branch"][l]), wout=bf(p["w_out"][l]),
        gffn=p["g_ffn"][l][None], wup=bf(p["w_ffn_up"][l]), wconv=p["w_ffn_conv"][l],
        bconv=p["b_ffn_conv"][l][None], wdn=bf(p["w_ffn_down"][l]),
    )


def _rope_tables(L):
    half = ROPE // 2
    inv_freq = 1.0 / (ROPE_BASE ** (jnp.arange(0, ROPE, 2, dtype=F32) / ROPE))
    ang = jnp.arange(L, dtype=F32)[:, None] * inv_freq[None, :]
    cos, sin = jnp.cos(ang), jnp.sin(ang)
    one, zero = jnp.ones((L, NOPE), F32), jnp.zeros((L, NOPE), F32)
    pad = jnp.zeros((L, LANE - NOPE - ROPE), F32)
    return (jnp.concatenate([one, cos, cos, pad], axis=1), jnp.concatenate([zero, -sin, sin, pad], axis=1))


def _trunk(x, mods, weights, s5ops, lb, g_final):
    b, L, _ = x.shape
    cos_t, sin_t = _rope_tables(L)
    for l in range(DEPTH):
        w = weights[l]
        q, k, vt, ut, hqv, hog, hff, gp = _inproj_call(x, mods[l], w, cos_t, sin_t)
        zat = _flash_call(q, k, vt)
        yt = _s5_call(ut, s5ops[l])
        of, ob = _hg_call(hqv, hff, lb[0, l][None], lb[1, l][None])
        x = _merge_call(x, mods[l], zat, yt, of, ob, hog, gp, w)
        x = _ffn_call(x, mods[l], w)
    return _final_call(x, g_final[None])


def kernel(x_prompt, x_sample, c_prompt, c_sample, w_ada, b_ada, g_mix, w_in, g_q_lat, w_q_up, g_kv_lat, w_kv_up,
           s5_lam_re, s5_lam_im, s5_log_dt, s5_b_re, s5_b_im, s5_c_re, s5_c_im, s5_d, w_glu, b_glu, hg_lb_logits,
           g_hg_out, w_branch, w_out, g_ffn, w_ffn_up, w_ffn_conv, b_ffn_conv, w_ffn_down, g_final):
    p = dict(g_mix=g_mix, w_in=w_in, g_q_lat=g_q_lat, w_q_up=w_q_up, g_kv_lat=g_kv_lat, w_kv_up=w_kv_up,
             w_glu=w_glu, b_glu=b_glu, g_hg_out=g_hg_out, w_branch=w_branch, w_out=w_out, g_ffn=g_ffn,
             w_ffn_up=w_ffn_up, w_ffn_conv=w_ffn_conv, b_ffn_conv=b_ffn_conv, w_ffn_down=w_ffn_down)
    depth = w_in.shape[0]
    assert depth == DEPTH
    bp, bs = c_prompt.shape[0], c_sample.shape[0]
    rows = -(-(bp + bs) // 8) * 8
    c_all = jnp.concatenate([c_prompt, c_sample, jnp.zeros((rows - bp - bs, D_MODEL), F32)], axis=0)
    mod_all = _ada_call(c_all, w_ada, b_ada)

    def mods_for(lo, n):
        m = mod_all[:, lo:lo + n].reshape(DEPTH, n, 6, D_MODEL)
        return jnp.pad(m, ((0, 0), (0, 0), (0, 2), (0, 0)))

    weights = [_layer_weights(p, l) for l in range(DEPTH)]
    s5ops = [_s5_operators(s5_lam_re[l], s5_lam_im[l], s5_log_dt[l], s5_b_re[l], s5_b_im[l],
                           s5_c_re[l], s5_c_im[l], s5_d[l]) for l in range(DEPTH)]
    gam = jax.nn.softmax(hg_lb_logits.astype(F32), axis=1)
    lb = jnp.cumsum(gam, axis=1) - gam[:, :1]
    y_prompt = _trunk(x_prompt, mods_for(0, bp), weights, s5ops, lb, g_final)
    y_sample = _trunk(x_sample, mods_for(bp, bs), weights, s5ops, lb, g_final)
    return (y_prompt, y_sample)
```

```python
import functools
import math

import jax
import jax.numpy as jnp
from jax import lax
from jax.experimental import pallas as pl
from jax.experimental.pallas import tpu as pltpu

F32 = jnp.float32
BF16 = jnp.bfloat16

D_MODEL = 1024
DEPTH = 4
MIX = 512
N_HEADS = 8
NOPE = 64
ROPE = 32
Q_LORA = 384
KV_LORA = 256
ROPE_BASE = 10000.0
S5_GROUPS = 32
S5_GROUP = 16
S5_STATE = 64
S5_CHUNK = 128
HG_HEADS = 4
HG_D = 128
HG_CHUNK = 64
HG_UNROLL = 4
D_FF = 2816
EPS = 1e-6
ATTN_SCALE = 1.0 / math.sqrt(NOPE + ROPE)
LOG2E = 1.4426950408889634
IN_OFFS = (0, 384, 640, 672, 1184, 1696, 2208, 2720, 3232, 3744, 6816)

LANE = 128
VMEM_LIMIT = 56 * 1024 * 1024

TL_IN = 512
TQ = 512
TK = 512
FLASH_UNROLL = 8
TL_HG = 512
TL_MG = 512
TL_FF = 512
FF_CW = 768
HALO = 8


def _dot(a, b):
    return jnp.dot(a, b, preferred_element_type=F32)


def _dot_nt(a, b):
    return lax.dot_general(a, b, (((1,), (1,)), ((), ())), preferred_element_type=F32)


def _dot_tn(a, b):
    return lax.dot_general(a, b, (((0,), (0,)), ((), ())), preferred_element_type=F32)


def _sigmoid(x):
    return 1.0 / (1.0 + jnp.exp(-x))


def _gate(x):
    return 0.5 * jnp.tanh(0.5 * x) + 0.5


def _rms(x):
    return x * lax.rsqrt(jnp.mean(x * x, axis=-1, keepdims=True) + EPS)


def _gelu_tanh(x):
    return 0.5 * x * (1.0 + jnp.tanh(math.sqrt(2.0 / math.pi) * (x + 0.044715 * (x * x * x))))


def _params(*sem):
    return pltpu.CompilerParams(dimension_semantics=sem, vmem_limit_bytes=VMEM_LIMIT)


def _const_spec(shape):
    nd = len(shape)
    return pl.BlockSpec(shape, lambda *_: (0,) * nd, pipeline_mode=pl.Buffered(1))


def _ada_kernel(c_ref, w_ref, b_ref, o_ref):
    c = c_ref[...]
    a = (c * _sigmoid(c)).astype(BF16)
    o_ref[0] = _dot(a, w_ref[0].astype(BF16)) + b_ref[0]


def _ada_call(c_all, w_ada, b_ada):
    nb = 1536
    rows = c_all.shape[0]
    return pl.pallas_call(
        _ada_kernel,
        grid=(DEPTH, 6 * D_MODEL // nb),
        in_specs=[pl.BlockSpec((rows, D_MODEL), lambda l, j: (0, 0)),
                  pl.BlockSpec((1, D_MODEL, nb), lambda l, j: (l, 0, j)),
                  pl.BlockSpec((1, 1, nb), lambda l, j: (l, 0, j))],
        out_specs=pl.BlockSpec((1, rows, nb), lambda l, j: (l, 0, j)),
        out_shape=jax.ShapeDtypeStruct((DEPTH, rows, 6 * D_MODEL), F32),
        compiler_params=_params("arbitrary", "arbitrary"),
        name="ada_mod",
    )(c_all, w_ada, b_ada.reshape(DEPTH, 1, 6 * D_MODEL))


def _inproj_kernel(x_ref, mod_ref, gmix_ref, cos_ref, sin_ref,
                   wlat_ref, wkr_ref, wut_ref, whqv_ref, whog_ref, whff_ref, wgate_ref,
                   gq_ref, gkv_ref, wqa_ref, wk_ref, wvt_ref,
                   q_ref, k_ref, vt_ref, ut_ref, hqv_ref, hog_ref, hff_ref, gp_ref):
    mod = mod_ref[0]
    h = _rms(x_ref[0]) * gmix_ref[...] * (1.0 + mod[1:2]) + mod[0:1]
    hb = h.astype(BF16)
    lat = _dot(hb, wlat_ref[...])
    qn = (_rms(lat[:, :Q_LORA]) * gq_ref[...]).astype(BF16)
    kvn = (_rms(lat[:, Q_LORA:]) * gkv_ref[...]).astype(BF16)
    cos = cos_ref[...]
    sin = sin_ref[...]
    lane = lax.broadcasted_iota(jnp.int32, sin.shape, 1)
    sin1 = jnp.where(lane < NOPE + ROPE // 2, sin, 0.0)
    sin2 = sin - sin1
    half = ROPE // 2

    def rotate(t, sl):
        down = pltpu.roll(t, t.shape[1] - half, 1)
        up = pltpu.roll(t, half, 1)
        return t[:, sl] * cos + down[:, sl] * sin1 + up[:, sl] * sin2

    qa = _dot(qn, wqa_ref[...])
    krp = _dot(hb, wkr_ref[...])
    kr = rotate(krp, slice(0, LANE))
    kn = _dot(kvn, wk_ref[...])
    qdn = pltpu.roll(qa, qa.shape[1] - half, 1)
    qup = pltpu.roll(qa, half, 1)
    for hd in range(N_HEADS):
        sl = slice(hd * LANE, (hd + 1) * LANE)
        q_ref[0, :, sl] = ((qa[:, sl] * cos + qdn[:, sl] * sin1 + qup[:, sl] * sin2)
                           * (ATTN_SCALE * LOG2E)).astype(BF16)
        k_ref[0, :, sl] = (kn[:, sl] + kr).astype(BF16)
    vt_ref[0] = _dot_nt(wvt_ref[...], kvn).astype(BF16)
    ut = _dot_nt(wut_ref[...], hb)
    for ch in range(ut.shape[1] // S5_CHUNK):
        ut_ref[0, :, ch * S5_GROUP:(ch + 1) * S5_GROUP, :] = (
            ut[:, ch * S5_CHUNK:(ch + 1) * S5_CHUNK].reshape(S5_GROUPS, S5_GROUP, S5_CHUNK))
    hqv_ref[0] = _dot(hb, whqv_ref[...]).astype(BF16)
    hog_ref[0] = _dot(hb, whog_ref[...]).astype(BF16)
    hff_ref[0] = _dot(hb, whff_ref[...])
    gp_ref[0] = _dot(hb, wgate_ref[...]).astype(BF16)


def _inproj_call(x, mod, w, cos_t, sin_t):
    b, L, _ = x.shape
    tl = TL_IN
    tok = lambda n: pl.BlockSpec((1, tl, n), lambda bi, i: (bi, i, 0))
    tr = lambda n: pl.BlockSpec((1, n, tl), lambda bi, i: (bi, 0, i))
    weights = [w["gmix"], None, None, w["wlat"], w["wkr"], w["wut"], w["whqv"], w["whog"], w["whff"], w["wgate"],
               w["gq"], w["gkv"], w["wqa"], w["wk"], w["wvt"]]
    in_specs = [tok(D_MODEL), pl.BlockSpec((1, 8, D_MODEL), lambda bi, i: (bi, 0, 0))]
    args = [x, mod]
    for a in weights:
        if a is None:
            continue
        in_specs.append(_const_spec(a.shape))
        args.append(a)
    in_specs[3:3] = [pl.BlockSpec((tl, LANE), lambda bi, i: (i, 0))] * 2
    args[3:3] = [cos_t, sin_t]
    out_shape = [jax.ShapeDtypeStruct((b, L, 1024), BF16), jax.ShapeDtypeStruct((b, L, 1024), BF16),
                 jax.ShapeDtypeStruct((b, MIX, L), BF16),
                 jax.ShapeDtypeStruct((b, S5_GROUPS, L // S5_CHUNK * S5_GROUP, S5_CHUNK), F32),
                 jax.ShapeDtypeStruct((b, L, 1024), BF16), jax.ShapeDtypeStruct((b, L, MIX), BF16),
                 jax.ShapeDtypeStruct((b, L, 1024), F32), jax.ShapeDtypeStruct((b, L, 3 * D_MODEL), BF16)]
    grp = pl.BlockSpec((1, S5_GROUPS, tl // S5_CHUNK * S5_GROUP, S5_CHUNK), lambda bi, i: (bi, 0, i, 0))
    out_specs = [tok(1024), tok(1024), tr(MIX), grp, tok(1024), tok(MIX), tok(1024), tok(3 * D_MODEL)]
    return pl.pallas_call(
        _inproj_kernel, grid=(b, L // tl), in_specs=in_specs, out_specs=out_specs, out_shape=out_shape,
        compiler_params=_params("parallel", "parallel"), name="inproj",
    )(*args)


def _flash_kernel(q_ref, k_ref, vt_ref, o_ref, s_ref, *, tk, nk):
    tq = q_ref.shape[1]
    half = NOPE
    ones = jnp.ones((16, tk), BF16)

    def scores(buf, j):
        off = pl.multiple_of(j * tk, tk)
        for hh in range(2):
            s_ref[buf, hh] = _dot_nt(k_ref[0, pl.ds(off, tk), hh * LANE:(hh + 1) * LANE],
                                     q_ref[0, :, hh * LANE:(hh + 1) * LANE])

    def process(buf, j, carry):
        off = pl.multiple_of(j * tk, tk)
        new = []
        for hh in range(2):
            m, acc = carry[hh]
            s = s_ref[buf, hh]
            mn = jnp.maximum(m, jnp.max(s, axis=0, keepdims=True))
            p = jnp.exp2(s - mn).astype(BF16)
            a = jnp.exp2(m - mn)
            v = jnp.concatenate([vt_ref[0, hh * half:(hh + 1) * half, pl.ds(off, tk)], ones], axis=0)
            new.append((mn, a * acc + _dot(v, p)))
        return tuple(new)

    unroll = FLASH_UNROLL if (nk - 1) // FLASH_UNROLL >= 2 else max(FLASH_UNROLL // 2, 2)

    def body(i, carry):
        for u in range(unroll):
            j = i * unroll + u
            scores(1 - u % 2, j + 1)
            carry = process(u % 2, j, carry)
        return carry

    scores(0, 0)
    init = tuple((jnp.full((1, tq), -1e30, F32), jnp.zeros((half + 16, tq), F32)) for _ in range(2))
    n_loop = (nk - 1) // unroll
    res = lax.fori_loop(0, n_loop, body, init)
    for j in range(n_loop * unroll, nk):
        if j + 1 < nk:
            scores(1 - j % 2, j + 1)
        res = process(j % 2, j, res)
    for hh in range(2):
        acc = res[hh][1]
        o_ref[0, hh * half:(hh + 1) * half, :] = (acc[:half] / acc[half:half + 1]).astype(BF16)


def _flash_call(q, k, vt):
    b, L, _ = q.shape
    tq, tk = min(TQ, L), min(TK, L)
    return pl.pallas_call(
        functools.partial(_flash_kernel, tk=tk, nk=L // tk),
        scratch_shapes=[pltpu.VMEM((2, 2, tk, tq), F32)],
        grid=(b, N_HEADS // 2, L // tq),
        in_specs=[pl.BlockSpec((1, tq, 2 * LANE), lambda bi, p, i: (bi, i, p)),
                  pl.BlockSpec((1, L, 2 * LANE), lambda bi, p, i: (bi, 0, p)),
                  pl.BlockSpec((1, LANE, L), lambda bi, p, i: (bi, p, 0))],
        out_specs=pl.BlockSpec((1, LANE, tq), lambda bi, p, i: (bi, p, i)),
        out_shape=jax.ShapeDtypeStruct((b, MIX, L), BF16),
        compiler_params=_params("parallel", "parallel", "arbitrary"), name="flash",
    )(q, k, vt)


def _s5_kernel(u_ref, t_ref, f_ref, e_ref, a_ref, y_ref, sloc_ref, st_ref, *, nb, nct):
    m = nb * nct
    ucat = jnp.concatenate(
        [jnp.concatenate([u_ref[bi, 0, pl.ds(hi, nct, stride=S5_GROUP), :].astype(BF16) for hi in range(S5_GROUP)],
                         axis=1) for bi in range(nb)], axis=0)
    y = _dot(ucat, t_ref[0])
    sloc_ref[...] = _dot(ucat, f_ref[0])
    a = a_ref[0]
    arf, aif, arb, aib = a[0:1], a[1:2], a[2:3], a[3:4]

    sb = 8 if nct % 8 == 0 else nct
    nblk = nct // sb

    def body(blk, carry):
        new = []
        for bi in range(nb):
            xrf, xif, xrb, xib = carry[bi]
            rf = pl.multiple_of(bi * nct + blk * sb, sb)
            rb = pl.multiple_of(bi * nct + (nblk - 1 - blk) * sb, sb)
            lf = sloc_ref[pl.ds(rf, sb), 0:2 * LANE]
            lb = sloc_ref[pl.ds(rb, sb), 2 * LANE:4 * LANE]
            frows, brows = [], []
            for r in range(sb):
                frows.append((xrf, xif))
                xrf, xif = (arf * xrf - aif * xif + lf[r:r + 1, 0:LANE],
                            arf * xif + aif * xrf + lf[r:r + 1, LANE:2 * LANE])
            for r in range(sb - 1, -1, -1):
                brows.append((xrb, xib))
                xrb, xib = (arb * xrb - aib * xib + lb[r:r + 1, 0:LANE],
                            arb * xib + aib * xrb + lb[r:r + 1, LANE:2 * LANE])
            brows = brows[::-1]
            st_ref[pl.ds(rf, sb), 0:LANE] = jnp.concatenate([t[0] for t in frows], axis=0)
            st_ref[pl.ds(rf, sb), LANE:2 * LANE] = jnp.concatenate([t[1] for t in frows], axis=0)
            st_ref[pl.ds(rb, sb), 2 * LANE:3 * LANE] = jnp.concatenate([t[0] for t in brows], axis=0)
            st_ref[pl.ds(rb, sb), 3 * LANE:4 * LANE] = jnp.concatenate([t[1] for t in brows], axis=0)
            new.append((xrf, xif, xrb, xib))
        return tuple(new)

    z = jnp.zeros((1, LANE), F32)
    lax.fori_loop(0, nblk, body, tuple((z, z, z, z) for _ in range(nb)))
    y = y + _dot(st_ref[...].astype(BF16), e_ref[0])
    for ho in range(S5_GROUP):
        for bi in range(nb):
            y_ref[bi, 0, pl.ds(ho, nct, stride=S5_GROUP), :] = y[bi * nct:(bi + 1) * nct,
                                                                 ho * S5_CHUNK:(ho + 1) * S5_CHUNK]


def _s5_call(u4, ops):
    b, _, rows, _ = u4.shape
    nct = rows // S5_GROUP
    gw = S5_GROUP * S5_CHUNK
    blk = pl.BlockSpec((b, 1, rows, S5_CHUNK), lambda g: (0, g, 0, 0))
    return pl.pallas_call(
        functools.partial(_s5_kernel, nb=b, nct=nct),
        grid=(S5_GROUPS,),
        in_specs=[blk,
                  pl.BlockSpec((1, gw, gw), lambda g: (g, 0, 0)),
                  pl.BlockSpec((1, gw, 4 * LANE), lambda g: (g, 0, 0)),
                  pl.BlockSpec((1, 4 * LANE, gw), lambda g: (g, 0, 0)),
                  pl.BlockSpec((1, 8, LANE), lambda g: (g, 0, 0))],
        out_specs=blk,
        out_shape=jax.ShapeDtypeStruct(u4.shape, F32),
        scratch_shapes=[pltpu.VMEM((b * nct, 4 * LANE), F32), pltpu.VMEM((b * nct, 4 * LANE), F32)],
        compiler_params=_params("parallel"), name="s5",
    )(u4, ops["t"], ops["f"], ops["e"], ops["a"])


def _toeplitz_kernel(w_ref, t_ref):
    c = S5_CHUNK

    def body(hi, _):
        r0 = pl.multiple_of(hi * c, c)
        for ho in range(S5_GROUP):
            row = w_ref[0, hi, ho:ho + 1, :]
            skew = pltpu.roll(jnp.broadcast_to(row, (c, 2 * c)), 0, 1, stride=1, stride_axis=0)
            t_ref[0, pl.ds(r0, c), ho * c:(ho + 1) * c] = skew[:, :c].astype(BF16)
        return 0

    lax.fori_loop(0, S5_GROUP, body, 0)


def _toeplitz_call(w):
    g = w.shape[0]
    gw = S5_GROUP * S5_CHUNK
    return pl.pallas_call(
        _toeplitz_kernel, grid=(g,),
        in_specs=[pl.BlockSpec((1, S5_GROUP, S5_GROUP, 2 * S5_CHUNK), lambda i: (i, 0, 0, 0))],
        out_specs=pl.BlockSpec((1, gw, gw), lambda i: (i, 0, 0)),
        out_shape=jax.ShapeDtypeStruct((g, gw, gw), BF16),
        compiler_params=_params("parallel"), name="s5_toeplitz",
    )(w)


def _s5_operators(lam_re, lam_im, log_dt, b_re, b_im, c_re, c_im, d):
    hp = lax.Precision.HIGHEST
    G, P, GS, C = S5_GROUPS, S5_STATE, S5_GROUP, S5_CHUNK
    lam_re, lam_im = lam_re.astype(F32), lam_im.astype(F32)
    dt = jnp.exp(log_dt.astype(F32))[..., None]
    zr, zi = lam_re * dt, lam_im * dt
    kk = jnp.arange(C + 1, dtype=F32)[None, None, :, None]
    mag = jnp.exp(zr[:, :, None, :] * kk)
    ang = zi[:, :, None, :] * kk
    pw_re, pw_im = mag * jnp.cos(ang), mag * jnp.sin(ang)
    lb_re, lb_im = pw_re[:, :, 1], pw_im[:, :, 1]
    den = lam_re * lam_re + lam_im * lam_im
    nr, ni = lb_re - 1.0, lb_im
    cr = (nr * lam_re + ni * lam_im) / den
    ci = (ni * lam_re - nr * lam_im) / den
    bb_re = cr[..., None] * b_re - ci[..., None] * b_im
    bb_im = cr[..., None] * b_im + ci[..., None] * b_re
    c_re, c_im = c_re.astype(F32), c_im.astype(F32)

    def kern(dr):
        pr, pi_ = pw_re[dr, :, :C], pw_im[dr, :, :C]
        cp_re = c_re[dr][:, None] * pr[:, :, None, :] - c_im[dr][:, None] * pi_[:, :, None, :]
        cp_im = c_re[dr][:, None] * pi_[:, :, None, :] + c_im[dr][:, None] * pr[:, :, None, :]
        return (jnp.einsum('gkhp,gpi->gkhi', cp_re, bb_re[dr], precision=hp)
                - jnp.einsum('gkhp,gpi->gkhi', cp_im, bb_im[dr], precision=hp))

    kf, kb = kern(0), kern(1)
    k0 = kf[:, 0] + kb[:, 0] + jnp.eye(GS, dtype=F32)[None] * d.astype(F32).reshape(G, GS)[:, :, None]
    kfull = jnp.concatenate([kb[:, 1:][:, ::-1], k0[:, None], kf[:, 1:], jnp.zeros((G, 1, GS, GS), F32)], axis=1)
    t_op = _toeplitz_call(jnp.roll(kfull, -(C - 1), axis=1).transpose(0, 3, 2, 1))

    def f_part(dr, idx):
        pr, pi_ = pw_re[dr][:, idx], pw_im[dr][:, idx]
        br, bi = bb_re[dr].transpose(0, 2, 1), bb_im[dr].transpose(0, 2, 1)
        re = pr[:, None] * br[:, :, None] - pi_[:, None] * bi[:, :, None]
        im = pr[:, None] * bi[:, :, None] + pi_[:, None] * br[:, :, None]
        return re.reshape(G, GS * C, P), im.reshape(G, GS * C, P)

    tau = jnp.arange(C)
    ffr, ffi = f_part(0, C - 1 - tau)
    fbr, fbi = f_part(1, tau)
    padl = lambda a: jnp.pad(a, ((0, 0), (0, 0), (0, LANE - P)))
    f_op = jnp.concatenate([padl(ffr), padl(ffi), padl(fbr), padl(fbi)], axis=-1).astype(BF16)

    def e_part(dr, idx):
        pr, pi_ = pw_re[dr][:, idx], pw_im[dr][:, idx]
        cr_, ci_ = c_re[dr].transpose(0, 2, 1), c_im[dr].transpose(0, 2, 1)
        prt, pit = pr.transpose(0, 2, 1), pi_.transpose(0, 2, 1)
        re = cr_[:, :, :, None] * prt[:, :, None, :] - ci_[:, :, :, None] * pit[:, :, None, :]
        im = cr_[:, :, :, None] * pit[:, :, None, :] + ci_[:, :, :, None] * prt[:, :, None, :]
        return re.reshape(G, P, GS * C), -im.reshape(G, P, GS * C)

    efr, efi = e_part(0, tau + 1)
    ebr, ebi = e_part(1, C - tau)
    padr = lambda a: jnp.pad(a, ((0, 0), (0, LANE - P), (0, 0)))
    e_op = jnp.concatenate([padr(efr), padr(efi), padr(ebr), padr(ebi)], axis=1).astype(BF16)

    arows = [pw_re[0, :, C], pw_im[0, :, C], pw_re[1, :, C], pw_im[1, :, C]]
    a_op = jnp.stack([jnp.pad(r, ((0, 0), (0, LANE - P))) for r in arows]
                     + [jnp.zeros((G, LANE), F32)] * 4, axis=1)
    return dict(t=t_op, f=f_op, e=e_op, a=a_op)


def _split3(x):
    hi = x.astype(BF16)
    r1 = x - hi.astype(F32)
    mid = r1.astype(BF16)
    lo = (r1 - mid.astype(F32)).astype(BF16)
    return hi, mid, lo


HG_HALVES = (32, 16, 8, 4, 2, 1)
HG_SEL_HALVES = (8, 4, 2)


def _hg_later(idx, h, fwd):
    return (idx % (2 * h) >= h) if fwd else (idx % (2 * h) < h)


def _hg_anchor(idx, h, fwd):
    base = idx // (2 * h) * (2 * h)
    return base + h if fwd else base + h - 1


def _hg_consts(fwd):
    c = HG_CHUNK
    t = lax.broadcasted_iota(jnp.int32, (c, c), 0)
    s = lax.broadcasted_iota(jnp.int32, (c, c), 1)
    tri = jnp.where((s <= t) if fwd else (s >= t), 1.0, 0.0).astype(BF16)
    masks = {h: (t // (2 * h) == s // (2 * h)) & _hg_later(t, h, fwd) & ~_hg_later(s, h, fwd) for h in HG_HALVES}
    rows = lax.broadcasted_iota(jnp.int32, (c, HG_D), 0)
    later = {h: _hg_later(rows, h, fwd) for h in HG_HALVES}
    sign = {h: jnp.where(later[h], 1.0, -1.0) for h in HG_HALVES}
    sr = lax.broadcasted_iota(jnp.int32, (len(HG_SEL_HALVES) * c, c), 0)
    sc = lax.broadcasted_iota(jnp.int32, (len(HG_SEL_HALVES) * c, c), 1)
    hit = jnp.zeros(sr.shape, jnp.bool_)
    for i, h in enumerate(HG_SEL_HALVES):
        hit = hit | ((sr // c == i) & (sc == _hg_anchor(sr % c, h, fwd)))
    sel = jnp.where(hit, 1.0, 0.0).astype(BF16)
    return dict(tri=tri, masks=masks, diag=(t == s), later=later, sign=sign, sel=sel)


def _hg_chunks(streams):
    c = HG_CHUNK

    gates = []
    for hq, hv, hf, lb, st, fwd, consts in streams:
        q = hq * _gate(hq)
        f = lb + (1.0 - lb) * _sigmoid(hf)
        logf = jnp.log(f)
        p1, p2, p3 = _split3(logf)
        tri = consts["tri"]
        cum = _dot(tri, p1) + _dot(tri, p2) + _dot(tri, p3)
        gates.append((q, 1.0 - f, f, logf, cum))

    refs = []
    for (hq, hv, hf, lb, st, fwd, consts), (q, k, f, logf, cum) in zip(streams, gates):
        cexc = cum - logf
        ref = {}
        for h in HG_HALVES:
            if 2 * h >= 16 and h not in HG_SEL_HALVES:
                ref[h] = jnp.concatenate(
                    [jnp.broadcast_to(cexc[a:a + 1, :], (2 * h, HG_D))
                     for a in (_hg_anchor(b0, h, fwd) for b0 in range(0, c, 2 * h))], axis=0)
        c_hi = cexc.astype(BF16)
        c_mid = (cexc - c_hi.astype(F32)).astype(BF16)
        gathered = _dot(consts["sel"], jnp.concatenate([c_hi, c_mid], axis=1))
        for i, h in enumerate(HG_SEL_HALVES):
            ref[h] = gathered[i * c:(i + 1) * c, :HG_D] + gathered[i * c:(i + 1) * c, HG_D:]
        refs.append(ref)

    scores = []
    for (hq, hv, hf, lb, st, fwd, consts), (q, k, f, logf, cum), ref in zip(streams, gates, refs):
        a = jnp.where(consts["diag"], _dot_nt(q.astype(BF16), k.astype(BF16)), 0.0)
        qf = q * f
        for h in HG_HALVES:
            late = consts["later"][h]
            if h == 1:
                y = jnp.where(late, qf, k)
            else:
                y = jnp.where(late, q, k) * jnp.exp((cum - ref[h]) * consts["sign"][h])
            y = y.astype(BF16)
            a = jnp.where(consts["masks"][h], _dot_nt(y, y), a)
        scores.append(a)

    outs = []
    for (hq, hv, hf, lb, st, fwd, consts), (q, k, f, logf, cum), a in zip(streams, gates, scores):
        if isinstance(st, int):
            st = outs[st][1]
        vb = hv.astype(BF16)
        o = _dot(a.astype(BF16), vb) + _dot_nt((q * jnp.exp(cum)).astype(BF16), st.astype(BF16))
        last = cum[c - 1:c] if fwd else cum[0:1]
        kst = (k * jnp.exp(last - cum)).astype(BF16)
        outs.append((o, st * jnp.exp(last) + _dot_tn(vb, kst)))
    return outs


def _hg_kernel(qvf_ref, ff_ref, qvb_ref, fb_ref, lbf_ref, lbb_ref, of_ref, ob_ref, sf_ref, sb_ref, *, ncc):
    @pl.when(pl.program_id(1) == 0)
    def _():
        sf_ref[...] = jnp.zeros_like(sf_ref)
        sb_ref[...] = jnp.zeros_like(sb_ref)

    cf = _hg_consts(True)
    cb = _hg_consts(False)
    unroll = HG_UNROLL if ncc % HG_UNROLL == 0 else 1
    per = 2 * HG_HEADS

    def body(it, _):
        streams, rows = [], []
        for u in range(unroll):
            cc = it * unroll + u
            rf = pl.multiple_of(cc * HG_CHUNK, HG_CHUNK)
            rb = pl.multiple_of((ncc - 1 - cc) * HG_CHUNK, HG_CHUNK)
            rows.append((rf, rb))
            for hd in range(HG_HEADS):
                sl = slice(hd * HG_D, (hd + 1) * HG_D)
                sv = slice(MIX + hd * HG_D, MIX + (hd + 1) * HG_D)
                prev = len(streams) - per
                streams.append((qvf_ref[0, pl.ds(rf, HG_CHUNK), sl].astype(F32),
                                qvf_ref[0, pl.ds(rf, HG_CHUNK), sv].astype(F32),
                                ff_ref[0, pl.ds(rf, HG_CHUNK), sl], lbf_ref[:, sl],
                                sf_ref[hd] if u == 0 else prev, True, cf))
                prev = len(streams) - per
                streams.append((qvb_ref[0, pl.ds(rb, HG_CHUNK), sl].astype(F32),
                                qvb_ref[0, pl.ds(rb, HG_CHUNK), sv].astype(F32),
                                fb_ref[0, pl.ds(rb, HG_CHUNK), sl], lbb_ref[:, sl],
                                sb_ref[hd] if u == 0 else prev, False, cb))
        outs = _hg_chunks(streams)
        for u, (rf, rb) in enumerate(rows):
            for hd in range(HG_HEADS):
                sl = slice(hd * HG_D, (hd + 1) * HG_D)
                of_ref[0, pl.ds(rf, HG_CHUNK), sl] = outs[u * per + 2 * hd][0].astype(BF16)
                ob_ref[0, pl.ds(rb, HG_CHUNK), sl] = outs[u * per + 2 * hd + 1][0].astype(BF16)
        for hd in range(HG_HEADS):
            sf_ref[hd] = outs[(unroll - 1) * per + 2 * hd][1]
            sb_ref[hd] = outs[(unroll - 1) * per + 2 * hd + 1][1]
        return 0

    lax.fori_loop(0, ncc // unroll, body, 0)


def _hg_call(hqv, hff, lbf, lbb):
    b, L, _ = hqv.shape
    tl = min(TL_HG, L)
    n = L // tl
    fwd = lambda w: pl.BlockSpec((1, tl, w), lambda bi, i: (bi, i, 0))
    bwd = lambda w, j: pl.BlockSpec((1, tl, w), lambda bi, i: (bi, n - 1 - i, j))
    return pl.pallas_call(
        functools.partial(_hg_kernel, ncc=tl // HG_CHUNK),
        grid=(b, n),
        in_specs=[fwd(2 * MIX), fwd(MIX), bwd(2 * MIX, 0), bwd(MIX, 1),
                  pl.BlockSpec((1, MIX), lambda bi, i: (0, 0)), pl.BlockSpec((1, MIX), lambda bi, i: (0, 0))],
        out_specs=[fwd(MIX), bwd(MIX, 0)],
        out_shape=[jax.ShapeDtypeStruct((b, L, MIX), BF16)] * 2,
        scratch_shapes=[pltpu.VMEM((HG_HEADS, HG_D, HG_D), F32), pltpu.VMEM((HG_HEADS, HG_D, HG_D), F32)],
        compiler_params=_params("parallel", "arbitrary"), name="hgrn2",
    )(hqv, hff, hqv, hff, lbf, lbb)


def _merge_kernel(x_ref, mod_ref, zat_ref, yt_ref, of_ref, ob_ref, og_ref, gp_ref,
                  wglut_ref, bglu_ref, ghg_ref, wbr_ref, wout_ref, o_ref):
    nch = yt_ref.shape[2] // S5_GROUP
    yt = jnp.concatenate([yt_ref[0, :, ch * S5_GROUP:(ch + 1) * S5_GROUP, :].reshape(MIX, S5_CHUNK)
                          for ch in range(nch)], axis=1)
    g = _gelu_tanh(yt)
    glu = _dot(wglut_ref[...], g.astype(BF16)) + bglu_ref[...]
    zbt = (g * _gate(glu)).astype(BF16)
    o = of_ref[0].astype(F32) + ob_ref[0].astype(F32)
    on = jnp.concatenate([_rms(o[:, hd * HG_D:(hd + 1) * HG_D]) for hd in range(HG_HEADS)], axis=1)
    og = og_ref[0].astype(F32)
    zc = (on * ghg_ref[...] * (og * _gate(og))).astype(BF16)
    bra = _dot_tn(zat_ref[0], wbr_ref[0])
    brb = _dot_tn(zbt, wbr_ref[1])
    brc = _dot(zc, wbr_ref[2])
    gp = gp_ref[0].astype(F32)
    mix = (_gate(gp[:, :D_MODEL]) * bra + _gate(gp[:, D_MODEL:2 * D_MODEL]) * brb
           + _gate(gp[:, 2 * D_MODEL:]) * brc)
    out = _dot(mix.astype(BF16), wout_ref[...])
    o_ref[0] = x_ref[0] + mod_ref[0][2:3] * out


def _merge_call(x, mod, zat, yt, of, ob, hog, gp, w):
    b, L, _ = x.shape
    tl = TL_MG
    tok = lambda n: pl.BlockSpec((1, tl, n), lambda bi, i: (bi, i, 0))
    tr = pl.BlockSpec((1, MIX, tl), lambda bi, i: (bi, 0, i))
    consts = [w["wglut"], w["bglu"], w["ghg"], w["wbr"], w["wout"]]
    return pl.pallas_call(
        _merge_kernel, grid=(b, L // tl),
        in_specs=[tok(D_MODEL), pl.BlockSpec((1, 8, D_MODEL), lambda bi, i: (bi, 0, 0)), tr,
                  pl.BlockSpec((1, S5_GROUPS, tl // S5_CHUNK * S5_GROUP, S5_CHUNK), lambda bi, i: (bi, 0, i, 0)),
                  tok(MIX), tok(MIX), tok(MIX), tok(3 * D_MODEL)] + [_const_spec(a.shape) for a in consts],
        out_specs=tok(D_MODEL),
        out_shape=jax.ShapeDtypeStruct(x.shape, F32),
        compiler_params=_params("parallel", "parallel"), name="merge",
    )(x, mod, zat, yt, of, ob, hog, gp, *consts)


def _ff_chunks():
    out, off = [], 0
    while off < D_FF:
        w = min(FF_CW, D_FF - off)
        out.append((off, w))
        off += w
    return out


def _ffn_kernel(x_ref, xp_ref, xn_ref, mod_ref, gffn_ref, wup_ref, wconv_ref, bconv_ref, wdn_ref, gfin_ref, o_ref,
                up_ref, h_ref, acc_ref, *, tl, final):
    i = pl.program_id(1)
    mod = mod_ref[0]

    def modulate(xv):
        return _rms(xv) * gffn_ref[...] * (1.0 + mod[4:5]) + mod[3:4]

    keep_prev = jnp.where(i > 0, 1.0, 0.0)
    keep_next = jnp.where(i < pl.num_programs(1) - 1, 1.0, 0.0)
    h_ref[...] = jnp.concatenate([modulate(xp_ref[0]) * keep_prev, modulate(x_ref[0]),
                                  modulate(xn_ref[0]) * keep_next], axis=0).astype(BF16)
    chunks = _ff_chunks()

    def up_proj(ci):
        off, cw = chunks[ci]
        for part in range(2):
            col = off + part * D_FF
            up_ref[ci % 2, part, :, 0:cw] = _dot(h_ref[...], wup_ref[:, col:col + cw])

    up_proj(0)
    for ci, (off, cw) in enumerate(chunks):
        if ci + 1 < len(chunks):
            up_proj(ci + 1)
        act = None
        for part in range(2):
            col = off + part * D_FF
            wc = wconv_ref[:, col:col + cw]
            buf = up_ref.at[ci % 2, part]
            y = (buf[pl.ds(HALO - 1, tl), 0:cw] * wc[0:1] + buf[pl.ds(HALO, tl), 0:cw] * wc[1:2]
                 + buf[pl.ds(HALO + 1, tl), 0:cw] * wc[2:3] + bconv_ref[:, col:col + cw])
            act = y * _gate(y) if part == 0 else act * y
        dn = _dot(act.astype(BF16), wdn_ref[off:off + cw, :])
        if ci == 0:
            acc_ref[...] = dn
        else:
            acc_ref[...] += dn
    out = x_ref[0] + mod[5:6] * acc_ref[...]
    o_ref[0] = _rms(out) * gfin_ref[...] if final else out


def _ffn_call(x, mod, w, g_final, final):
    b, L, _ = x.shape
    tl = TL_FF
    hb = tl // HALO
    nh = L // HALO
    consts = [w["gffn"], w["wup"], w["wconv"], w["bconv"], w["wdn"], g_final]
    return pl.pallas_call(
        functools.partial(_ffn_kernel, tl=tl, final=final), grid=(b, L // tl),
        in_specs=[pl.BlockSpec((1, tl, D_MODEL), lambda bi, i: (bi, i, 0)),
                  pl.BlockSpec((1, HALO, D_MODEL), lambda bi, i: (bi, jnp.maximum(i * hb - 1, 0), 0)),
                  pl.BlockSpec((1, HALO, D_MODEL), lambda bi, i: (bi, jnp.minimum((i + 1) * hb, nh - 1), 0)),
                  pl.BlockSpec((1, 8, D_MODEL), lambda bi, i: (bi, 0, 0))] + [_const_spec(a.shape) for a in consts],
        out_specs=pl.BlockSpec((1, tl, D_MODEL), lambda bi, i: (bi, i, 0)),
        out_shape=jax.ShapeDtypeStruct(x.shape, F32),
        scratch_shapes=[pltpu.VMEM((2, 2, tl + 2 * HALO, FF_CW), F32), pltpu.VMEM((tl + 2 * HALO, D_MODEL), BF16),
                        pltpu.VMEM((tl, D_MODEL), F32)],
        compiler_params=_params("parallel", "parallel"), name="ffn",
    )(x, x, x, mod, *consts)


def _layer_weights(p, l):
    w_in = p["w_in"][l]
    col = lambda i: w_in[:, IN_OFFS[i]:IN_OFFS[i + 1]]
    half = ROPE // 2
    kr = col(2)
    z = lambda n: jnp.zeros((D_MODEL, n), F32)
    wkr = jnp.concatenate([z(NOPE), kr, z(LANE - NOPE - ROPE)], axis=1)
    wq = p["w_q_up"][l].reshape(Q_LORA, N_HEADS, NOPE + ROPE)
    zq = lambda n: jnp.zeros((Q_LORA, N_HEADS, n), F32)
    x1, x2 = wq[:, :, NOPE:NOPE + half], wq[:, :, NOPE + half:]
    wqa = jnp.concatenate([wq[:, :, :NOPE], x1, x2, zq(LANE - NOPE - ROPE)], axis=2).reshape(Q_LORA, N_HEADS * LANE)
    wkv = p["w_kv_up"][l].reshape(KV_LORA, N_HEADS, 2 * NOPE)
    wk = jnp.concatenate([wkv[:, :, :NOPE], jnp.zeros((KV_LORA, N_HEADS, LANE - NOPE), F32)],
                         axis=2).reshape(KV_LORA, N_HEADS * LANE)
    wvt = wkv[:, :, NOPE:].reshape(KV_LORA, MIX).T
    bf = lambda a: a.astype(BF16)
    return dict(
        gmix=p["g_mix"][l][None], gq=p["g_q_lat"][l][None], gkv=p["g_kv_lat"][l][None],
        wlat=bf(jnp.concatenate([col(0), col(1)], axis=1)), wkr=bf(wkr), wut=bf(col(3).T),
        whqv=bf(jnp.concatenate([col(4), col(7)], axis=1)), whog=bf(col(8)),
        whff=bf(jnp.concatenate([col(5), col(6)], axis=1)), wgate=bf(col(9)),
        wqa=bf(wqa), wk=bf(wk), wvt=bf(wvt),
        wglut=bf(p["w_glu"][l].T), bglu=p["b_glu"][l][:, None], ghg=p["g_hg_out"][l][None],
        wbr=bf(p["w_branch"][l]), wout=bf(p["w_out"][l]),
        gffn=p["g_ffn"][l][None], wup=bf(p["w_ffn_up"][l]), wconv=p["w_ffn_conv"][l],
        bconv=p["b_ffn_conv"][l][None], wdn=bf(p["w_ffn_down"][l]),
    )


def _rope_tables(L):
    half = ROPE // 2
    inv_freq = 1.0 / (ROPE_BASE ** (jnp.arange(0, ROPE, 2, dtype=F32) / ROPE))
    ang = jnp.arange(L, dtype=F32)[:, None] * inv_freq[None, :]
    cos, sin = jnp.cos(ang), jnp.sin(ang)
    one, zero = jnp.ones((L, NOPE), F32), jnp.zeros((L, NOPE), F32)
    pad = jnp.zeros((L, LANE - NOPE - ROPE), F32)
    return (jnp.concatenate([one, cos, cos, pad], axis=1), jnp.concatenate([zero, -sin, sin, pad], axis=1))


def _trunk(x, mods, weights, s5ops, lb, g_final):
    b, L, _ = x.shape
    cos_t, sin_t = _rope_tables(L)
    for l in range(DEPTH):
        w = weights[l]
        q, k, vt, ut, hqv, hog, hff, gp = _inproj_call(x, mods[l], w, cos_t, sin_t)
        zat = _flash_call(q, k, vt)
        yt = _s5_call(ut, s5ops[l])
        of, ob = _hg_call(hqv, hff, lb[0, l][None], lb[1, l][None])
        x = _merge_call(x, mods[l], zat, yt, of, ob, hog, gp, w)
        x = _ffn_call(x, mods[l], w, g_final[None], final=(l == DEPTH - 1))
    return x


def kernel(x_prompt, x_sample, c_prompt, c_sample, w_ada, b_ada, g_mix, w_in, g_q_lat, w_q_up, g_kv_lat, w_kv_up,
           s5_lam_re, s5_lam_im, s5_log_dt, s5_b_re, s5_b_im, s5_c_re, s5_c_im, s5_d, w_glu, b_glu, hg_lb_logits,
           g_hg_out, w_branch, w_out, g_ffn, w_ffn_up, w_ffn_conv, b_ffn_conv, w_ffn_down, g_final):
    p = dict(g_mix=g_mix, w_in=w_in, g_q_lat=g_q_lat, w_q_up=w_q_up, g_kv_lat=g_kv_lat, w_kv_up=w_kv_up,
             w_glu=w_glu, b_glu=b_glu, g_hg_out=g_hg_out, w_branch=w_branch, w_out=w_out, g_ffn=g_ffn,
             w_ffn_up=w_ffn_up, w_ffn_conv=w_ffn_conv, b_ffn_conv=b_ffn_conv, w_ffn_down=w_ffn_down)
    depth = w_in.shape[0]
    assert depth == DEPTH
    bp, bs = c_prompt.shape[0], c_sample.shape[0]
    rows = -(-(bp + bs) // 8) * 8
    c_all = jnp.concatenate([c_prompt, c_sample, jnp.zeros((rows - bp - bs, D_MODEL), F32)], axis=0)
    mod_all = _ada_call(c_all, w_ada, b_ada)

    def mods_for(lo, n):
        m = mod_all[:, lo:lo + n].reshape(DEPTH, n, 6, D_MODEL)
        return jnp.pad(m, ((0, 0), (0, 0), (0, 2), (0, 0)))

    weights = [_layer_weights(p, l) for l in range(DEPTH)]
    s5ops = [_s5_operators(s5_lam_re[l], s5_lam_im[l], s5_log_dt[l], s5_b_re[l], s5_b_im[l],
                           s5_c_re[l], s5_c_im[l], s5_d[l]) for l in range(DEPTH)]
    gam = jax.nn.softmax(hg_lb_logits.astype(F32), axis=1)
    lb = jnp.cumsum(gam, axis=1) - gam[:, :1]
    y_prompt = _trunk(x_prompt, mods_for(0, bp), weights, s5ops, lb, g_final)
    y_sample = _trunk(x_sample, mods_for(bp, bs), weights, s5ops, lb, g_final)
    return (y_prompt, y_sample)
```
